```python
import math
import jax, jax.numpy as jnp
from jax import lax
import numpy as np

D_MODEL = 1024
BATCH = 2
SEQ = 8192
DEPTH = 4
DEC_BATCH = 128
DEC_SEQ = 1
PAST_LEN = 8192
PAGE_SIZE = 128

N_AB_LAYERS = (DEPTH + 1) // 2
N_SSD_LAYERS = DEPTH // 2

CONF_CH = D_MODEL // 2
CONF_KERNEL = 31
HEAD_DIM = 64
N_Q_HEADS = (D_MODEL // 2) // HEAD_DIM
N_KV_HEADS = 2
Q_PER_KV = N_Q_HEADS // N_KV_HEADS
WINDOW = 128
ATT_BLOCK = 128
ATT_SCALE = HEAD_DIM ** -0.5
AB_IN = 2 * CONF_CH + (N_Q_HEADS + 2 * N_KV_HEADS) * HEAD_DIM
AB_MIX = CONF_CH + N_Q_HEADS * HEAD_DIM
SSD_INNER = 2 * D_MODEL
SSD_HEAD_DIM = 64
SSD_HEADS = SSD_INNER // SSD_HEAD_DIM
SSD_GROUPS = 4
SSD_HPG = SSD_HEADS // SSD_GROUPS
SSD_STATE = 128
SSD_CONV = 4
SSD_CONV_CH = SSD_INNER + 2 * SSD_GROUPS * SSD_STATE
SSD_CHUNK = 128
SSD_IN = SSD_INNER + SSD_CONV_CH + SSD_HEADS
D_FF = 4 * D_MODEL
RMS_EPS = 1e-6
LN_EPS = 1e-5

kernel_name = 'hybrid_conv_swa_ssd_decoder_step'


def _rmsnorm(x, g):
    xf = x.astype(jnp.float32)
    y = xf * lax.rsqrt(jnp.mean(xf * xf, axis=-1, keepdims=True) + RMS_EPS)
    return (y * g.astype(jnp.float32)).astype(x.dtype)


def _layernorm(x, g, b):
    xf = x.astype(jnp.float32)
    xc = xf - jnp.mean(xf, axis=-1, keepdims=True)
    var = jnp.mean(xc * xc, axis=-1, keepdims=True)
    y = xc * lax.rsqrt(var + LN_EPS) * g.astype(jnp.float32) + b.astype(jnp.float32)
    return y.astype(x.dtype)


def _dwconv_valid(xp, w, b):
    y = lax.conv_general_dilated(xp, w[:, None, :].astype(xp.dtype), window_strides=(1,), padding='VALID',
                                 dimension_numbers=('NWC', 'WIO', 'NWC'), feature_group_count=xp.shape[-1])
    return y + b.astype(xp.dtype)


def _sink_softmax(s, mask, sinks):
    sink = sinks.astype(jnp.float32)[:, :, None, None]
    s = jnp.where(mask, s, -jnp.inf)
    m = jnp.maximum(jnp.max(s, axis=-1, keepdims=True), sink)
    p = jnp.exp(s - m)
    return p / (jnp.sum(p, axis=-1, keepdims=True) + jnp.exp(sink - m))


def _conformer_conv(u, hist, dw_w, dw_b, ln_g, ln_b):
    a = u[..., :CONF_CH] * jax.nn.sigmoid(u[..., CONF_CH:])
    ap = jnp.concatenate([hist.astype(a.dtype), a], axis=1)
    c = _dwconv_valid(ap, dw_w, dw_b)
    return jax.nn.silu(_layernorm(c, ln_g, ln_b)), ap


def _swa_prompt(q, k, v, sinks):
    bsz, t = q.shape[:2]
    nb = t // ATT_BLOCK
    qb = q.reshape(bsz, nb, ATT_BLOCK, N_KV_HEADS, Q_PER_KV, HEAD_DIM)
    kb = k.reshape(bsz, nb, ATT_BLOCK, N_KV_HEADS, HEAD_DIM)
    vb = v.reshape(bsz, nb, ATT_BLOCK, N_KV_HEADS, HEAD_DIM)
    kk = jnp.concatenate([jnp.concatenate([jnp.zeros_like(kb[:, :1]), kb[:, :-1]], axis=1), kb], axis=2)
    vv = jnp.concatenate([jnp.concatenate([jnp.zeros_like(vb[:, :1]), vb[:, :-1]], axis=1), vb], axis=2)
    s = jnp.einsum('bnqkgd,bnskd->bnkgqs', qb, kk).astype(jnp.float32) * ATT_SCALE
    qpos = ATT_BLOCK + jnp.arange(ATT_BLOCK)[:, None]
    kpos = jnp.arange(2 * ATT_BLOCK)[None, :]
    band = (kpos <= qpos) & (qpos - kpos < WINDOW)
    has_prev = (jnp.arange(nb) > 0)[:, None, None] | (kpos >= ATT_BLOCK)[None]
    mask = (band[None] & has_prev)[None, :, None, None]
    p = _sink_softmax(s, mask, sinks.reshape(N_KV_HEADS, Q_PER_KV))
    o = jnp.einsum('bnkgqs,bnskd->bnqkgd', p.astype(v.dtype), vv)
    return o.reshape(bsz, t, N_Q_HEADS * HEAD_DIM)


def _swa_sample(q, k, v, k_cache, v_cache, sinks):
    bsz, t = q.shape[:2]
    wc = k_cache.shape[1]
    kk = jnp.concatenate([k_cache.astype(k.dtype), k], axis=1)
    vv = jnp.concatenate([v_cache.astype(v.dtype), v], axis=1)
    qg = q.reshape(bsz, t, N_KV_HEADS, Q_PER_KV, HEAD_DIM)
    s = jnp.einsum('bqkgd,bskd->bkgqs', qg, kk).astype(jnp.float32) * ATT_SCALE
    qpos = wc + jnp.arange(t)[:, None]
    kpos = jnp.arange(wc + t)[None, :]
    mask = (kpos <= qpos) & (qpos - kpos < WINDOW)
    p = _sink_softmax(s, mask, sinks.reshape(N_KV_HEADS, Q_PER_KV))
    o = jnp.einsum('bkgqs,bskd->bqkgd', p.astype(v.dtype), vv)
    return o.reshape(bsz, t, N_Q_HEADS * HEAD_DIM)


def _ssd_chunked(x, dt, a, bm, cm):
    bsz, t = x.shape[:2]
    nc = t // SSD_CHUNK
    L = SSD_CHUNK
    xc = (x.astype(jnp.float32) * dt[..., None]).reshape(bsz, nc, L, SSD_GROUPS, SSD_HPG, SSD_HEAD_DIM)
    cs = jnp.cumsum((dt * a).reshape(bsz, nc, L, SSD_GROUPS, SSD_HPG), axis=2)
    bc = bm.astype(jnp.float32).reshape(bsz, nc, L, SSD_GROUPS, SSD_STATE)
    cc = cm.astype(jnp.float32).reshape(bsz, nc, L, SSD_GROUPS, SSD_STATE)
    causal = jnp.tril(jnp.ones((L, L), dtype=bool))[:, :, None, None]
    decay = jnp.exp(jnp.where(causal, cs[:, :, :, None] - cs[:, :, None, :], -jnp.inf))
    cb = jnp.einsum('bclgn,bcsgn->bclsg', cc, bc)
    y_diag = jnp.einsum('bclsgk,bcsgkp->bclgkp', cb[..., None] * decay, xc)
    decay_to_end = jnp.exp(cs[:, :, -1:] - cs)
    states = jnp.einsum('bclgn,bclgkp->bcgkpn', bc, xc * decay_to_end[..., None])
    chunk_decay = jnp.exp(cs[:, :, -1])

    def step(h, inp):
        st, dcy = inp
        return h * dcy[..., None, None] + st, h

    h0 = jnp.zeros((bsz, SSD_GROUPS, SSD_HPG, SSD_HEAD_DIM, SSD_STATE), jnp.float32)
    h_final, h_in = lax.scan(step, h0, (jnp.moveaxis(states, 1, 0), jnp.moveaxis(chunk_decay, 1, 0)))
    h_in = jnp.moveaxis(h_in, 0, 1)
    y_off = jnp.einsum('bclgn,bcgkpn->bclgkp', cc, h_in) * jnp.exp(cs)[..., None]
    y = (y_diag + y_off).reshape(bsz, t, SSD_HEADS, SSD_HEAD_DIM)
    return y.astype(x.dtype), h_final.reshape(bsz, SSD_HEADS, SSD_HEAD_DIM, SSD_STATE).astype(x.dtype)


def _ssd_recurrent(x, dt, a, bm, cm, h0):
    bsz, t = x.shape[:2]
    xg = (x.astype(jnp.float32) * dt[..., None]).reshape(bsz, t, SSD_GROUPS, SSD_HPG, SSD_HEAD_DIM)
    da = jnp.exp(dt * a).reshape(bsz, t, SSD_GROUPS, SSD_HPG)
    h = h0.astype(jnp.float32).reshape(bsz, SSD_GROUPS, SSD_HPG, SSD_HEAD_DIM, SSD_STATE)

    def step(h, inp):
        xt, dat, bt, ct = inp
        h = h * dat[..., None, None] + jnp.einsum('bgkp,bgn->bgkpn', xt, bt)
        return h, jnp.einsum('bgkpn,bgn->bgkp', h, ct)

    h, ys = lax.scan(step, h, (jnp.moveaxis(xg, 1, 0), jnp.moveaxis(da, 1, 0),
                               jnp.moveaxis(bm.astype(jnp.float32), 1, 0), jnp.moveaxis(cm.astype(jnp.float32), 1, 0)))
    y = jnp.moveaxis(ys, 0, 1).reshape(bsz, t, SSD_HEADS, SSD_HEAD_DIM)
    return y.astype(x.dtype), h.reshape(bsz, SSD_HEADS, SSD_HEAD_DIM, SSD_STATE).astype(x.dtype)


def _ssd_mixer(h, hist, w_in, conv_w, conv_b, dt_bias, a_log, d_skip, norm_g, w_out, h0):
    bsz, t, _ = h.shape
    u = h @ w_in
    z = u[..., :SSD_INNER]
    xbc = u[..., SSD_INNER:SSD_INNER + SSD_CONV_CH]
    dt_raw = u[..., SSD_INNER + SSD_CONV_CH:]
    xbc_p = jnp.concatenate([hist.astype(xbc.dtype), xbc], axis=1)
    xbc_c = jax.nn.silu(_dwconv_valid(xbc_p, conv_w, conv_b))
    gn = SSD_GROUPS * SSD_STATE
    xs = xbc_c[..., :SSD_INNER].reshape(bsz, t, SSD_HEADS, SSD_HEAD_DIM)
    bm = xbc_c[..., SSD_INNER:SSD_INNER + gn].reshape(bsz, t, SSD_GROUPS, SSD_STATE)
    cm = xbc_c[..., SSD_INNER + gn:].reshape(bsz, t, SSD_GROUPS, SSD_STATE)
    dt = jax.nn.softplus(dt_raw.astype(jnp.float32) + dt_bias.astype(jnp.float32))
    a = -jnp.exp(a_log.astype(jnp.float32))
    if h0 is None:
        y, state = _ssd_chunked(xs, dt, a, bm, cm)
    else:
        y, state = _ssd_recurrent(xs, dt, a, bm, cm, h0)
    y = y + d_skip[:, None].astype(y.dtype) * xs
    y = (y.reshape(bsz, t, SSD_INNER) * jax.nn.silu(z)).reshape(bsz, t, SSD_GROUPS, SSD_INNER // SSD_GROUPS)
    y = _rmsnorm(y, norm_g.reshape(SSD_GROUPS, SSD_INNER // SSD_GROUPS)).reshape(bsz, t, SSD_INNER)
    return y @ w_out, state, xbc_p


def _trunk(x, P, cache):
    prompt = cache is None
    bsz, t, _ = x.shape
    win_k, win_v, conf_rows, ssm_states, ssd_rows = [], [], [], [], []
    qw = N_Q_HEADS * HEAD_DIM
    kw = N_KV_HEADS * HEAD_DIM
    o = 2 * CONF_CH
    for layer in range(DEPTH):
        g = P['norm_g'][layer]
        i = layer // 2
        h = _rmsnorm(x, g[0])
        if layer % 2 == 0:
            u = h @ P['ab_w_in'][i]
            q = u[..., o:o + qw].reshape(bsz, t, N_Q_HEADS, HEAD_DIM)
            k = u[..., o + qw:o + qw + kw].reshape(bsz, t, N_KV_HEADS, HEAD_DIM)
            v = u[..., o + qw + kw:].reshape(bsz, t, N_KV_HEADS, HEAD_DIM)
            if prompt:
                hist = jnp.zeros((bsz, CONF_KERNEL - 1, CONF_CH), x.dtype)
            else:
                hist = cache[2][i]
            c_out, ap = _conformer_conv(u[..., :o], hist, P['conf_dw_w'][i], P['conf_dw_b'][i],
                                        P['conf_ln_g'][i], P['conf_ln_b'][i])
            if prompt:
                a_out = _swa_prompt(q, k, v, P['attn_sinks'][i])
                wp = min(WINDOW, t)
                win_k.append(k[:, t - wp:])
                win_v.append(v[:, t - wp:])
                conf_rows.append(ap[:, ap.shape[1] - (CONF_KERNEL - 1):])
            else:
                a_out = _swa_sample(q, k, v, cache[0][i], cache[1][i], P['attn_sinks'][i])
                win_k.append(k)
                win_v.append(v)
                conf_rows.append(ap[:, CONF_KERNEL - 1:])
            mix = jnp.concatenate([c_out, a_out], axis=-1) @ P['ab_w_out'][i]
        else:
            if prompt:
                hist = jnp.zeros((bsz, SSD_CONV - 1, SSD_CONV_CH), x.dtype)
                h0 = None
            else:
                hist = cache[4][i]
                h0 = cache[3][i]
            mix, state, xbc_p = _ssd_mixer(h, hist, P['ssd_w_in'][i], P['ssd_conv_w'][i], P['ssd_conv_b'][i],
                                           P['ssd_dt_bias'][i], P['ssd_a_log'][i], P['ssd_d'][i],
                                           P['ssd_norm_g'][i], P['ssd_w_out'][i], h0)
            ssm_states.append(state)
            if prompt:
                ssd_rows.append(xbc_p[:, xbc_p.shape[1] - (SSD_CONV - 1):])
            else:
                ssd_rows.append(xbc_p[:, SSD_CONV - 1:])
        x = x + _rmsnorm(mix, g[1])
        h = _rmsnorm(x, g[2])
        f = jnp.square(jax.nn.relu(h @ P['mlp_w_up'][layer])) @ P['mlp_w_down'][layer]
        x = x + _rmsnorm(f, g[3])
    return x, jnp.stack(win_k), jnp.stack(win_v), jnp.stack(conf_rows), jnp.stack(ssm_states), jnp.stack(ssd_rows)


def setup_inputs(seed: int = 0) -> dict:
    key = jax.random.key(seed)
    ks = jax.random.split(key, 26)
    f32 = jnp.float32

    def nrm(k, shape, scale):
        return jax.random.normal(k, shape, f32) * scale

    wc = min(WINDOW, PAST_LEN)
    dt0 = jnp.exp(jax.random.uniform(ks[16], (N_SSD_LAYERS, SSD_HEADS), f32, math.log(1e-3), math.log(1e-1)))
    return {
        'x_prompt': nrm(ks[0], (BATCH, SEQ, D_MODEL), 1.0),
        'x_sample': nrm(ks[1], (DEC_BATCH, DEC_SEQ, D_MODEL), 1.0),
        'cache_win_k': nrm(ks[2], (N_AB_LAYERS, DEC_BATCH, wc, N_KV_HEADS, HEAD_DIM), 1.0),
        'cache_win_v': nrm(ks[3], (N_AB_LAYERS, DEC_BATCH, wc, N_KV_HEADS, HEAD_DIM), 1.0),
        'state_conf_conv': nrm(ks[4], (N_AB_LAYERS, DEC_BATCH, CONF_KERNEL - 1, CONF_CH), 0.5),
        'state_ssm': nrm(ks[5], (N_SSD_LAYERS, DEC_BATCH, SSD_HEADS, SSD_HEAD_DIM, SSD_STATE), 0.1),
        'state_ssd_conv': nrm(ks[6], (N_SSD_LAYERS, DEC_BATCH, SSD_CONV - 1, SSD_CONV_CH), 1.0),
        'norm_g': 1.0 + nrm(ks[7], (DEPTH, 4, D_MODEL), 0.05),
        'ab_w_in': nrm(ks[8], (N_AB_LAYERS, D_MODEL, AB_IN), D_MODEL ** -0.5),
        'conf_dw_w': nrm(ks[9], (N_AB_LAYERS, CONF_KERNEL, CONF_CH), CONF_KERNEL ** -0.5),
        'conf_dw_b': nrm(ks[10], (N_AB_LAYERS, CONF_CH), 0.01),
        'conf_ln_g': 1.0 + nrm(ks[11], (N_AB_LAYERS, CONF_CH), 0.05),
        'conf_ln_b': nrm(ks[12], (N_AB_LAYERS, CONF_CH), 0.01),
        'attn_sinks': nrm(ks[13], (N_AB_LAYERS, N_Q_HEADS), 0.5),
        'ab_w_out': nrm(ks[14], (N_AB_LAYERS, AB_MIX, D_MODEL), AB_MIX ** -0.5),
        'ssd_w_in': nrm(ks[15], (N_SSD_LAYERS, D_MODEL, SSD_IN), D_MODEL ** -0.5),
        'ssd_conv_w': nrm(ks[17], (N_SSD_LAYERS, SSD_CONV, SSD_CONV_CH), SSD_CONV ** -0.5),
        'ssd_conv_b': nrm(ks[18], (N_SSD_LAYERS, SSD_CONV_CH), 0.01),
        'ssd_dt_bias': dt0 + jnp.log(-jnp.expm1(-dt0)),
        'ssd_a_log': jnp.log(jax.random.uniform(ks[19], (N_SSD_LAYERS, SSD_HEADS), f32, 1.0, 16.0)),
        'ssd_d': 1.0 + nrm(ks[20], (N_SSD_LAYERS, SSD_HEADS), 0.05),
        'ssd_norm_g': 1.0 + nrm(ks[21], (N_SSD_LAYERS, SSD_INNER), 0.05),
        'ssd_w_out': nrm(ks[22], (N_SSD_LAYERS, SSD_INNER, D_MODEL), SSD_INNER ** -0.5),
        'mlp_w_up': nrm(ks[23], (DEPTH, D_MODEL, D_FF), D_MODEL ** -0.5),
        'mlp_w_down': nrm(ks[24], (DEPTH, D_FF, D_MODEL), D_FF ** -0.5),
    }


def reference(x_prompt, x_sample, cache_win_k, cache_win_v, state_conf_conv, state_ssm, state_ssd_conv,
              norm_g, ab_w_in, conf_dw_w, conf_dw_b, conf_ln_g, conf_ln_b, attn_sinks, ab_w_out,
              ssd_w_in, ssd_conv_w, ssd_conv_b, ssd_dt_bias, ssd_a_log, ssd_d, ssd_norm_g, ssd_w_out,
              mlp_w_up, mlp_w_down):
    P = {'norm_g': norm_g, 'ab_w_in': ab_w_in, 'conf_dw_w': conf_dw_w, 'conf_dw_b': conf_dw_b,
         'conf_ln_g': conf_ln_g, 'conf_ln_b': conf_ln_b, 'attn_sinks': attn_sinks, 'ab_w_out': ab_w_out,
         'ssd_w_in': ssd_w_in, 'ssd_conv_w': ssd_conv_w, 'ssd_conv_b': ssd_conv_b, 'ssd_dt_bias': ssd_dt_bias,
         'ssd_a_log': ssd_a_log, 'ssd_d': ssd_d, 'ssd_norm_g': ssd_norm_g, 'ssd_w_out': ssd_w_out,
         'mlp_w_up': mlp_w_up, 'mlp_w_down': mlp_w_down}
    y_prompt, wk_p, wv_p, cc_p, ssm_p, sc_p = _trunk(x_prompt, P, None)
    y_sample, wk_s, wv_s, cc_s, ssm_s, sc_s = _trunk(
        x_sample, P, (cache_win_k, cache_win_v, state_conf_conv, state_ssm, state_ssd_conv))
    return (y_prompt, y_sample, wk_p, wv_p, wk_s, wv_s, cc_p, cc_s, ssm_p, ssm_s, sc_p, sc_s)
```

```python
import functools

import jax
import jax.numpy as jnp
from jax import lax
from jax.experimental import pallas as pl
from jax.experimental.pallas import tpu as pltpu

F32 = jnp.float32
BF16 = jnp.bfloat16

D_MODEL = 1024
CONF_CH = 512
CONF_KERNEL = 31
HEAD_DIM = 64
N_Q_HEADS = 8
N_KV_HEADS = 2
Q_PER_KV = N_Q_HEADS // N_KV_HEADS
WINDOW = 128
ATT_BLOCK = 128
ATT_SCALE = HEAD_DIM ** -0.5
Q_W = N_Q_HEADS * HEAD_DIM
KV_W = N_KV_HEADS * HEAD_DIM
GLU_W = 2 * CONF_CH
AB_IN = GLU_W + Q_W + 2 * KV_W
SSD_INNER = 2048
SSD_HEAD_DIM = 64
SSD_HEADS = 32
SSD_GROUPS = 4
SSD_HPG = SSD_HEADS // SSD_GROUPS
SSD_STATE = 128
SSD_CONV = 4
SSD_BC = SSD_GROUPS * SSD_STATE
SSD_CONV_CH = SSD_INNER + 2 * SSD_BC
SSD_ZX = SSD_INNER + SSD_CONV_CH
SSD_GW = SSD_INNER // SSD_GROUPS
SSD_CHUNK = 128
D_FF = 4096
RMS_EPS = 1e-6
LN_EPS = 1e-5

LANES = 128
SUBLANES = 8
VMEM_LIMIT = 56 * 1024 * 1024

NEG_INF = float("-inf")


def _cparams(n_axes):
    return pltpu.CompilerParams(dimension_semantics=("arbitrary",) * n_axes,
                                vmem_limit_bytes=VMEM_LIMIT)


def _rms(x, g):
    return x * lax.rsqrt(jnp.mean(x * x, axis=-1, keepdims=True) + RMS_EPS) * g


def _sigmoid(x):
    return 1.0 / (1.0 + jnp.exp(-x))


def _silu(x):
    return x * _sigmoid(x)


def _softplus(x):
    return jnp.maximum(x, 0.0) + jnp.log1p(jnp.exp(-jnp.abs(x)))


def _dot(a, b):
    return jnp.dot(a, b, preferred_element_type=F32)


def _dot_nt(a, b):
    return lax.dot_general(a, b, (((1,), (1,)), ((), ())), preferred_element_type=F32)


def _dot_tn(a, b):
    return lax.dot_general(a, b, (((0,), (0,)), ((), ())), preferred_element_type=F32)


def _split3(x):
    hi = x.astype(BF16)
    r1 = x - hi.astype(F32)
    mid = r1.astype(BF16)
    lo = (r1 - mid.astype(F32)).astype(BF16)
    return hi, mid, lo


def _dot_sel_left(sel, x):
    hi, mid, lo = _split3(x)
    return _dot(sel, hi) + _dot(sel, mid) + _dot(sel, lo)


def _dot_sel_right(x, sel):
    hi, mid, lo = _split3(x)
    return _dot(hi, sel) + _dot(mid, sel) + _dot(lo, sel)


def _norm_matmul_kernel(x_ref, g_ref, w_ref, o_ref, hn_ref):
    @pl.when(pl.program_id(1) == 0)
    def _():
        hn_ref[...] = _rms(x_ref[...], g_ref[...]).astype(BF16)

    o_ref[...] = _dot(hn_ref[...], w_ref[...])


def _norm_matmul(x, g, w, *, tm, tn):
    m, d = x.shape
    n = w.shape[1]
    return pl.pallas_call(
        _norm_matmul_kernel,
        grid=(m // tm, n // tn),
        in_specs=[pl.BlockSpec((tm, d), lambda i, j: (i, 0)),
                  pl.BlockSpec((1, d), lambda i, j: (0, 0)),
                  pl.BlockSpec((d, tn), lambda i, j: (0, j))],
        out_specs=pl.BlockSpec((tm, tn), lambda i, j: (i, j)),
        out_shape=jax.ShapeDtypeStruct((m, n), F32),
        scratch_shapes=[pltpu.VMEM((tm, d), BF16)],
        compiler_params=_cparams(2),
        name="norm_matmul",
    )(x, g, w)


def _proj_norm_res_kernel(a_ref, w_ref, x_ref, g_ref, o_ref):
    y = _dot(a_ref[...].astype(BF16), w_ref[...])
    o_ref[...] = x_ref[...] + _rms(y, g_ref[...])


def _proj_norm_res(a, w, x, g, *, tm):
    m, k = a.shape
    d = w.shape[1]
    return pl.pallas_call(
        _proj_norm_res_kernel,
        grid=(m // tm,),
        in_specs=[pl.BlockSpec((tm, k), lambda i: (i, 0)),
                  pl.BlockSpec((k, d), lambda i: (0, 0)),
                  pl.BlockSpec((tm, d), lambda i: (i, 0)),
                  pl.BlockSpec((1, d), lambda i: (0, 0))],
        out_specs=pl.BlockSpec((tm, d), lambda i: (i, 0)),
        out_shape=jax.ShapeDtypeStruct((m, d), F32),
        compiler_params=_cparams(1),
        name="proj_norm_res",
    )(a, w, x, g)


def _mlp_kernel(x_ref, gpre_ref, wup_ref, wdn_ref, gpost_ref, o_ref, *, tf):
    x = x_ref[...]
    h = _rms(x, gpre_ref[...]).astype(BF16)
    f = None
    for c in range(D_FF // tf):
        a = jnp.maximum(_dot(h, wup_ref[:, c * tf:(c + 1) * tf]), 0.0)
        part = _dot((a * a).astype(BF16), wdn_ref[c * tf:(c + 1) * tf, :])
        f = part if f is None else f + part
    o_ref[...] = x + _rms(f, gpost_ref[...])


def _mlp(x, gpre, wup, wdn, gpost, *, tm, tf=1024):
    m, d = x.shape
    return pl.pallas_call(
        functools.partial(_mlp_kernel, tf=tf),
        grid=(m // tm,),
        in_specs=[pl.BlockSpec((tm, d), lambda i: (i, 0)),
                  pl.BlockSpec((1, d), lambda i: (0, 0)),
                  pl.BlockSpec((d, D_FF), lambda i: (0, 0)),
                  pl.BlockSpec((D_FF, d), lambda i: (0, 0)),
                  pl.BlockSpec((1, d), lambda i: (0, 0))],
        out_specs=pl.BlockSpec((tm, d), lambda i: (i, 0)),
        out_shape=jax.ShapeDtypeStruct((m, d), F32),
        compiler_params=_cparams(1),
        name="mlp",
    )(x, gpre, wup, wdn, gpost)


CONV_HALO = 32
CONV_ROWS = 64


def _glu(u):
    return u[:, :CONF_CH] * _sigmoid(u[:, CONF_CH:GLU_W])


def _ab_prompt_kernel(sink_ref, u_ref, uglu_prev_ref, ukv_prev_ref, x_ref, dw_ref, db_ref, lng_ref, lnb_ref,
                      wout_ref, g_ref, o_ref, tail_ref, ap_ref, mix_ref, *, tq):
    i = pl.program_id(1)
    has_prev = i > 0

    a_prev = _glu(uglu_prev_ref[...])
    ap_ref[0:CONV_HALO, :] = jnp.where(has_prev, a_prev, 0.0)
    ap_ref[CONV_HALO:CONV_HALO + tq, :] = _glu(u_ref[:, 0:GLU_W])
    tail_ref[...] = ap_ref[tq:tq + CONV_HALO, :]

    first_tap = CONV_HALO - (CONF_KERNEL - 1)
    for r in range(tq // CONV_ROWS):
        acc = jnp.broadcast_to(db_ref[...], (CONV_ROWS, CONF_CH))
        for k in range(CONF_KERNEL):
            start = r * CONV_ROWS + first_tap + k
            acc = acc + ap_ref[start:start + CONV_ROWS, :] * dw_ref[k:k + 1, :]
        mu = jnp.mean(acc, axis=-1, keepdims=True)
        xc = acc - mu
        var = jnp.mean(xc * xc, axis=-1, keepdims=True)
        ln = xc * lax.rsqrt(var + LN_EPS) * lng_ref[...] + lnb_ref[...]
        mix_ref[r * CONV_ROWS:(r + 1) * CONV_ROWS, 0:CONF_CH] = _silu(ln).astype(BF16)

    nrow = Q_PER_KV * ATT_BLOCK
    qi = lax.broadcasted_iota(jnp.int32, (nrow, 2 * ATT_BLOCK), 0) % ATT_BLOCK
    kc = lax.broadcasted_iota(jnp.int32, (nrow, 2 * ATT_BLOCK), 1)
    band = (kc > qi) & (kc <= qi + WINDOW)
    band_first = band & ((kc >= ATT_BLOCK) | has_prev)
    k_off = GLU_W + Q_W
    v_off = k_off + KV_W
    for j in range(tq // ATT_BLOCK):
        rows = slice(j * ATT_BLOCK, (j + 1) * ATT_BLOCK)
        q = (u_ref[rows, GLU_W:GLU_W + Q_W] * ATT_SCALE).astype(BF16)
        if j == 0:
            k_prev = ukv_prev_ref[:, 0:KV_W]
            v_prev = ukv_prev_ref[:, KV_W:2 * KV_W]
            mask = band_first
        else:
            prev = slice((j - 1) * ATT_BLOCK, j * ATT_BLOCK)
            k_prev = u_ref[prev, k_off:k_off + KV_W]
            v_prev = u_ref[prev, v_off:v_off + KV_W]
            mask = band
        kk = jnp.concatenate([k_prev, u_ref[rows, k_off:k_off + KV_W]], axis=0).astype(BF16)
        vv = jnp.concatenate([v_prev, u_ref[rows, v_off:v_off + KV_W]], axis=0).astype(BF16)
        for g in range(N_KV_HEADS):
            heads = [g * Q_PER_KV + hh for hh in range(Q_PER_KV)]
            qg = jnp.concatenate([q[:, h * HEAD_DIM:(h + 1) * HEAD_DIM] for h in heads], axis=0)
            s = _dot_nt(qg, kk[:, g * HEAD_DIM:(g + 1) * HEAD_DIM])
            s = jnp.where(mask, s, NEG_INF)
            sink = jnp.concatenate([jnp.full((ATT_BLOCK, 1), sink_ref[h], F32) for h in heads], axis=0)
            m = jnp.maximum(jnp.max(s, axis=-1, keepdims=True), sink)
            p = jnp.exp(s - m)
            den = jnp.sum(p, axis=-1, keepdims=True) + jnp.exp(sink - m)
            o = _dot(p.astype(BF16), vv[:, g * HEAD_DIM:(g + 1) * HEAD_DIM]) / den
            og = jnp.concatenate([o[hh * ATT_BLOCK:(hh + 1) * ATT_BLOCK, :] for hh in range(Q_PER_KV)], axis=1)
            c0 = CONF_CH + g * Q_PER_KV * HEAD_DIM
            mix_ref[rows, c0:c0 + Q_PER_KV * HEAD_DIM] = og.astype(BF16)

    y = _dot(mix_ref[...], wout_ref[...])
    o_ref[...] = x_ref[...] + _rms(y, g_ref[...])


def _ab_prompt(u, x, sinks, dw, db, lng, lnb, wout, g, *, bsz, t, tq):
    nt = t // tq
    m = bsz * t
    d = x.shape[1]

    def cur(b, i, s):
        return (b * nt + i, 0)

    def glu_prev(b, i, s):
        return (jnp.maximum((b * nt + i) * (tq // CONV_HALO) - 1, 0), 0)

    def kv_prev(b, i, s):
        return (jnp.maximum((b * nt + i) * (tq // ATT_BLOCK) - 1, 0), (GLU_W + Q_W) // (2 * KV_W))

    const = lambda b, i, s: (0, 0)
    grid_spec = pltpu.PrefetchScalarGridSpec(
        num_scalar_prefetch=1,
        grid=(bsz, nt),
        in_specs=[pl.BlockSpec((tq, AB_IN), cur),
                  pl.BlockSpec((CONV_HALO, GLU_W), glu_prev),
                  pl.BlockSpec((ATT_BLOCK, 2 * KV_W), kv_prev),
                  pl.BlockSpec((tq, d), cur),
                  pl.BlockSpec(dw.shape, const),
                  pl.BlockSpec((1, CONF_CH), const),
                  pl.BlockSpec((1, CONF_CH), const),
                  pl.BlockSpec((1, CONF_CH), const),
                  pl.BlockSpec(wout.shape, const),
                  pl.BlockSpec((1, d), const)],
        out_specs=[pl.BlockSpec((tq, d), cur),
                   pl.BlockSpec((CONV_HALO, CONF_CH), cur)],
        scratch_shapes=[pltpu.VMEM((CONV_HALO + tq, CONF_CH), F32),
                        pltpu.VMEM((tq, CONF_CH + Q_W), BF16)],
    )
    return pl.pallas_call(
        functools.partial(_ab_prompt_kernel, tq=tq),
        grid_spec=grid_spec,
        out_shape=[jax.ShapeDtypeStruct((m, d), F32),
                   jax.ShapeDtypeStruct((bsz * nt * CONV_HALO, CONF_CH), F32)],
        compiler_params=_cparams(2),
        name="ab_prompt",
    )(sinks, u, u, u, x, dw, db, lng, lnb, wout, g)


SSD_HALO = SUBLANES
SSD_CONV_COLS = 256


def _ssd_conv_silu(xp_ref, cw_ref, cb_ref, out_ref, rows):
    first_tap = SSD_HALO - (SSD_CONV - 1)
    for cc in range(SSD_CONV_CH // SSD_CONV_COLS):
        cols = slice(cc * SSD_CONV_COLS, (cc + 1) * SSD_CONV_COLS)
        acc = jnp.broadcast_to(cb_ref[:, cols], (rows, SSD_CONV_COLS))
        for k in range(SSD_CONV):
            acc = acc + xp_ref[first_tap + k:first_tap + k + rows, cols] * cw_ref[k:k + 1, cols]
        out_ref[:, cols] = _silu(acc)


def _ssd_prompt_kernel(uzx_ref, udt_ref, cw_ref, cb_ref, dtb_ref, alog_ref, dskip_ref, ng_ref, tri_ref,
                       y_ref, state_ref, xp_ref, xbc_ref, h_ref, xdec_ref, *, n_chunks):
    c = pl.program_id(1)
    L = SSD_CHUNK

    @pl.when(c == 0)
    def _():
        xp_ref[0:SSD_HALO, :] = jnp.zeros((SSD_HALO, SSD_CONV_CH), F32)
        h_ref[...] = jnp.zeros_like(h_ref)

    xp_ref[SSD_HALO:SSD_HALO + L, :] = uzx_ref[:, SSD_INNER:SSD_ZX]
    _ssd_conv_silu(xp_ref, cw_ref, cb_ref, xbc_ref, L)
    xp_ref[0:SSD_HALO, :] = xp_ref[L:L + SSD_HALO, :]

    dt = _softplus(udt_ref[...] + dtb_ref[...])
    a = -jnp.exp(alog_ref[...])
    cs = _dot_sel_left(tri_ref[...], dt * a)
    cs_t = cs.T

    row = lax.broadcasted_iota(jnp.int32, (L, LANES), 0)
    lane = lax.broadcasted_iota(jnp.int32, (L, LANES), 1)
    causal = row >= lane
    lo_half = lane < SSD_HEAD_DIM

    def bc(v, h):
        return jnp.broadcast_to(v[:, h:h + 1], (L, LANES))

    for g in range(SSD_GROUPS):
        b_g = xbc_ref[:, SSD_INNER + g * SSD_STATE:SSD_INNER + (g + 1) * SSD_STATE].astype(BF16)
        c_g = xbc_ref[:, SSD_INNER + SSD_BC + g * SSD_STATE:SSD_INNER + SSD_BC + (g + 1) * SSD_STATE].astype(BF16)
        hrows = slice(g * SSD_GW, (g + 1) * SSD_GW)
        cb = _dot_nt(c_g, b_g)
        yoff_raw = _dot_nt(c_g, h_ref[hrows, :].astype(BF16))
        end_decay = []
        gated = []
        for pr in range(SSD_HPG // 2):
            h1 = g * SSD_HPG + 2 * pr
            h2 = h1 + 1
            cols = slice(h1 * SSD_HEAD_DIM, (h1 + 2) * SSD_HEAD_DIM)
            a1 = bc(cs, h1)
            a2 = bc(cs, h2)
            m1 = cb * jnp.exp(jnp.where(causal, a1 - cs_t[h1:h1 + 1, :], NEG_INF))
            m2 = cb * jnp.exp(jnp.where(causal, a2 - cs_t[h2:h2 + 1, :], NEG_INF))
            xs = xbc_ref[:, cols]
            xc = xs * jnp.where(lo_half, bc(dt, h1), bc(dt, h2))
            lhs = jnp.concatenate([m1, m2], axis=1).astype(BF16)
            rhs = jnp.concatenate([jnp.where(lo_half, xc, 0.0), jnp.where(lo_half, 0.0, xc)], axis=0).astype(BF16)
            y_diag = _dot(lhs, rhs)
            csp = jnp.where(lo_half, a1, a2)
            y_off = jnp.exp(csp) * yoff_raw[:, 2 * pr * SSD_HEAD_DIM:(2 * pr + 2) * SSD_HEAD_DIM]
            xdec_ref[:, cols] = (xc * jnp.exp(csp[L - 1:L, :] - csp)).astype(BF16)
            end_decay += [jnp.exp(a1[L - 1:L, :]), jnp.exp(a2[L - 1:L, :])]
            y = y_diag + y_off + dskip_ref[:, cols] * xs
            gated.append(y * _silu(uzx_ref[:, cols]))
        ss = sum(jnp.sum(v * v, axis=-1, keepdims=True) for v in gated)
        scale = lax.rsqrt(ss * (1.0 / SSD_GW) + RMS_EPS)
        for pr in range(SSD_HPG // 2):
            cols = slice(g * SSD_GW + pr * LANES, g * SSD_GW + (pr + 1) * LANES)
            y_ref[:, cols] = gated[pr] * scale * ng_ref[:, cols]
        st = _dot_tn(xdec_ref[:, hrows], b_g)
        for k in range(SSD_HPG):
            hs = slice(g * SSD_GW + k * SSD_HEAD_DIM, g * SSD_GW + (k + 1) * SSD_HEAD_DIM)
            h_ref[hs, :] = h_ref[hs, :] * end_decay[k] + st[k * SSD_HEAD_DIM:(k + 1) * SSD_HEAD_DIM, :]

    @pl.when(c == n_chunks - 1)
    def _():
        state_ref[0] = h_ref[...]


def _ssd_prompt(uzx, udt, cw, cb, dtb, alog, dskip, ng, tri, *, bsz, t):
    nc = t // SSD_CHUNK
    L = SSD_CHUNK
    cur = lambda b, c: (b * nc + c, 0)
    const = lambda b, c: (0, 0)
    return pl.pallas_call(
        functools.partial(_ssd_prompt_kernel, n_chunks=nc),
        grid=(bsz, nc),
        in_specs=[pl.BlockSpec((L, SSD_ZX), cur),
                  pl.BlockSpec((L, LANES), cur),
                  pl.BlockSpec((SSD_CONV, SSD_CONV_CH), const),
                  pl.BlockSpec((1, SSD_CONV_CH), const),
                  pl.BlockSpec((1, LANES), const),
                  pl.BlockSpec((1, LANES), const),
                  pl.BlockSpec((1, SSD_INNER), const),
                  pl.BlockSpec((1, SSD_INNER), const),
                  pl.BlockSpec((L, L), const)],
        out_specs=[pl.BlockSpec((L, SSD_INNER), cur),
                   pl.BlockSpec((1, SSD_INNER, SSD_STATE), lambda b, c: (b, 0, 0))],
        out_shape=[jax.ShapeDtypeStruct((bsz * t, SSD_INNER), F32),
                   jax.ShapeDtypeStruct((bsz, SSD_INNER, SSD_STATE), F32)],
        scratch_shapes=[pltpu.VMEM((SSD_HALO + L, SSD_CONV_CH), F32),
                        pltpu.VMEM((L, SSD_CONV_CH), F32),
                        pltpu.VMEM((SSD_INNER, SSD_STATE), F32),
                        pltpu.VMEM((L, SSD_INNER), BF16)],
        compiler_params=_cparams(2),
        name="ssd_prompt",
    )(uzx, udt, cw, cb, dtb, alog, dskip, ng, tri)


def _conf_sample_kernel(u_ref, hist_ref, dw_ref, db_ref, lng_ref, lnb_ref, c_ref, a_ref):
    a = _glu(u_ref[:, 0:GLU_W])
    a_ref[...] = a
    nh = CONF_KERNEL - 1
    acc = db_ref[...] + jnp.sum(hist_ref[...] * dw_ref[0:nh, :], axis=1) + a * dw_ref[nh:nh + 1, :]
    mu = jnp.mean(acc, axis=-1, keepdims=True)
    xc = acc - mu
    var = jnp.mean(xc * xc, axis=-1, keepdims=True)
    c_ref[...] = _silu(xc * lax.rsqrt(var + LN_EPS) * lng_ref[...] + lnb_ref[...])


def _conf_sample(u, hist, dw, db, lng, lnb, *, bb):
    n = u.shape[0]
    nh = CONF_KERNEL - 1
    const = lambda i: (0, 0)
    return pl.pallas_call(
        _conf_sample_kernel,
        grid=(n // bb,),
        in_specs=[pl.BlockSpec((bb, AB_IN), lambda i: (i, 0)),
                  pl.BlockSpec((bb, nh, CONF_CH), lambda i: (i, 0, 0)),
                  pl.BlockSpec(dw.shape, const),
                  pl.BlockSpec((1, CONF_CH), const),
                  pl.BlockSpec((1, CONF_CH), const),
                  pl.BlockSpec((1, CONF_CH), const)],
        out_specs=[pl.BlockSpec((bb, CONF_CH), lambda i: (i, 0)),
                   pl.BlockSpec((bb, CONF_CH), lambda i: (i, 0))],
        out_shape=[jax.ShapeDtypeStruct((n, CONF_CH), F32),
                   jax.ShapeDtypeStruct((n, CONF_CH), F32)],
        compiler_params=_cparams(1),
        name="conf_sample",
    )(u, hist, dw, db, lng, lnb)


def _attn_sample_kernel(qm_ref, kc_ref, vc_ref, kn_ref, vn_ref, sink_ref, o_ref, *, wc):
    q = qm_ref[...] * ATT_SCALE
    s = jnp.einsum("bhc,bsc->bhs", q.astype(BF16), kc_ref[...].astype(BF16), preferred_element_type=F32)
    kpos = lax.broadcasted_iota(jnp.int32, s.shape, 2)
    s = jnp.where((kpos <= wc) & (wc - kpos < WINDOW), s, NEG_INF)
    s_new = jnp.sum(q * kn_ref[...], axis=-1, keepdims=True)
    sink = sink_ref[...][None]
    m = jnp.maximum(jnp.maximum(jnp.max(s, axis=-1, keepdims=True), s_new), sink)
    p = jnp.exp(s - m)
    p_new = jnp.exp(s_new - m)
    den = jnp.sum(p, axis=-1, keepdims=True) + p_new + jnp.exp(sink - m)
    o = jnp.einsum("bhs,bsc->bhc", p.astype(BF16), vc_ref[...].astype(BF16), preferred_element_type=F32)
    o_ref[...] = (o + p_new * vn_ref[...]) / den


def _attn_sample(qm, kc, vc, kn, vn, sink, *, bb):
    n, wc, _ = kc.shape
    blk3 = lambda i: (i, 0, 0)
    return pl.pallas_call(
        functools.partial(_attn_sample_kernel, wc=wc),
        grid=(n // bb,),
        in_specs=[pl.BlockSpec((bb, N_Q_HEADS, KV_W), blk3),
                  pl.BlockSpec((bb, wc, KV_W), blk3),
                  pl.BlockSpec((bb, wc, KV_W), blk3),
                  pl.BlockSpec((bb, 1, KV_W), blk3),
                  pl.BlockSpec((bb, 1, KV_W), blk3),
                  pl.BlockSpec((N_Q_HEADS, 1), lambda i: (0, 0))],
        out_specs=pl.BlockSpec((bb, N_Q_HEADS, KV_W), blk3),
        out_shape=jax.ShapeDtypeStruct((n, N_Q_HEADS, KV_W), F32),
        compiler_params=_cparams(1),
        name="attn_sample",
    )(qm, kc, vc, kn, vn, sink)


def _ssd_sample_pre_kernel(uzx_ref, udt_ref, hist_ref, cw_ref, cb_ref, dtb_ref, alog_ref, expand_ref,
                           xbc_ref, xdt_ref, da_ref):
    nh = SSD_CONV - 1
    xbc = uzx_ref[:, SSD_INNER:SSD_ZX]
    acc = cb_ref[...] + jnp.sum(hist_ref[...] * cw_ref[0:nh, :], axis=1) + xbc * cw_ref[nh:nh + 1, :]
    conv = _silu(acc)
    xbc_ref[...] = conv
    dt = _softplus(udt_ref[...] + dtb_ref[...])
    da_ref[...] = jnp.exp(dt * (-jnp.exp(alog_ref[...])))
    xdt_ref[...] = conv[:, 0:SSD_INNER] * _dot_sel_right(dt, expand_ref[...])


def _ssd_sample_pre(uzx, udt, hist, cw, cb, dtb, alog, expand, *, bb):
    n = uzx.shape[0]
    const = lambda i: (0, 0)
    row = lambda i: (i, 0)
    return pl.pallas_call(
        _ssd_sample_pre_kernel,
        grid=(n // bb,),
        in_specs=[pl.BlockSpec((bb, SSD_ZX), row),
                  pl.BlockSpec((bb, LANES), row),
                  pl.BlockSpec((bb, SSD_CONV - 1, SSD_CONV_CH), lambda i: (i, 0, 0)),
                  pl.BlockSpec((SSD_CONV, SSD_CONV_CH), const),
                  pl.BlockSpec((1, SSD_CONV_CH), const),
                  pl.BlockSpec((1, LANES), const),
                  pl.BlockSpec((1, LANES), const),
                  pl.BlockSpec((LANES, SSD_INNER), const)],
        out_specs=[pl.BlockSpec((bb, SSD_CONV_CH), row),
                   pl.BlockSpec((bb, SSD_INNER), row),
                   pl.BlockSpec((bb, LANES), row)],
        out_shape=[jax.ShapeDtypeStruct((n, SSD_CONV_CH), F32),
                   jax.ShapeDtypeStruct((n, SSD_INNER), F32),
                   jax.ShapeDtypeStruct((n, LANES), F32)],
        compiler_params=_cparams(1),
        name="ssd_sample_pre",
    )(uzx, udt, hist, cw, cb, dtb, alog, expand)


def _ssd_sample_state_kernel(da_ref, st_ref, xdt_ref, xbc_ref, so_ref, yt_ref, xt_ref, xr_ref, *, bb):
    i = pl.program_id(0)

    @pl.when(i == 0)
    def _():
        xt_ref[...] = xdt_ref[...].T
        yt_ref[...] = jnp.zeros_like(yt_ref)

    xr_ref[...] = pltpu.roll(xt_ref[...], (LANES - i * bb) % LANES, 1)
    lane = lax.broadcasted_iota(jnp.int32, (SSD_HEAD_DIM, LANES), 1)
    for j in range(bb):
        b = i * bb + j
        for g in range(SSD_GROUPS):
            b_row = xbc_ref[j:j + 1, SSD_INNER + g * SSD_STATE:SSD_INNER + (g + 1) * SSD_STATE]
            c_row = xbc_ref[j:j + 1, SSD_INNER + SSD_BC + g * SSD_STATE:SSD_INNER + SSD_BC + (g + 1) * SSD_STATE]

            def body(k, carry):
                h = g * SSD_HPG + k
                r0 = pl.multiple_of(h * SSD_HEAD_DIM, SSD_HEAD_DIM)
                hs = pl.ds(r0, SSD_HEAD_DIM)
                hn = st_ref[j, hs, :] * da_ref[b, h] + xr_ref[hs, j:j + 1] * b_row
                so_ref[j, hs, :] = hn
                ycol = jnp.sum(hn * c_row, axis=-1, keepdims=True)
                yt_ref[hs, :] = jnp.where(lane == b, ycol, yt_ref[hs, :])
                return carry

            lax.fori_loop(0, SSD_HPG, body, 0)


def _ssd_sample_state(da, state, xdt, xbc, *, bb):
    n = state.shape[0]
    grid_spec = pltpu.PrefetchScalarGridSpec(
        num_scalar_prefetch=1,
        grid=(n // bb,),
        in_specs=[pl.BlockSpec((bb, SSD_INNER, SSD_STATE), lambda i, s: (i, 0, 0)),
                  pl.BlockSpec((n, SSD_INNER), lambda i, s: (0, 0)),
                  pl.BlockSpec((bb, SSD_CONV_CH), lambda i, s: (i, 0))],
        out_specs=[pl.BlockSpec((bb, SSD_INNER, SSD_STATE), lambda i, s: (i, 0, 0)),
                   pl.BlockSpec((SSD_INNER, n), lambda i, s: (0, 0))],
        scratch_shapes=[pltpu.VMEM((SSD_INNER, n), F32),
                        pltpu.VMEM((SSD_INNER, n), F32)],
    )
    return pl.pallas_call(
        functools.partial(_ssd_sample_state_kernel, bb=bb),
        grid_spec=grid_spec,
        out_shape=[jax.ShapeDtypeStruct(state.shape, F32),
                   jax.ShapeDtypeStruct((SSD_INNER, n), F32)],
        compiler_params=_cparams(1),
        name="ssd_sample_state",
    )(da, state, xdt, xbc)


def _ssd_sample_post_kernel(yt_ref, xbc_ref, uzx_ref, dskip_ref, ng_ref, o_ref):
    y = yt_ref[...].T + dskip_ref[...] * xbc_ref[:, 0:SSD_INNER]
    y = y * _silu(uzx_ref[:, 0:SSD_INNER])
    for g in range(SSD_GROUPS):
        cols = slice(g * SSD_GW, (g + 1) * SSD_GW)
        o_ref[:, cols] = _rms(y[:, cols], ng_ref[:, cols])


def _ssd_sample_post(yt, xbc, uzx, dskip, ng):
    n = xbc.shape[0]
    const = lambda i: (0, 0)
    return pl.pallas_call(
        _ssd_sample_post_kernel,
        grid=(1,),
        in_specs=[pl.BlockSpec((SSD_INNER, n), const),
                  pl.BlockSpec((n, SSD_CONV_CH), const),
                  pl.BlockSpec((n, SSD_ZX), const),
                  pl.BlockSpec((1, SSD_INNER), const),
                  pl.BlockSpec((1, SSD_INNER), const)],
        out_specs=pl.BlockSpec((n, SSD_INNER), const),
        out_shape=jax.ShapeDtypeStruct((n, SSD_INNER), F32),
        compiler_params=_cparams(1),
        name="ssd_sample_post",
    )(yt, xbc, uzx, dskip, ng)


def _pad_lanes(v):
    return jnp.pad(v.astype(F32), (0, LANES - v.shape[0]))[None, :]


def _head_expand(v):
    return jnp.repeat(v.astype(F32), SSD_HEAD_DIM)[None, :]


def _prep_layer(P, layer):
    i = layer // 2
    g = P["norm_g"][layer]
    w = {"g": [g[k][None, :] for k in range(4)],
         "w_up": P["mlp_w_up"][layer].astype(BF16),
         "w_down": P["mlp_w_down"][layer].astype(BF16)}
    if layer % 2 == 0:
        w.update(w_in=P["ab_w_in"][i].astype(BF16), w_out=P["ab_w_out"][i].astype(BF16),
                 dw=P["conf_dw_w"][i], db=P["conf_dw_b"][i][None, :],
                 lng=P["conf_ln_g"][i][None, :], lnb=P["conf_ln_b"][i][None, :],
                 sinks=P["attn_sinks"][i])
    else:
        w_in = P["ssd_w_in"][i]
        w_dt = jnp.pad(w_in[:, SSD_ZX:], ((0, 0), (0, LANES - SSD_HEADS)))
        w.update(w_zx=w_in[:, :SSD_ZX].astype(BF16), w_dt=w_dt.astype(BF16),
                 w_out=P["ssd_w_out"][i].astype(BF16),
                 cw=P["ssd_conv_w"][i], cb=P["ssd_conv_b"][i][None, :],
                 dtb=_pad_lanes(P["ssd_dt_bias"][i]), alog=_pad_lanes(P["ssd_a_log"][i]),
                 dskip=_head_expand(P["ssd_d"][i]), ng=P["ssd_norm_g"][i][None, :])
    return w


def _trunk_prompt(x3, W):
    bsz, t, d = x3.shape
    x = x3.reshape(bsz * t, d)
    tri = jnp.tril(jnp.ones((SSD_CHUNK, SSD_CHUNK), F32)).astype(BF16)
    win_k, win_v, conf_rows, ssm_states, ssd_rows = [], [], [], [], []
    tq = 512
    for layer in range(4):
        w = W[layer]
        g = w["g"]
        if layer % 2 == 0:
            u = _norm_matmul(x, g[0], w["w_in"], tm=1024, tn=AB_IN // 2)
            x, tails = _ab_prompt(u, x, w["sinks"], w["dw"], w["db"], w["lng"], w["lnb"], w["w_out"], g[1],
                                  bsz=bsz, t=t, tq=tq)
            u3 = u.reshape(bsz, t, AB_IN)
            wp = min(WINDOW, t)
            k_off = GLU_W + Q_W
            win_k.append(u3[:, t - wp:, k_off:k_off + KV_W].reshape(bsz, wp, N_KV_HEADS, HEAD_DIM))
            win_v.append(u3[:, t - wp:, k_off + KV_W:].reshape(bsz, wp, N_KV_HEADS, HEAD_DIM))
            tails = tails.reshape(bsz, t // tq, CONV_HALO, CONF_CH)
            conf_rows.append(tails[:, -1, CONV_HALO - (CONF_KERNEL - 1):, :])
        else:
            uzx = _norm_matmul(x, g[0], w["w_zx"], tm=1024, tn=1024)
            udt = _norm_matmul(x, g[0], w["w_dt"], tm=1024, tn=LANES)
            y, state = _ssd_prompt(uzx, udt, w["cw"], w["cb"], w["dtb"], w["alog"], w["dskip"], w["ng"], tri,
                                   bsz=bsz, t=t)
            x = _proj_norm_res(y, w["w_out"], x, g[1], tm=512)
            ssm_states.append(state.reshape(bsz, SSD_HEADS, SSD_HEAD_DIM, SSD_STATE))
            ssd_rows.append(uzx.reshape(bsz, t, SSD_ZX)[:, t - (SSD_CONV - 1):, SSD_INNER:])
        x = _mlp(x, g[2], w["w_up"], w["w_down"], g[3], tm=512)
    return (x.reshape(bsz, t, d), jnp.stack(win_k), jnp.stack(win_v), jnp.stack(conf_rows),
            jnp.stack(ssm_states), jnp.stack(ssd_rows))


def _trunk_sample(x3, W, cache_k, cache_v, conf_hist, ssm, ssd_hist):
    n, t, d = x3.shape
    x = x3.reshape(n, d)
    expand = jnp.repeat(jnp.eye(LANES, SSD_HEADS, dtype=F32), SSD_HEAD_DIM, axis=1).astype(BF16)
    win_k, win_v, conf_rows, ssm_states, ssd_rows = [], [], [], [], []
    head_group = jnp.arange(N_Q_HEADS) // Q_PER_KV
    for layer in range(4):
        w = W[layer]
        g = w["g"]
        i = layer // 2
        if layer % 2 == 0:
            u = _norm_matmul(x, g[0], w["w_in"], tm=n, tn=AB_IN // 2)
            c_out, a = _conf_sample(u, conf_hist[i], w["dw"], w["db"], w["lng"], w["lnb"], bb=32)
            k_off = GLU_W + Q_W
            q = u[:, GLU_W:k_off].reshape(n, N_Q_HEADS, HEAD_DIM)
            k_new = u[:, k_off:k_off + KV_W]
            v_new = u[:, k_off + KV_W:]
            lane_group = jnp.arange(KV_W) // HEAD_DIM
            qm = jnp.where(head_group[None, :, None] == lane_group[None, None, :],
                           jnp.tile(q, (1, 1, N_KV_HEADS)), 0.0)
            wc = cache_k.shape[2]
            o = _attn_sample(qm, cache_k[i].reshape(n, wc, KV_W), cache_v[i].reshape(n, wc, KV_W),
                             k_new[:, None, :], v_new[:, None, :], w["sinks"][:, None], bb=32)
            o = o.reshape(n, N_Q_HEADS, N_KV_HEADS, HEAD_DIM)
            a_out = jnp.take_along_axis(o, head_group[None, :, None, None], axis=2).reshape(n, Q_W)
            x = _proj_norm_res(jnp.concatenate([c_out, a_out], axis=-1), w["w_out"], x, g[1], tm=n)
            win_k.append(k_new.reshape(n, 1, N_KV_HEADS, HEAD_DIM))
            win_v.append(v_new.reshape(n, 1, N_KV_HEADS, HEAD_DIM))
            conf_rows.append(a[:, None, :])
        else:
            uzx = _norm_matmul(x, g[0], w["w_zx"], tm=n, tn=1024)
            udt = _norm_matmul(x, g[0], w["w_dt"], tm=n, tn=LANES)
            xbc, xdt, da = _ssd_sample_pre(uzx, udt, ssd_hist[i], w["cw"], w["cb"], w["dtb"], w["alog"], expand,
                                           bb=32)
            state, yt = _ssd_sample_state(da, ssm[i].reshape(n, SSD_INNER, SSD_STATE), xdt, xbc, bb=8)
            y = _ssd_sample_post(yt, xbc, uzx, w["dskip"], w["ng"])
            x = _proj_norm_res(y, w["w_out"], x, g[1], tm=n)
            ssm_states.append(state.reshape(n, SSD_HEADS, SSD_HEAD_DIM, SSD_STATE))
            ssd_rows.append(uzx[:, None, SSD_INNER:])
        x = _mlp(x, g[2], w["w_up"], w["w_down"], g[3], tm=n)
    return (x.reshape(n, t, d), jnp.stack(win_k), jnp.stack(win_v), jnp.stack(conf_rows),
            jnp.stack(ssm_states), jnp.stack(ssd_rows))


def kernel(x_prompt, x_sample, cache_win_k, cache_win_v, state_conf_conv, state_ssm, state_ssd_conv,
           norm_g, ab_w_in, conf_dw_w, conf_dw_b, conf_ln_g, conf_ln_b, attn_sinks, ab_w_out,
           ssd_w_in, ssd_conv_w, ssd_conv_b, ssd_dt_bias, ssd_a_log, ssd_d, ssd_norm_g, ssd_w_out,
           mlp_w_up, mlp_w_down):
    P = {"norm_g": norm_g, "ab_w_in": ab_w_in, "conf_dw_w": conf_dw_w, "conf_dw_b": conf_dw_b,
         "conf_ln_g": conf_ln_g, "conf_ln_b": conf_ln_b, "attn_sinks": attn_sinks, "ab_w_out": ab_w_out,
         "ssd_w_in": ssd_w_in, "ssd_conv_w": ssd_conv_w, "ssd_conv_b": ssd_conv_b, "ssd_dt_bias": ssd_dt_bias,
         "ssd_a_log": ssd_a_log, "ssd_d": ssd_d, "ssd_norm_g": ssd_norm_g, "ssd_w_out": ssd_w_out,
         "mlp_w_up": mlp_w_up, "mlp_w_down": mlp_w_down}
    W = [_prep_layer(P, layer) for layer in range(4)]
    y_p, wk_p, wv_p, cc_p, ssm_p, sc_p = _trunk_prompt(x_prompt, W)
    y_s, wk_s, wv_s, cc_s, ssm_s, sc_s = _trunk_sample(x_sample, W, cache_win_k, cache_win_v,
                                                       state_conf_conv, state_ssm, state_ssd_conv)
    return (y_p, y_s, wk_p, wv_p, wk_s, wv_s, cc_p, cc_s, ssm_p, ssm_s, sc_p, sc_s)
```

```python
import functools

import jax
import jax.numpy as jnp
from jax import lax
from jax.experimental import pallas as pl
from jax.experimental.pallas import tpu as pltpu

F32 = jnp.float32
BF16 = jnp.bfloat16

D_MODEL = 1024
CONF_CH = 512
CONF_KERNEL = 31
HEAD_DIM = 64
N_Q_HEADS = 8
N_KV_HEADS = 2
Q_PER_KV = N_Q_HEADS // N_KV_HEADS
WINDOW = 128
ATT_BLOCK = 128
ATT_SCALE = HEAD_DIM ** -0.5
Q_W = N_Q_HEADS * HEAD_DIM
KV_W = N_KV_HEADS * HEAD_DIM
GLU_W = 2 * CONF_CH
AB_IN = GLU_W + Q_W + 2 * KV_W
SSD_INNER = 2048
SSD_HEAD_DIM = 64
SSD_HEADS = 32
SSD_GROUPS = 4
SSD_HPG = SSD_HEADS // SSD_GROUPS
SSD_STATE = 128
SSD_CONV = 4
SSD_BC = SSD_GROUPS * SSD_STATE
SSD_CONV_CH = SSD_INNER + 2 * SSD_BC
SSD_ZX = SSD_INNER + SSD_CONV_CH
SSD_GW = SSD_INNER // SSD_GROUPS
SSD_CHUNK = 128
D_FF = 4096
RMS_EPS = 1e-6
LN_EPS = 1e-5

LANES = 128
SUBLANES = 8
VMEM_LIMIT = 56 * 1024 * 1024

NEG_INF = float("-inf")


def _cparams(n_axes):
    return pltpu.CompilerParams(dimension_semantics=("arbitrary",) * n_axes,
                                vmem_limit_bytes=VMEM_LIMIT)


def _rms(x, g):
    return x * lax.rsqrt(jnp.mean(x * x, axis=-1, keepdims=True) + RMS_EPS) * g


def _sigmoid(x):
    return 1.0 / (1.0 + jnp.exp(-x))


def _silu(x):
    return x * _sigmoid(x)


def _softplus(x):
    return jnp.maximum(x, 0.0) + jnp.log1p(jnp.exp(-jnp.abs(x)))


def _dot(a, b):
    return jnp.dot(a, b, preferred_element_type=F32)


def _dot_nt(a, b):
    return lax.dot_general(a, b, (((1,), (1,)), ((), ())), preferred_element_type=F32)


def _dot_tn(a, b):
    return lax.dot_general(a, b, (((0,), (0,)), ((), ())), preferred_element_type=F32)


def _split3(x):
    hi = x.astype(BF16)
    r1 = x - hi.astype(F32)
    mid = r1.astype(BF16)
    lo = (r1 - mid.astype(F32)).astype(BF16)
    return hi, mid, lo


def _dot_sel_left(sel, x):
    hi, mid, lo = _split3(x)
    return _dot(sel, hi) + _dot(sel, mid) + _dot(sel, lo)


def _dot_sel_right(x, sel):
    hi, mid, lo = _split3(x)
    return _dot(hi, sel) + _dot(mid, sel) + _dot(lo, sel)


def _norm_matmul_kernel(x_ref, g_ref, w_ref, o_ref, hn_ref):
    @pl.when(pl.program_id(1) == 0)
    def _():
        hn_ref[...] = _rms(x_ref[...], g_ref[...]).astype(BF16)

    o_ref[...] = _dot(hn_ref[...], w_ref[...])


def _norm_matmul(x, g, w, *, tm, tn):
    m, d = x.shape
    n = w.shape[1]
    return pl.pallas_call(
        _norm_matmul_kernel,
        grid=(m // tm, n // tn),
        in_specs=[pl.BlockSpec((tm, d), lambda i, j: (i, 0)),
                  pl.BlockSpec((1, d), lambda i, j: (0, 0)),
                  pl.BlockSpec((d, tn), lambda i, j: (0, j))],
        out_specs=pl.BlockSpec((tm, tn), lambda i, j: (i, j)),
        out_shape=jax.ShapeDtypeStruct((m, n), F32),
        scratch_shapes=[pltpu.VMEM((tm, d), BF16)],
        compiler_params=_cparams(2),
        name="norm_matmul",
    )(x, g, w)


def _proj_norm_res_kernel(a_ref, w_ref, x_ref, g_ref, o_ref):
    y = _dot(a_ref[...].astype(BF16), w_ref[...])
    o_ref[...] = x_ref[...] + _rms(y, g_ref[...])


def _proj_norm_res(a, w, x, g, *, tm):
    m, k = a.shape
    d = w.shape[1]
    return pl.pallas_call(
        _proj_norm_res_kernel,
        grid=(m // tm,),
        in_specs=[pl.BlockSpec((tm, k), lambda i: (i, 0)),
                  pl.BlockSpec((k, d), lambda i: (0, 0)),
                  pl.BlockSpec((tm, d), lambda i: (i, 0)),
                  pl.BlockSpec((1, d), lambda i: (0, 0))],
        out_specs=pl.BlockSpec((tm, d), lambda i: (i, 0)),
        out_shape=jax.ShapeDtypeStruct((m, d), F32),
        compiler_params=_cparams(1),
        name="proj_norm_res",
    )(a, w, x, g)


def _mlp_kernel(x_ref, gpre_ref, wup_ref, wdn_ref, gpost_ref, o_ref, *, tf):
    x = x_ref[...]
    h = _rms(x, gpre_ref[...]).astype(BF16)
    f = None
    for c in range(D_FF // tf):
        a = jnp.maximum(_dot(h, wup_ref[:, c * tf:(c + 1) * tf]), 0.0)
        part = _dot((a * a).astype(BF16), wdn_ref[c * tf:(c + 1) * tf, :])
        f = part if f is None else f + part
    o_ref[...] = x + _rms(f, gpost_ref[...])


def _mlp(x, gpre, wup, wdn, gpost, *, tm, tf=1024):
    m, d = x.shape
    return pl.pallas_call(
        functools.partial(_mlp_kernel, tf=tf),
        grid=(m // tm,),
        in_specs=[pl.BlockSpec((tm, d), lambda i: (i, 0)),
                  pl.BlockSpec((1, d), lambda i: (0, 0)),
                  pl.BlockSpec((d, D_FF), lambda i: (0, 0)),
                  pl.BlockSpec((D_FF, d), lambda i: (0, 0)),
                  pl.BlockSpec((1, d), lambda i: (0, 0))],
        out_specs=pl.BlockSpec((tm, d), lambda i: (i, 0)),
        out_shape=jax.ShapeDtypeStruct((m, d), F32),
        compiler_params=_cparams(1),
        name="mlp",
    )(x, gpre, wup, wdn, gpost)


CONV_HALO = 32
CONV_ROWS = 64


def _glu(u):
    return u[:, :CONF_CH] * _sigmoid(u[:, CONF_CH:GLU_W])


def _ab_prompt_kernel(sink_ref, u_ref, uglu_prev_ref, ukv_prev_ref, x_ref, dw_ref, db_ref, lng_ref, lnb_ref,
                      wout_ref, g_ref, o_ref, tail_ref, ap_ref, sh_ref, mix_ref, *, tq):
    i = pl.program_id(1)
    has_prev = i > 0

    n_ap = CONV_HALO + tq
    a_prev = _glu(uglu_prev_ref[...])
    ap_ref[0:CONV_HALO, :] = jnp.where(has_prev, a_prev, 0.0)
    ap_ref[CONV_HALO:n_ap, :] = _glu(u_ref[:, 0:GLU_W])
    ap_ref[n_ap:n_ap + SUBLANES, :] = jnp.zeros((SUBLANES, CONF_CH), F32)
    tail_ref[...] = ap_ref[tq:tq + CONV_HALO, :]
    for r in range(1, SUBLANES):
        sh_ref[r - 1] = ap_ref[r:r + n_ap, :]

    first_tap = CONV_HALO - (CONF_KERNEL - 1)
    for rc in range(tq // CONV_ROWS):
        acc = jnp.broadcast_to(db_ref[...], (CONV_ROWS, CONF_CH))
        for k in range(CONF_KERNEL):
            tile, r = divmod(first_tap + k, SUBLANES)
            start = rc * CONV_ROWS + tile * SUBLANES
            if r == 0:
                rows = ap_ref[start:start + CONV_ROWS, :]
            else:
                rows = sh_ref[r - 1, start:start + CONV_ROWS, :]
            acc = acc + rows * dw_ref[k:k + 1, :]
        mu = jnp.mean(acc, axis=-1, keepdims=True)
        xc = acc - mu
        var = jnp.mean(xc * xc, axis=-1, keepdims=True)
        ln = xc * lax.rsqrt(var + LN_EPS) * lng_ref[...] + lnb_ref[...]
        mix_ref[rc * CONV_ROWS:(rc + 1) * CONV_ROWS, 0:CONF_CH] = _silu(ln).astype(BF16)

    nrow = Q_PER_KV * ATT_BLOCK
    qi = lax.broadcasted_iota(jnp.int32, (nrow, 2 * ATT_BLOCK), 0) % ATT_BLOCK
    kc = lax.broadcasted_iota(jnp.int32, (nrow, 2 * ATT_BLOCK), 1)
    band = (kc > qi) & (kc <= qi + WINDOW)
    band_first = band & ((kc >= ATT_BLOCK) | has_prev)
    k_off = GLU_W + Q_W
    v_off = k_off + KV_W
    for j in range(tq // ATT_BLOCK):
        rows = slice(j * ATT_BLOCK, (j + 1) * ATT_BLOCK)
        q = (u_ref[rows, GLU_W:GLU_W + Q_W] * ATT_SCALE).astype(BF16)
        if j == 0:
            k_prev = ukv_prev_ref[:, 0:KV_W]
            v_prev = ukv_prev_ref[:, KV_W:2 * KV_W]
            mask = band_first
        else:
            prev = slice((j - 1) * ATT_BLOCK, j * ATT_BLOCK)
            k_prev = u_ref[prev, k_off:k_off + KV_W]
            v_prev = u_ref[prev, v_off:v_off + KV_W]
            mask = band
        kk = jnp.concatenate([k_prev, u_ref[rows, k_off:k_off + KV_W]], axis=0).astype(BF16)
        vv = jnp.concatenate([v_prev, u_ref[rows, v_off:v_off + KV_W]], axis=0).astype(BF16)
        for g in range(N_KV_HEADS):
            heads = [g * Q_PER_KV + hh for hh in range(Q_PER_KV)]
            qg = jnp.concatenate([q[:, h * HEAD_DIM:(h + 1) * HEAD_DIM] for h in heads], axis=0)
            s = _dot_nt(qg, kk[:, g * HEAD_DIM:(g + 1) * HEAD_DIM])
            s = jnp.where(mask, s, NEG_INF)
            sink = jnp.concatenate([jnp.full((ATT_BLOCK, 1), sink_ref[h], F32) for h in heads], axis=0)
            m = jnp.maximum(jnp.max(s, axis=-1, keepdims=True), sink)
            p = jnp.exp(s - m)
            den = jnp.sum(p, axis=-1, keepdims=True) + jnp.exp(sink - m)
            o = _dot(p.astype(BF16), vv[:, g * HEAD_DIM:(g + 1) * HEAD_DIM]) / den
            og = jnp.concatenate([o[hh * ATT_BLOCK:(hh + 1) * ATT_BLOCK, :] for hh in range(Q_PER_KV)], axis=1)
            c0 = CONF_CH + g * Q_PER_KV * HEAD_DIM
            mix_ref[rows, c0:c0 + Q_PER_KV * HEAD_DIM] = og.astype(BF16)

    y = _dot(mix_ref[...], wout_ref[...])
    o_ref[...] = x_ref[...] + _rms(y, g_ref[...])


def _ab_prompt(u, x, sinks, dw, db, lng, lnb, wout, g, *, bsz, t, tq):
    nt = t // tq
    m = bsz * t
    d = x.shape[1]

    def cur(b, i, s):
        return (b * nt + i, 0)

    def glu_prev(b, i, s):
        return (jnp.maximum((b * nt + i) * (tq // CONV_HALO) - 1, 0), 0)

    def kv_prev(b, i, s):
        return (jnp.maximum((b * nt + i) * (tq // ATT_BLOCK) - 1, 0), (GLU_W + Q_W) // (2 * KV_W))

    const = lambda b, i, s: (0, 0)
    grid_spec = pltpu.PrefetchScalarGridSpec(
        num_scalar_prefetch=1,
        grid=(bsz, nt),
        in_specs=[pl.BlockSpec((tq, AB_IN), cur),
                  pl.BlockSpec((CONV_HALO, GLU_W), glu_prev),
                  pl.BlockSpec((ATT_BLOCK, 2 * KV_W), kv_prev),
                  pl.BlockSpec((tq, d), cur),
                  pl.BlockSpec(dw.shape, const),
                  pl.BlockSpec((1, CONF_CH), const),
                  pl.BlockSpec((1, CONF_CH), const),
                  pl.BlockSpec((1, CONF_CH), const),
                  pl.BlockSpec(wout.shape, const),
                  pl.BlockSpec((1, d), const)],
        out_specs=[pl.BlockSpec((tq, d), cur),
                   pl.BlockSpec((CONV_HALO, CONF_CH), cur)],
        scratch_shapes=[pltpu.VMEM((CONV_HALO + tq + SUBLANES, CONF_CH), F32),
                        pltpu.VMEM((SUBLANES - 1, CONV_HALO + tq, CONF_CH), F32),
                        pltpu.VMEM((tq, CONF_CH + Q_W), BF16)],
    )
    return pl.pallas_call(
        functools.partial(_ab_prompt_kernel, tq=tq),
        grid_spec=grid_spec,
        out_shape=[jax.ShapeDtypeStruct((m, d), F32),
                   jax.ShapeDtypeStruct((bsz * nt * CONV_HALO, CONF_CH), F32)],
        compiler_params=_cparams(2),
        name="ab_prompt",
    )(sinks, u, u, u, x, dw, db, lng, lnb, wout, g)


SSD_HALO = SUBLANES
SSD_CONV_COLS = 256


def _ssd_conv_silu(xp_ref, cw_ref, cb_ref, out_ref, rows):
    first_tap = SSD_HALO - (SSD_CONV - 1)
    for cc in range(SSD_CONV_CH // SSD_CONV_COLS):
        cols = slice(cc * SSD_CONV_COLS, (cc + 1) * SSD_CONV_COLS)
        acc = jnp.broadcast_to(cb_ref[:, cols], (rows, SSD_CONV_COLS))
        for k in range(SSD_CONV):
            acc = acc + xp_ref[first_tap + k:first_tap + k + rows, cols] * cw_ref[k:k + 1, cols]
        out_ref[:, cols] = _silu(acc)


def _ssd_prompt_kernel(uzx_ref, udt_ref, cw_ref, cb_ref, dtb_ref, alog_ref, dskip_ref, ng_ref, tri_ref,
                       y_ref, state_ref, xp_ref, xbc_ref, h_ref, xdec_ref, *, n_chunks):
    c = pl.program_id(1)
    L = SSD_CHUNK

    @pl.when(c == 0)
    def _():
        xp_ref[0:SSD_HALO, :] = jnp.zeros((SSD_HALO, SSD_CONV_CH), F32)
        h_ref[...] = jnp.zeros_like(h_ref)

    xp_ref[SSD_HALO:SSD_HALO + L, :] = uzx_ref[:, SSD_INNER:SSD_ZX]
    _ssd_conv_silu(xp_ref, cw_ref, cb_ref, xbc_ref, L)
    xp_ref[0:SSD_HALO, :] = xp_ref[L:L + SSD_HALO, :]

    dt = _softplus(udt_ref[...] + dtb_ref[...])
    a = -jnp.exp(alog_ref[...])
    cs = _dot_sel_left(tri_ref[...], dt * a)
    cs_t = cs.T

    row = lax.broadcasted_iota(jnp.int32, (L, LANES), 0)
    lane = lax.broadcasted_iota(jnp.int32, (L, LANES), 1)
    causal = row >= lane
    lo_half = lane < SSD_HEAD_DIM

    def bc(v, h):
        return jnp.broadcast_to(v[:, h:h + 1], (L, LANES))

    for g in range(SSD_GROUPS):
        b_g = xbc_ref[:, SSD_INNER + g * SSD_STATE:SSD_INNER + (g + 1) * SSD_STATE].astype(BF16)
        c_g = xbc_ref[:, SSD_INNER + SSD_BC + g * SSD_STATE:SSD_INNER + SSD_BC + (g + 1) * SSD_STATE].astype(BF16)
        hrows = slice(g * SSD_GW, (g + 1) * SSD_GW)
        cb = _dot_nt(c_g, b_g)
        yoff_raw = _dot_nt(c_g, h_ref[hrows, :].astype(BF16))
        end_decay = []
        gated = []
        for pr in range(SSD_HPG // 2):
            h1 = g * SSD_HPG + 2 * pr
            h2 = h1 + 1
            cols = slice(h1 * SSD_HEAD_DIM, (h1 + 2) * SSD_HEAD_DIM)
            a1 = bc(cs, h1)
            a2 = bc(cs, h2)
            m1 = cb * jnp.exp(jnp.where(causal, a1 - cs_t[h1:h1 + 1, :], NEG_INF))
            m2 = cb * jnp.exp(jnp.where(causal, a2 - cs_t[h2:h2 + 1, :], NEG_INF))
            xs = xbc_ref[:, cols]
            xc = xs * jnp.where(lo_half, bc(dt, h1), bc(dt, h2))
            lhs = jnp.concatenate([m1, m2], axis=1).astype(BF16)
            rhs = jnp.concatenate([jnp.where(lo_half, xc, 0.0), jnp.where(lo_half, 0.0, xc)], axis=0).astype(BF16)
            y_diag = _dot(lhs, rhs)
            csp = jnp.where(lo_half, a1, a2)
            y_off = jnp.exp(csp) * yoff_raw[:, 2 * pr * SSD_HEAD_DIM:(2 * pr + 2) * SSD_HEAD_DIM]
            xdec_ref[:, cols] = (xc * jnp.exp(csp[L - 1:L, :] - csp)).astype(BF16)
            end_decay += [jnp.exp(a1[L - 1:L, :]), jnp.exp(a2[L - 1:L, :])]
            y = y_diag + y_off + dskip_ref[:, cols] * xs
            gated.append(y * _silu(uzx_ref[:, cols]))
        ss = sum(jnp.sum(v * v, axis=-1, keepdims=True) for v in gated)
        scale = lax.rsqrt(ss * (1.0 / SSD_GW) + RMS_EPS)
        for pr in range(SSD_HPG // 2):
            cols = slice(g * SSD_GW + pr * LANES, g * SSD_GW + (pr + 1) * LANES)
            y_ref[:, cols] = gated[pr] * scale * ng_ref[:, cols]
        st = _dot_tn(xdec_ref[:, hrows], b_g)
        for k in range(SSD_HPG):
            hs = slice(g * SSD_GW + k * SSD_HEAD_DIM, g * SSD_GW + (k + 1) * SSD_HEAD_DIM)
            h_ref[hs, :] = h_ref[hs, :] * end_decay[k] + st[k * SSD_HEAD_DIM:(k + 1) * SSD_HEAD_DIM, :]

    @pl.when(c == n_chunks - 1)
    def _():
        state_ref[0] = h_ref[...]


def _ssd_prompt(uzx, udt, cw, cb, dtb, alog, dskip, ng, tri, *, bsz, t):
    nc = t // SSD_CHUNK
    L = SSD_CHUNK
    cur = lambda b, c: (b * nc + c, 0)
    const = lambda b, c: (0, 0)
    return pl.pallas_call(
        functools.partial(_ssd_prompt_kernel, n_chunks=nc),
        grid=(bsz, nc),
        in_specs=[pl.BlockSpec((L, SSD_ZX), cur),
                  pl.BlockSpec((L, LANES), cur),
                  pl.BlockSpec((SSD_CONV, SSD_CONV_CH), const),
                  pl.BlockSpec((1, SSD_CONV_CH), const),
                  pl.BlockSpec((1, LANES), const),
                  pl.BlockSpec((1, LANES), const),
                  pl.BlockSpec((1, SSD_INNER), const),
                  pl.BlockSpec((1, SSD_INNER), const),
                  pl.BlockSpec((L, L), const)],
        out_specs=[pl.BlockSpec((L, SSD_INNER), cur),
                   pl.BlockSpec((1, SSD_INNER, SSD_STATE), lambda b, c: (b, 0, 0))],
        out_shape=[jax.ShapeDtypeStruct((bsz * t, SSD_INNER), F32),
                   jax.ShapeDtypeStruct((bsz, SSD_INNER, SSD_STATE), F32)],
        scratch_shapes=[pltpu.VMEM((SSD_HALO + L, SSD_CONV_CH), F32),
                        pltpu.VMEM((L, SSD_CONV_CH), F32),
                        pltpu.VMEM((SSD_INNER, SSD_STATE), F32),
                        pltpu.VMEM((L, SSD_INNER), BF16)],
        compiler_params=_cparams(2),
        name="ssd_prompt",
    )(uzx, udt, cw, cb, dtb, alog, dskip, ng, tri)


def _conf_sample_kernel(u_ref, hist_ref, dw_ref, db_ref, lng_ref, lnb_ref, c_ref, a_ref):
    a = _glu(u_ref[:, 0:GLU_W])
    a_ref[...] = a
    nh = CONF_KERNEL - 1
    acc = db_ref[...] + jnp.sum(hist_ref[...] * dw_ref[0:nh, :], axis=1) + a * dw_ref[nh:nh + 1, :]
    mu = jnp.mean(acc, axis=-1, keepdims=True)
    xc = acc - mu
    var = jnp.mean(xc * xc, axis=-1, keepdims=True)
    c_ref[...] = _silu(xc * lax.rsqrt(var + LN_EPS) * lng_ref[...] + lnb_ref[...])


def _conf_sample(u, hist, dw, db, lng, lnb, *, bb):
    n = u.shape[0]
    nh = CONF_KERNEL - 1
    const = lambda i: (0, 0)
    return pl.pallas_call(
        _conf_sample_kernel,
        grid=(n // bb,),
        in_specs=[pl.BlockSpec((bb, AB_IN), lambda i: (i, 0)),
                  pl.BlockSpec((bb, nh, CONF_CH), lambda i: (i, 0, 0)),
                  pl.BlockSpec(dw.shape, const),
                  pl.BlockSpec((1, CONF_CH), const),
                  pl.BlockSpec((1, CONF_CH), const),
                  pl.BlockSpec((1, CONF_CH), const)],
        out_specs=[pl.BlockSpec((bb, CONF_CH), lambda i: (i, 0)),
                   pl.BlockSpec((bb, CONF_CH), lambda i: (i, 0))],
        out_shape=[jax.ShapeDtypeStruct((n, CONF_CH), F32),
                   jax.ShapeDtypeStruct((n, CONF_CH), F32)],
        compiler_params=_cparams(1),
        name="conf_sample",
    )(u, hist, dw, db, lng, lnb)


def _attn_sample_kernel(qm_ref, kc_ref, vc_ref, kn_ref, vn_ref, sink_ref, o_ref, *, wc):
    q = qm_ref[...] * ATT_SCALE
    s = jnp.einsum("bhc,bsc->bhs", q.astype(BF16), kc_ref[...].astype(BF16), preferred_element_type=F32)
    kpos = lax.broadcasted_iota(jnp.int32, s.shape, 2)
    s = jnp.where((kpos <= wc) & (wc - kpos < WINDOW), s, NEG_INF)
    s_new = jnp.sum(q * kn_ref[...], axis=-1, keepdims=True)
    sink = sink_ref[...][None]
    m = jnp.maximum(jnp.maximum(jnp.max(s, axis=-1, keepdims=True), s_new), sink)
    p = jnp.exp(s - m)
    p_new = jnp.exp(s_new - m)
    den = jnp.sum(p, axis=-1, keepdims=True) + p_new + jnp.exp(sink - m)
    o = jnp.einsum("bhs,bsc->bhc", p.astype(BF16), vc_ref[...].astype(BF16), preferred_element_type=F32)
    o_ref[...] = (o + p_new * vn_ref[...]) / den


def _attn_sample(qm, kc, vc, kn, vn, sink, *, bb):
    n, wc, _ = kc.shape
    blk3 = lambda i: (i, 0, 0)
    return pl.pallas_call(
        functools.partial(_attn_sample_kernel, wc=wc),
        grid=(n // bb,),
        in_specs=[pl.BlockSpec((bb, N_Q_HEADS, KV_W), blk3),
                  pl.BlockSpec((bb, wc, KV_W), blk3),
                  pl.BlockSpec((bb, wc, KV_W), blk3),
                  pl.BlockSpec((bb, 1, KV_W), blk3),
                  pl.BlockSpec((bb, 1, KV_W), blk3),
                  pl.BlockSpec((N_Q_HEADS, 1), lambda i: (0, 0))],
        out_specs=pl.BlockSpec((bb, N_Q_HEADS, KV_W), blk3),
        out_shape=jax.ShapeDtypeStruct((n, N_Q_HEADS, KV_W), F32),
        compiler_params=_cparams(1),
        name="attn_sample",
    )(qm, kc, vc, kn, vn, sink)


def _ssd_sample_pre_kernel(uzx_ref, udt_ref, hist_ref, cw_ref, cb_ref, dtb_ref, alog_ref, expand_ref,
                           xbc_ref, xdt_ref, da_ref):
    nh = SSD_CONV - 1
    xbc = uzx_ref[:, SSD_INNER:SSD_ZX]
    acc = cb_ref[...] + jnp.sum(hist_ref[...] * cw_ref[0:nh, :], axis=1) + xbc * cw_ref[nh:nh + 1, :]
    conv = _silu(acc)
    xbc_ref[...] = conv
    dt = _softplus(udt_ref[...] + dtb_ref[...])
    da_ref[...] = jnp.exp(dt * (-jnp.exp(alog_ref[...])))
    xdt_ref[...] = conv[:, 0:SSD_INNER] * _dot_sel_right(dt, expand_ref[...])


def _ssd_sample_pre(uzx, udt, hist, cw, cb, dtb, alog, expand, *, bb):
    n = uzx.shape[0]
    const = lambda i: (0, 0)
    row = lambda i: (i, 0)
    return pl.pallas_call(
        _ssd_sample_pre_kernel,
        grid=(n // bb,),
        in_specs=[pl.BlockSpec((bb, SSD_ZX), row),
                  pl.BlockSpec((bb, LANES), row),
                  pl.BlockSpec((bb, SSD_CONV - 1, SSD_CONV_CH), lambda i: (i, 0, 0)),
                  pl.BlockSpec((SSD_CONV, SSD_CONV_CH), const),
                  pl.BlockSpec((1, SSD_CONV_CH), const),
                  pl.BlockSpec((1, LANES), const),
                  pl.BlockSpec((1, LANES), const),
                  pl.BlockSpec((LANES, SSD_INNER), const)],
        out_specs=[pl.BlockSpec((bb, SSD_CONV_CH), row),
                   pl.BlockSpec((bb, SSD_INNER), row),
                   pl.BlockSpec((bb, LANES), row)],
        out_shape=[jax.ShapeDtypeStruct((n, SSD_CONV_CH), F32),
                   jax.ShapeDtypeStruct((n, SSD_INNER), F32),
                   jax.ShapeDtypeStruct((n, LANES), F32)],
        compiler_params=_cparams(1),
        name="ssd_sample_pre",
    )(uzx, udt, hist, cw, cb, dtb, alog, expand)


def _ssd_sample_state_kernel(da_ref, st_ref, xdt_ref, xbc_ref, *rest, bb, has_prev):
    so_ref, yt_ref, xt_ref, xr_ref, xb_ref, yr_ref = rest[1:] if has_prev else rest
    i = pl.program_id(0)

    @pl.when(i == 0)
    def _():
        xt_ref[...] = xdt_ref[...].T
        yt_ref[...] = jnp.zeros_like(yt_ref)
        yr_ref[...] = jnp.zeros_like(yr_ref)

    xr_ref[...] = pltpu.roll(xt_ref[...], (LANES - i * bb) % LANES, 1)
    lane_g = lax.broadcasted_iota(jnp.int32, (SSD_GW, LANES), 1)
    for j in range(bb):
        b = i * bb + j
        xb_ref[...] = jnp.broadcast_to(xr_ref[:, j:j + 1], xb_ref.shape)
        for g in range(SSD_GROUPS):
            b_row = xbc_ref[j:j + 1, SSD_INNER + g * SSD_STATE:SSD_INNER + (g + 1) * SSD_STATE]
            c_row = xbc_ref[j:j + 1, SSD_INNER + SSD_BC + g * SSD_STATE:SSD_INNER + SSD_BC + (g + 1) * SSD_STATE]
            c_rows = jnp.broadcast_to(c_row, (LANES, SSD_STATE)).astype(BF16)
            parts = []
            for k in range(SSD_HPG):
                h = g * SSD_HPG + k
                hs = slice(h * SSD_HEAD_DIM, (h + 1) * SSD_HEAD_DIM)
                hn = st_ref[0, j, hs, :] * da_ref[b, h] + xb_ref[hs, :] * b_row
                so_ref[0, j, hs, :] = hn
                parts.append(hn.astype(BF16))
            y = _dot_nt(jnp.concatenate(parts, axis=0), c_rows)
            rows = slice(g * SSD_GW, (g + 1) * SSD_GW)
            yr_ref[rows, :] = jnp.where(lane_g == j, y, yr_ref[rows, :])
    lane = lax.broadcasted_iota(jnp.int32, yt_ref.shape, 1)
    mine = (lane >= i * bb) & (lane < (i + 1) * bb)
    yt_ref[...] = jnp.where(mine, pltpu.roll(yr_ref[...], i * bb, 1), yt_ref[...])


def _ssd_sample_state(da, state_all, xdt, xbc, prev_out, *, layer_idx, bb):
    n = state_all.shape[1]
    has_prev = prev_out is not None
    slab = lambda i, s: (layer_idx, i, 0, 0)
    in_specs = [pl.BlockSpec((1, bb, SSD_INNER, SSD_STATE), slab),
                pl.BlockSpec((n, SSD_INNER), lambda i, s: (0, 0)),
                pl.BlockSpec((bb, SSD_CONV_CH), lambda i, s: (i, 0))]
    operands = [da, state_all, xdt, xbc]
    if has_prev:
        in_specs.append(pl.BlockSpec(memory_space=pl.ANY))
        operands.append(prev_out)
    grid_spec = pltpu.PrefetchScalarGridSpec(
        num_scalar_prefetch=1,
        grid=(n // bb,),
        in_specs=in_specs,
        out_specs=[pl.BlockSpec((1, bb, SSD_INNER, SSD_STATE), slab),
                   pl.BlockSpec((SSD_INNER, n), lambda i, s: (0, 0))],
        scratch_shapes=[pltpu.VMEM((SSD_INNER, n), F32),
                        pltpu.VMEM((SSD_INNER, n), F32),
                        pltpu.VMEM((SSD_INNER, SSD_STATE), F32),
                        pltpu.VMEM((SSD_INNER, n), F32)],
    )
    return pl.pallas_call(
        functools.partial(_ssd_sample_state_kernel, bb=bb, has_prev=has_prev),
        grid_spec=grid_spec,
        out_shape=[jax.ShapeDtypeStruct(state_all.shape, F32),
                   jax.ShapeDtypeStruct((SSD_INNER, n), F32)],
        input_output_aliases={len(operands) - 1: 0} if has_prev else {},
        compiler_params=_cparams(1),
        name="ssd_sample_state",
    )(*operands)


def _ssd_sample_post_kernel(yt_ref, xbc_ref, uzx_ref, dskip_ref, ng_ref, o_ref):
    y = yt_ref[...].T + dskip_ref[...] * xbc_ref[:, 0:SSD_INNER]
    y = y * _silu(uzx_ref[:, 0:SSD_INNER])
    for g in range(SSD_GROUPS):
        cols = slice(g * SSD_GW, (g + 1) * SSD_GW)
        o_ref[:, cols] = _rms(y[:, cols], ng_ref[:, cols])


def _ssd_sample_post(yt, xbc, uzx, dskip, ng):
    n = xbc.shape[0]
    const = lambda i: (0, 0)
    return pl.pallas_call(
        _ssd_sample_post_kernel,
        grid=(1,),
        in_specs=[pl.BlockSpec((SSD_INNER, n), const),
                  pl.BlockSpec((n, SSD_CONV_CH), const),
                  pl.BlockSpec((n, SSD_ZX), const),
                  pl.BlockSpec((1, SSD_INNER), const),
                  pl.BlockSpec((1, SSD_INNER), const)],
        out_specs=pl.BlockSpec((n, SSD_INNER), const),
        out_shape=jax.ShapeDtypeStruct((n, SSD_INNER), F32),
        compiler_params=_cparams(1),
        name="ssd_sample_post",
    )(yt, xbc, uzx, dskip, ng)


def _pad_lanes(v):
    return jnp.pad(v.astype(F32), (0, LANES - v.shape[0]))[None, :]


def _head_expand(v):
    return jnp.repeat(v.astype(F32), SSD_HEAD_DIM)[None, :]


def _prep_layer(P, layer):
    i = layer // 2
    g = P["norm_g"][layer]
    w = {"g": [g[k][None, :] for k in range(4)],
         "w_up": P["mlp_w_up"][layer].astype(BF16),
         "w_down": P["mlp_w_down"][layer].astype(BF16)}
    if layer % 2 == 0:
        w.update(w_in=P["ab_w_in"][i].astype(BF16), w_out=P["ab_w_out"][i].astype(BF16),
                 dw=P["conf_dw_w"][i], db=P["conf_dw_b"][i][None, :],
                 lng=P["conf_ln_g"][i][None, :], lnb=P["conf_ln_b"][i][None, :],
                 sinks=P["attn_sinks"][i])
    else:
        w_in = P["ssd_w_in"][i]
        w_dt = jnp.pad(w_in[:, SSD_ZX:], ((0, 0), (0, LANES - SSD_HEADS)))
        w.update(w_zx=w_in[:, :SSD_ZX].astype(BF16), w_dt=w_dt.astype(BF16),
                 w_out=P["ssd_w_out"][i].astype(BF16),
                 cw=P["ssd_conv_w"][i], cb=P["ssd_conv_b"][i][None, :],
                 dtb=_pad_lanes(P["ssd_dt_bias"][i]), alog=_pad_lanes(P["ssd_a_log"][i]),
                 dskip=_head_expand(P["ssd_d"][i]), ng=P["ssd_norm_g"][i][None, :])
    return w


def _trunk_prompt(x3, W):
    bsz, t, d = x3.shape
    x = x3.reshape(bsz * t, d)
    tri = jnp.tril(jnp.ones((SSD_CHUNK, SSD_CHUNK), F32)).astype(BF16)
    win_k, win_v, conf_rows, ssm_states, ssd_rows = [], [], [], [], []
    tq = 512
    for layer in range(4):
        w = W[layer]
        g = w["g"]
        if layer % 2 == 0:
            u = _norm_matmul(x, g[0], w["w_in"], tm=1024, tn=AB_IN // 2)
            x, tails = _ab_prompt(u, x, w["sinks"], w["dw"], w["db"], w["lng"], w["lnb"], w["w_out"], g[1],
                                  bsz=bsz, t=t, tq=tq)
            u3 = u.reshape(bsz, t, AB_IN)
            wp = min(WINDOW, t)
            k_off = GLU_W + Q_W
            win_k.append(u3[:, t - wp:, k_off:k_off + KV_W].reshape(bsz, wp, N_KV_HEADS, HEAD_DIM))
            win_v.append(u3[:, t - wp:, k_off + KV_W:].reshape(bsz, wp, N_KV_HEADS, HEAD_DIM))
            tails = tails.reshape(bsz, t // tq, CONV_HALO, CONF_CH)
            conf_rows.append(tails[:, -1, CONV_HALO - (CONF_KERNEL - 1):, :])
        else:
            uzx = _norm_matmul(x, g[0], w["w_zx"], tm=1024, tn=1024)
            udt = _norm_matmul(x, g[0], w["w_dt"], tm=1024, tn=LANES)
            y, state = _ssd_prompt(uzx, udt, w["cw"], w["cb"], w["dtb"], w["alog"], w["dskip"], w["ng"], tri,
                                   bsz=bsz, t=t)
            x = _proj_norm_res(y, w["w_out"], x, g[1], tm=512)
            ssm_states.append(state.reshape(bsz, SSD_HEADS, SSD_HEAD_DIM, SSD_STATE))
            ssd_rows.append(uzx.reshape(bsz, t, SSD_ZX)[:, t - (SSD_CONV - 1):, SSD_INNER:])
        x = _mlp(x, g[2], w["w_up"], w["w_down"], g[3], tm=512)
    return (x.reshape(bsz, t, d), jnp.stack(win_k), jnp.stack(win_v), jnp.stack(conf_rows),
            jnp.stack(ssm_states), jnp.stack(ssd_rows))


def _trunk_sample(x3, W, cache_k, cache_v, conf_hist, ssm, ssd_hist):
    n, t, d = x3.shape
    x = x3.reshape(n, d)
    expand = jnp.repeat(jnp.eye(LANES, SSD_HEADS, dtype=F32), SSD_HEAD_DIM, axis=1).astype(BF16)
    win_k, win_v, conf_rows, ssd_rows = [], [], [], []
    ssm_out = None
    head_group = jnp.arange(N_Q_HEADS) // Q_PER_KV
    for layer in range(4):
        w = W[layer]
        g = w["g"]
        i = layer // 2
        if layer % 2 == 0:
            u = _norm_matmul(x, g[0], w["w_in"], tm=n, tn=AB_IN // 2)
            c_out, a = _conf_sample(u, conf_hist[i], w["dw"], w["db"], w["lng"], w["lnb"], bb=32)
            k_off = GLU_W + Q_W
            q = u[:, GLU_W:k_off].reshape(n, N_Q_HEADS, HEAD_DIM)
            k_new = u[:, k_off:k_off + KV_W]
            v_new = u[:, k_off + KV_W:]
            lane_group = jnp.arange(KV_W) // HEAD_DIM
            qm = jnp.where(head_group[None, :, None] == lane_group[None, None, :],
                           jnp.tile(q, (1, 1, N_KV_HEADS)), 0.0)
            wc = cache_k.shape[2]
            o = _attn_sample(qm, cache_k[i].reshape(n, wc, KV_W), cache_v[i].reshape(n, wc, KV_W),
                             k_new[:, None, :], v_new[:, None, :], w["sinks"][:, None], bb=32)
            o = o.reshape(n, N_Q_HEADS, N_KV_HEADS, HEAD_DIM)
            a_out = jnp.take_along_axis(o, head_group[None, :, None, None], axis=2).reshape(n, Q_W)
            x = _proj_norm_res(jnp.concatenate([c_out, a_out], axis=-1), w["w_out"], x, g[1], tm=n)
            win_k.append(k_new.reshape(n, 1, N_KV_HEADS, HEAD_DIM))
            win_v.append(v_new.reshape(n, 1, N_KV_HEADS, HEAD_DIM))
            conf_rows.append(a[:, None, :])
        else:
            uzx = _norm_matmul(x, g[0], w["w_zx"], tm=n, tn=1024)
            udt = _norm_matmul(x, g[0], w["w_dt"], tm=n, tn=LANES)
            xbc, xdt, da = _ssd_sample_pre(uzx, udt, ssd_hist[i], w["cw"], w["cb"], w["dtb"], w["alog"], expand,
                                           bb=32)
            ssm_out, yt = _ssd_sample_state(da, ssm.reshape(ssm.shape[0], n, SSD_INNER, SSD_STATE), xdt, xbc,
                                            ssm_out, layer_idx=i, bb=8)
            y = _ssd_sample_post(yt, xbc, uzx, w["dskip"], w["ng"])
            x = _proj_norm_res(y, w["w_out"], x, g[1], tm=n)
            ssd_rows.append(uzx[:, None, SSD_INNER:])
        x = _mlp(x, g[2], w["w_up"], w["w_down"], g[3], tm=n)
    return (x.reshape(n, t, d), jnp.stack(win_k), jnp.stack(win_v), jnp.stack(conf_rows),
            ssm_out.reshape(ssm.shape), jnp.stack(ssd_rows))


def kernel(x_prompt, x_sample, cache_win_k, cache_win_v, state_conf_conv, state_ssm, state_ssd_conv,
           norm_g, ab_w_in, conf_dw_w, conf_dw_b, conf_ln_g, conf_ln_b, attn_sinks, ab_w_out,
           ssd_w_in, ssd_conv_w, ssd_conv_b, ssd_dt_bias, ssd_a_log, ssd_d, ssd_norm_g, ssd_w_out,
           mlp_w_up, mlp_w_down):
    P = {"norm_g": norm_g, "ab_w_in": ab_w_in, "conf_dw_w": conf_dw_w, "conf_dw_b": conf_dw_b,
         "conf_ln_g": conf_ln_g, "conf_ln_b": conf_ln_b, "attn_sinks": attn_sinks, "ab_w_out": ab_w_out,
         "ssd_w_in": ssd_w_in, "ssd_conv_w": ssd_conv_w, "ssd_conv_b": ssd_conv_b, "ssd_dt_bias": ssd_dt_bias,
         "ssd_a_log": ssd_a_log, "ssd_d": ssd_d, "ssd_norm_g": ssd_norm_g, "ssd_w_out": ssd_w_out,
         "mlp_w_up": mlp_w_up, "mlp_w_down": mlp_w_down}
    W = [_prep_layer(P, layer) for layer in range(4)]
    y_p, wk_p, wv_p, cc_p, ssm_p, sc_p = _trunk_prompt(x_prompt, W)
    y_s, wk_s, wv_s, cc_s, ssm_s, sc_s = _trunk_sample(x_sample, W, cache_win_k, cache_win_v,
                                                       state_conf_conv, state_ssm, state_ssd_conv)
    return (y_p, y_s, wk_p, wv_p, wk_s, wv_s, cc_p, cc_s, ssm_p, ssm_s, sc_p, sc_s)
```

```python
import functools

import jax
import jax.numpy as jnp
from jax import lax
from jax.experimental import pallas as pl
from jax.experimental.pallas import tpu as pltpu

F32 = jnp.float32
BF16 = jnp.bfloat16

D_MODEL = 1024
CONF_CH = 512
CONF_KERNEL = 31
HEAD_DIM = 64
N_Q_HEADS = 8
N_KV_HEADS = 2
Q_PER_KV = N_Q_HEADS // N_KV_HEADS
WINDOW = 128
ATT_BLOCK = 128
ATT_SCALE = HEAD_DIM ** -0.5
Q_W = N_Q_HEADS * HEAD_DIM
KV_W = N_KV_HEADS * HEAD_DIM
GLU_W = 2 * CONF_CH
AB_IN = GLU_W + Q_W + 2 * KV_W
SSD_INNER = 2048
SSD_HEAD_DIM = 64
SSD_HEADS = 32
SSD_GROUPS = 4
SSD_HPG = SSD_HEADS // SSD_GROUPS
SSD_STATE = 128
SSD_CONV = 4
SSD_BC = SSD_GROUPS * SSD_STATE
SSD_CONV_CH = SSD_INNER + 2 * SSD_BC
SSD_ZX = SSD_INNER + SSD_CONV_CH
SSD_GW = SSD_INNER // SSD_GROUPS
SSD_CHUNK = 128
D_FF = 4096
RMS_EPS = 1e-6
LN_EPS = 1e-5

LANES = 128
SUBLANES = 8
VMEM_LIMIT = 56 * 1024 * 1024

NEG_INF = float("-inf")


def _cparams(n_axes):
    return pltpu.CompilerParams(dimension_semantics=("arbitrary",) * n_axes,
                                vmem_limit_bytes=VMEM_LIMIT)


def _rms(x, g):
    return x * lax.rsqrt(jnp.mean(x * x, axis=-1, keepdims=True) + RMS_EPS) * g


def _sigmoid(x):
    return 1.0 / (1.0 + jnp.exp(-x))


def _silu(x):
    return x * _sigmoid(x)


def _log1p(e):
    u = 1.0 + e
    tiny = u == 1.0
    return jnp.where(tiny, e, jnp.log(u) * (e / jnp.where(tiny, 1.0, u - 1.0)))


def _softplus(x):
    return jnp.maximum(x, 0.0) + _log1p(jnp.exp(-jnp.abs(x)))


def _dot(a, b):
    return jnp.dot(a, b, preferred_element_type=F32)


def _dot_nt(a, b):
    return lax.dot_general(a, b, (((1,), (1,)), ((), ())), preferred_element_type=F32)


def _dot_tn(a, b):
    return lax.dot_general(a, b, (((0,), (0,)), ((), ())), preferred_element_type=F32)


def _split3(x):
    hi = x.astype(BF16)
    r1 = x - hi.astype(F32)
    mid = r1.astype(BF16)
    lo = (r1 - mid.astype(F32)).astype(BF16)
    return hi, mid, lo


def _dot_sel_left(sel, x):
    hi, mid, lo = _split3(x)
    return _dot(sel, hi) + _dot(sel, mid) + _dot(sel, lo)


def _dot_sel_right(x, sel):
    hi, mid, lo = _split3(x)
    return _dot(hi, sel) + _dot(mid, sel) + _dot(lo, sel)


def _norm_matmul_kernel(x_ref, g_ref, w_ref, o_ref, hn_ref):
    @pl.when(pl.program_id(1) == 0)
    def _():
        hn_ref[...] = _rms(x_ref[...], g_ref[...]).astype(BF16)

    o_ref[...] = _dot(hn_ref[...], w_ref[...])


def _norm_matmul(x, g, w, *, tm, tn):
    m, d = x.shape
    n = w.shape[1]
    return pl.pallas_call(
        _norm_matmul_kernel,
        grid=(m // tm, n // tn),
        in_specs=[pl.BlockSpec((tm, d), lambda i, j: (i, 0)),
                  pl.BlockSpec((1, d), lambda i, j: (0, 0)),
                  pl.BlockSpec((d, tn), lambda i, j: (0, j))],
        out_specs=pl.BlockSpec((tm, tn), lambda i, j: (i, j)),
        out_shape=jax.ShapeDtypeStruct((m, n), F32),
        scratch_shapes=[pltpu.VMEM((tm, d), BF16)],
        compiler_params=_cparams(2),
        name="norm_matmul",
    )(x, g, w)


def _proj_norm_res_kernel(a_ref, w_ref, x_ref, g_ref, o_ref):
    y = _dot(a_ref[...].astype(BF16), w_ref[...])
    o_ref[...] = x_ref[...] + _rms(y, g_ref[...])


def _proj_norm_res(a, w, x, g, *, tm):
    m, k = a.shape
    d = w.shape[1]
    return pl.pallas_call(
        _proj_norm_res_kernel,
        grid=(m // tm,),
        in_specs=[pl.BlockSpec((tm, k), lambda i: (i, 0)),
                  pl.BlockSpec((k, d), lambda i: (0, 0)),
                  pl.BlockSpec((tm, d), lambda i: (i, 0)),
                  pl.BlockSpec((1, d), lambda i: (0, 0))],
        out_specs=pl.BlockSpec((tm, d), lambda i: (i, 0)),
        out_shape=jax.ShapeDtypeStruct((m, d), F32),
        compiler_params=_cparams(1),
        name="proj_norm_res",
    )(a, w, x, g)


def _mlp_kernel(x_ref, gpre_ref, wup_ref, wdn_ref, gpost_ref, o_ref, *, tf):
    x = x_ref[...]
    h = _rms(x, gpre_ref[...]).astype(BF16)
    f = None
    for c in range(D_FF // tf):
        a = jnp.maximum(_dot(h, wup_ref[:, c * tf:(c + 1) * tf]), 0.0)
        part = _dot((a * a).astype(BF16), wdn_ref[c * tf:(c + 1) * tf, :])
        f = part if f is None else f + part
    o_ref[...] = x + _rms(f, gpost_ref[...])


def _mlp(x, gpre, wup, wdn, gpost, *, tm, tf=1024):
    m, d = x.shape
    return pl.pallas_call(
        functools.partial(_mlp_kernel, tf=tf),
        grid=(m // tm,),
        in_specs=[pl.BlockSpec((tm, d), lambda i: (i, 0)),
                  pl.BlockSpec((1, d), lambda i: (0, 0)),
                  pl.BlockSpec((d, D_FF), lambda i: (0, 0)),
                  pl.BlockSpec((D_FF, d), lambda i: (0, 0)),
                  pl.BlockSpec((1, d), lambda i: (0, 0))],
        out_specs=pl.BlockSpec((tm, d), lambda i: (i, 0)),
        out_shape=jax.ShapeDtypeStruct((m, d), F32),
        compiler_params=_cparams(1),
        name="mlp",
    )(x, gpre, wup, wdn, gpost)


CONV_HALO = 32
CONV_ROWS = 64


def _glu(u):
    return u[:, :CONF_CH] * _sigmoid(u[:, CONF_CH:GLU_W])


def _ab_prompt_kernel(sink_ref, u_ref, uglu_prev_ref, ukv_prev_ref, x_ref, dw_ref, db_ref, lng_ref, lnb_ref,
                      wout_ref, g_ref, o_ref, tail_ref, ap_ref, sh_ref, mix_ref, s_ref, p_ref, *, tq):
    i = pl.program_id(1)
    has_prev = i > 0

    n_ap = CONV_HALO + tq
    a_prev = _glu(uglu_prev_ref[...])
    ap_ref[0:CONV_HALO, :] = jnp.where(has_prev, a_prev, 0.0)
    ap_ref[CONV_HALO:n_ap, :] = _glu(u_ref[:, 0:GLU_W])
    ap_ref[n_ap:n_ap + SUBLANES, :] = jnp.zeros((SUBLANES, CONF_CH), F32)
    tail_ref[...] = ap_ref[tq:tq + CONV_HALO, :]
    for r in range(1, SUBLANES):
        sh_ref[r - 1] = ap_ref[r:r + n_ap, :]

    first_tap = CONV_HALO - (CONF_KERNEL - 1)
    for rc in range(tq // CONV_ROWS):
        acc = jnp.broadcast_to(db_ref[...], (CONV_ROWS, CONF_CH))
        for k in range(CONF_KERNEL):
            tile, r = divmod(first_tap + k, SUBLANES)
            start = rc * CONV_ROWS + tile * SUBLANES
            if r == 0:
                rows = ap_ref[start:start + CONV_ROWS, :]
            else:
                rows = sh_ref[r - 1, start:start + CONV_ROWS, :]
            acc = acc + rows * dw_ref[k:k + 1, :]
        mu = jnp.mean(acc, axis=-1, keepdims=True)
        xc = acc - mu
        var = jnp.mean(xc * xc, axis=-1, keepdims=True)
        ln = xc * lax.rsqrt(var + LN_EPS) * lng_ref[...] + lnb_ref[...]
        mix_ref[rc * CONV_ROWS:(rc + 1) * CONV_ROWS, 0:CONF_CH] = _silu(ln).astype(BF16)

    n_blk = tq // ATT_BLOCK
    n_keys = 2 * ATT_BLOCK
    grp_rows = Q_PER_KV * ATT_BLOCK
    blk_rows = N_Q_HEADS * ATT_BLOCK
    k_off = GLU_W + Q_W
    v_off = k_off + KV_W

    def kv_rows(j, off):
        rows = slice(j * ATT_BLOCK, (j + 1) * ATT_BLOCK)
        if j == 0:
            prev = ukv_prev_ref[:, off - k_off:off - k_off + KV_W]
        else:
            prev = u_ref[(j - 1) * ATT_BLOCK:j * ATT_BLOCK, off:off + KV_W]
        return jnp.concatenate([prev, u_ref[rows, off:off + KV_W]], axis=0)

    col0 = lax.broadcasted_iota(jnp.int32, (grp_rows, n_keys), 1) == 0
    for j in range(n_blk):
        rows = slice(j * ATT_BLOCK, (j + 1) * ATT_BLOCK)
        q = (u_ref[rows, GLU_W:GLU_W + Q_W] * ATT_SCALE).astype(BF16)
        kk = kv_rows(j, k_off).astype(BF16)
        for g in range(N_KV_HEADS):
            heads = [g * Q_PER_KV + hh for hh in range(Q_PER_KV)]
            qg = jnp.concatenate([q[:, h * HEAD_DIM:(h + 1) * HEAD_DIM] for h in heads], axis=0)
            s = _dot_nt(qg, kk[:, g * HEAD_DIM:(g + 1) * HEAD_DIM])
            sink = jnp.concatenate([jnp.full((ATT_BLOCK, 1), sink_ref[h], F32) for h in heads], axis=0)
            base = j * blk_rows + g * grp_rows
            s_ref[base:base + grp_rows, :] = jnp.where(col0, sink, s)

    qi = lax.broadcasted_iota(jnp.int32, (ATT_BLOCK, n_keys), 0)
    kc = lax.broadcasted_iota(jnp.int32, (ATT_BLOCK, n_keys), 1)
    visible = ((kc > qi) & (kc <= qi + WINDOW)) | (kc == 0)
    visible_first = visible & ((kc >= ATT_BLOCK) | (kc == 0) | has_prev)
    bias = jnp.where(visible, 0.0, NEG_INF)
    bias_first = jnp.where(visible_first, 0.0, NEG_INF)
    for lo, hi, bb in ((0, blk_rows, bias_first), (blk_rows, n_blk * blk_rows, bias)):
        sm = s_ref[lo:hi, :].reshape((hi - lo) // ATT_BLOCK, ATT_BLOCK, n_keys) + bb
        m = jnp.max(sm, axis=-1, keepdims=True)
        p_ref[lo:hi, :] = jnp.exp(sm - m).astype(BF16).reshape(hi - lo, n_keys)

    vrow = lax.broadcasted_iota(jnp.int32, (n_keys, HEAD_DIM), 0)
    ones = jnp.ones((n_keys, HEAD_DIM), BF16)
    lo_half = lax.broadcasted_iota(jnp.int32, (ATT_BLOCK, 2 * HEAD_DIM), 1) < HEAD_DIM
    for j in range(n_blk):
        rows = slice(j * ATT_BLOCK, (j + 1) * ATT_BLOCK)
        vv = kv_rows(j, v_off)
        for g in range(N_KV_HEADS):
            vg = jnp.where(vrow > 0, vv[:, g * HEAD_DIM:(g + 1) * HEAD_DIM], 0.0).astype(BF16)
            v_ext = (jnp.concatenate([vg, ones], axis=1), jnp.concatenate([ones, vg], axis=1))
            for pr in range(Q_PER_KV // 2):
                normed = []
                for par in range(2):
                    base = j * blk_rows + g * grp_rows + (2 * pr + par) * ATT_BLOCK
                    oe = _dot(p_ref[base:base + ATT_BLOCK, :], v_ext[par])
                    normed.append(oe / pltpu.roll(oe, HEAD_DIM, 1))
                c0 = CONF_CH + (g * Q_PER_KV + 2 * pr) * HEAD_DIM
                mix_ref[rows, c0:c0 + 2 * HEAD_DIM] = jnp.where(lo_half, normed[0], normed[1]).astype(BF16)

    y = _dot(mix_ref[...], wout_ref[...])
    o_ref[...] = x_ref[...] + _rms(y, g_ref[...])


def _ab_prompt(u, x, sinks, dw, db, lng, lnb, wout, g, *, bsz, t, tq):
    nt = t // tq
    m = bsz * t
    d = x.shape[1]

    def cur(b, i, s):
        return (b * nt + i, 0)

    def glu_prev(b, i, s):
        return (jnp.maximum((b * nt + i) * (tq // CONV_HALO) - 1, 0), 0)

    def kv_prev(b, i, s):
        return (jnp.maximum((b * nt + i) * (tq // ATT_BLOCK) - 1, 0), (GLU_W + Q_W) // (2 * KV_W))

    const = lambda b, i, s: (0, 0)
    grid_spec = pltpu.PrefetchScalarGridSpec(
        num_scalar_prefetch=1,
        grid=(bsz, nt),
        in_specs=[pl.BlockSpec((tq, AB_IN), cur),
                  pl.BlockSpec((CONV_HALO, GLU_W), glu_prev),
                  pl.BlockSpec((ATT_BLOCK, 2 * KV_W), kv_prev),
                  pl.BlockSpec((tq, d), cur),
                  pl.BlockSpec(dw.shape, const),
                  pl.BlockSpec((1, CONF_CH), const),
                  pl.BlockSpec((1, CONF_CH), const),
                  pl.BlockSpec((1, CONF_CH), const),
                  pl.BlockSpec(wout.shape, const),
                  pl.BlockSpec((1, d), const)],
        out_specs=[pl.BlockSpec((tq, d), cur),
                   pl.BlockSpec((CONV_HALO, CONF_CH), cur)],
        scratch_shapes=[pltpu.VMEM((CONV_HALO + tq + SUBLANES, CONF_CH), F32),
                        pltpu.VMEM((SUBLANES - 1, CONV_HALO + tq, CONF_CH), F32),
                        pltpu.VMEM((tq, CONF_CH + Q_W), BF16),
                        pltpu.VMEM((tq * N_Q_HEADS, 2 * ATT_BLOCK), F32),
                        pltpu.VMEM((tq * N_Q_HEADS, 2 * ATT_BLOCK), BF16)],
    )
    return pl.pallas_call(
        functools.partial(_ab_prompt_kernel, tq=tq),
        grid_spec=grid_spec,
        out_shape=[jax.ShapeDtypeStruct((m, d), F32),
                   jax.ShapeDtypeStruct((bsz * nt * CONV_HALO, CONF_CH), F32)],
        compiler_params=_cparams(2),
        name="ab_prompt",
    )(sinks, u, u, u, x, dw, db, lng, lnb, wout, g)


SSD_HALO = SUBLANES
SSD_CONV_COLS = 256
SSD_PROJ_COLS = 512


def _ssd_conv_masks(ch):
    sub = lax.broadcasted_iota(jnp.int32, (SUBLANES, ch), 0)
    return [sub < back for back in range(SSD_CONV)]


def _ssd_conv_silu(halo, cur, w, b, wrapped):
    rows, ch = cur.shape
    tiles = [halo] + [cur[r:r + SUBLANES, :] for r in range(0, rows, SUBLANES)]
    acc = b + cur * w[SSD_CONV - 1:SSD_CONV, :]
    for back in range(1, SSD_CONV):
        rolled = [pltpu.roll(v, back, 0) for v in tiles]
        shifted = jnp.concatenate([jnp.where(wrapped[back], rolled[i], rolled[i + 1])
                                   for i in range(len(tiles) - 1)], axis=0)
        acc = acc + shifted * w[SSD_CONV - 1 - back:SSD_CONV - back, :]
    return _silu(acc)


def _ssd_prompt_kernel(x0_ref, xn_ref, g_ref, wzx_ref, wdt_ref, cw_ref, cb_ref, dtb_ref, alog_ref, dskip_ref,
                       ng_ref, tri_ref, y_ref, state_ref, tail_ref,
                       ua_ref, dta_ref, ub_ref, dtb2_ref, hn_ref, halo_ref, xbc_ref, h_ref, xdec_ref, *, n_chunks):
    c = pl.program_id(1)
    n = pl.program_id(0) * n_chunks + c

    def in_proj_pieces(x_ref, u_ref, udt_ref):
        def norm():
            hn_ref[...] = _rms(x_ref[...], g_ref[...]).astype(BF16)

        def zx(cc):
            cols = slice(cc * SSD_PROJ_COLS, (cc + 1) * SSD_PROJ_COLS)
            u_ref[:, cols] = _dot(hn_ref[...], wzx_ref[:, cols])

        def dt():
            udt_ref[...] = _dot(hn_ref[...], wdt_ref[...])

        return [norm] + [functools.partial(zx, cc) for cc in range(SSD_ZX // SSD_PROJ_COLS)] + [dt]

    @pl.when(n == 0)
    def _():
        for piece in in_proj_pieces(x0_ref, ua_ref, dta_ref):
            piece()

    @pl.when(c == 0)
    def _():
        halo_ref[...] = jnp.zeros_like(halo_ref)
        h_ref[...] = jnp.zeros_like(h_ref)

    refs = (cw_ref, cb_ref, dtb_ref, alog_ref, dskip_ref, ng_ref, tri_ref, y_ref, state_ref, tail_ref,
            halo_ref, xbc_ref, h_ref, xdec_ref)

    @pl.when(n % 2 == 0)
    def _():
        _ssd_chunk(ua_ref, dta_ref, *refs, c=c, n_chunks=n_chunks,
                   fillers=in_proj_pieces(xn_ref, ub_ref, dtb2_ref))

    @pl.when(n % 2 == 1)
    def _():
        _ssd_chunk(ub_ref, dtb2_ref, *refs, c=c, n_chunks=n_chunks,
                   fillers=in_proj_pieces(xn_ref, ua_ref, dta_ref))


def _ssd_chunk(u_ref, udt_ref, cw_ref, cb_ref, dtb_ref, alog_ref, dskip_ref, ng_ref, tri_ref, y_ref, state_ref,
               tail_ref, halo_ref, xbc_ref, h_ref, xdec_ref, *, c, n_chunks, fillers):
    L = SSD_CHUNK
    n_stages = SSD_CONV_CH // SSD_CONV_COLS + SSD_HEADS // 2
    fillers = list(fillers)
    n_fill = len(fillers)
    stage = [0]

    def fill():
        stage[0] += 1
        while fillers and n_fill - len(fillers) < (stage[0] * n_fill + n_stages - 1) // n_stages:
            fillers.pop(0)()

    wrapped = _ssd_conv_masks(SSD_CONV_COLS)
    for cc in range(SSD_CONV_CH // SSD_CONV_COLS):
        fill()
        cols = slice(cc * SSD_CONV_COLS, (cc + 1) * SSD_CONV_COLS)
        ucols = slice(SSD_INNER + cc * SSD_CONV_COLS, SSD_INNER + (cc + 1) * SSD_CONV_COLS)
        xbc_ref[:, cols] = _ssd_conv_silu(halo_ref[:, cols], u_ref[:, ucols], cw_ref[:, cols], cb_ref[:, cols],
                                          wrapped)
    halo_ref[...] = u_ref[L - SSD_HALO:L, SSD_INNER:SSD_ZX]

    dt = _softplus(udt_ref[...] + dtb_ref[...])
    a = -jnp.exp(alog_ref[...])
    cs = _dot_sel_left(tri_ref[...], dt * a)
    cs_t = cs.T

    row = lax.broadcasted_iota(jnp.int32, (L, LANES), 0)
    lane = lax.broadcasted_iota(jnp.int32, (L, LANES), 1)
    causal = row >= lane
    lo_half = lane < SSD_HEAD_DIM

    def bc(v, h):
        return jnp.broadcast_to(v[:, h:h + 1], (L, LANES))

    for g in range(SSD_GROUPS):
        b_g = xbc_ref[:, SSD_INNER + g * SSD_STATE:SSD_INNER + (g + 1) * SSD_STATE].astype(BF16)
        c_g = xbc_ref[:, SSD_INNER + SSD_BC + g * SSD_STATE:SSD_INNER + SSD_BC + (g + 1) * SSD_STATE].astype(BF16)
        hrows = slice(g * SSD_GW, (g + 1) * SSD_GW)
        cb = _dot_nt(c_g, b_g)
        yoff_raw = _dot_nt(c_g, h_ref[hrows, :].astype(BF16))
        end_decay = []
        gated = []
        for pr in range(SSD_HPG // 2):
            fill()
            h1 = g * SSD_HPG + 2 * pr
            h2 = h1 + 1
            cols = slice(h1 * SSD_HEAD_DIM, (h1 + 2) * SSD_HEAD_DIM)
            a1 = bc(cs, h1)
            a2 = bc(cs, h2)
            m1 = cb * jnp.exp(jnp.where(causal, a1 - cs_t[h1:h1 + 1, :], NEG_INF))
            m2 = cb * jnp.exp(jnp.where(causal, a2 - cs_t[h2:h2 + 1, :], NEG_INF))
            xs = xbc_ref[:, cols]
            xc = xs * jnp.where(lo_half, bc(dt, h1), bc(dt, h2))
            lhs = jnp.concatenate([m1, m2], axis=1).astype(BF16)
            rhs = jnp.concatenate([jnp.where(lo_half, xc, 0.0), jnp.where(lo_half, 0.0, xc)], axis=0).astype(BF16)
            y_diag = _dot(lhs, rhs)
            csp = jnp.where(lo_half, a1, a2)
            y_off = jnp.exp(csp) * yoff_raw[:, 2 * pr * SSD_HEAD_DIM:(2 * pr + 2) * SSD_HEAD_DIM]
            xdec_ref[:, cols] = (xc * jnp.exp(csp[L - 1:L, :] - csp)).astype(BF16)
            end_decay += [jnp.exp(a1[L - 1:L, :]), jnp.exp(a2[L - 1:L, :])]
            y = y_diag + y_off + dskip_ref[:, cols] * xs
            gated.append(y * _silu(u_ref[:, cols]))
        ss = sum(jnp.sum(v * v, axis=-1, keepdims=True) for v in gated)
        scale = lax.rsqrt(ss * (1.0 / SSD_GW) + RMS_EPS)
        for pr in range(SSD_HPG // 2):
            cols = slice(g * SSD_GW + pr * LANES, g * SSD_GW + (pr + 1) * LANES)
            y_ref[:, cols] = (gated[pr] * scale * ng_ref[:, cols]).astype(y_ref.dtype)
        st = _dot_tn(xdec_ref[:, hrows], b_g)
        for k in range(SSD_HPG):
            hs = slice(g * SSD_GW + k * SSD_HEAD_DIM, g * SSD_GW + (k + 1) * SSD_HEAD_DIM)
            h_ref[hs, :] = h_ref[hs, :] * end_decay[k] + st[k * SSD_HEAD_DIM:(k + 1) * SSD_HEAD_DIM, :]

    assert not fillers

    @pl.when(c == n_chunks - 1)
    def _():
        state_ref[0] = h_ref[...]
        tail_ref[...] = halo_ref[...]


def _ssd_prompt(x, g, wzx, wdt, cw, cb, dtb, alog, dskip, ng, tri, *, bsz, t):
    nc = t // SSD_CHUNK
    L = SSD_CHUNK
    d = x.shape[1]
    last = bsz * nc - 1
    cur = lambda b, c: (b * nc + c, 0)
    nxt = lambda b, c: (jnp.minimum(b * nc + c + 1, last), 0)
    const = lambda b, c: (0, 0)
    resident = pl.Buffered(1)
    return pl.pallas_call(
        functools.partial(_ssd_prompt_kernel, n_chunks=nc),
        grid=(bsz, nc),
        in_specs=[pl.BlockSpec((L, d), const),
                  pl.BlockSpec((L, d), nxt),
                  pl.BlockSpec((1, d), const),
                  pl.BlockSpec((d, SSD_ZX), const, pipeline_mode=resident),
                  pl.BlockSpec((d, LANES), const, pipeline_mode=resident),
                  pl.BlockSpec((SSD_CONV, SSD_CONV_CH), const),
                  pl.BlockSpec((1, SSD_CONV_CH), const),
                  pl.BlockSpec((1, LANES), const),
                  pl.BlockSpec((1, LANES), const),
                  pl.BlockSpec((1, SSD_INNER), const),
                  pl.BlockSpec((1, SSD_INNER), const),
                  pl.BlockSpec((L, L), const)],
        out_specs=[pl.BlockSpec((L, SSD_INNER), cur),
                   pl.BlockSpec((1, SSD_INNER, SSD_STATE), lambda b, c: (b, 0, 0)),
                   pl.BlockSpec((SSD_HALO, SSD_CONV_CH), lambda b, c: (b, 0))],
        out_shape=[jax.ShapeDtypeStruct((bsz * t, SSD_INNER), BF16),
                   jax.ShapeDtypeStruct((bsz, SSD_INNER, SSD_STATE), F32),
                   jax.ShapeDtypeStruct((bsz * SSD_HALO, SSD_CONV_CH), F32)],
        scratch_shapes=[pltpu.VMEM((L, SSD_ZX), F32),
                        pltpu.VMEM((L, LANES), F32),
                        pltpu.VMEM((L, SSD_ZX), F32),
                        pltpu.VMEM((L, LANES), F32),
                        pltpu.VMEM((L, d), BF16),
                        pltpu.VMEM((SSD_HALO, SSD_CONV_CH), F32),
                        pltpu.VMEM((L, SSD_CONV_CH), F32),
                        pltpu.VMEM((SSD_INNER, SSD_STATE), F32),
                        pltpu.VMEM((L, SSD_INNER), BF16)],
        compiler_params=_cparams(2),
        name="ssd_prompt",
    )(x, x, g, wzx, wdt, cw, cb, dtb, alog, dskip, ng, tri)


def _conf_sample_kernel(u_ref, hist_ref, dw_ref, db_ref, lng_ref, lnb_ref, c_ref, a_ref):
    a = _glu(u_ref[:, 0:GLU_W])
    a_ref[...] = a
    nh = CONF_KERNEL - 1
    acc = db_ref[...] + jnp.sum(hist_ref[...] * dw_ref[0:nh, :], axis=1) + a * dw_ref[nh:nh + 1, :]
    mu = jnp.mean(acc, axis=-1, keepdims=True)
    xc = acc - mu
    var = jnp.mean(xc * xc, axis=-1, keepdims=True)
    c_ref[...] = _silu(xc * lax.rsqrt(var + LN_EPS) * lng_ref[...] + lnb_ref[...])


def _conf_sample(u, hist, dw, db, lng, lnb, *, bb):
    n = u.shape[0]
    nh = CONF_KERNEL - 1
    const = lambda i: (0, 0)
    return pl.pallas_call(
        _conf_sample_kernel,
        grid=(n // bb,),
        in_specs=[pl.BlockSpec((bb, AB_IN), lambda i: (i, 0)),
                  pl.BlockSpec((bb, nh, CONF_CH), lambda i: (i, 0, 0)),
                  pl.BlockSpec(dw.shape, const),
                  pl.BlockSpec((1, CONF_CH), const),
                  pl.BlockSpec((1, CONF_CH), const),
                  pl.BlockSpec((1, CONF_CH), const)],
        out_specs=[pl.BlockSpec((bb, CONF_CH), lambda i: (i, 0)),
                   pl.BlockSpec((bb, CONF_CH), lambda i: (i, 0))],
        out_shape=[jax.ShapeDtypeStruct((n, CONF_CH), F32),
                   jax.ShapeDtypeStruct((n, CONF_CH), F32)],
        compiler_params=_cparams(1),
        name="conf_sample",
    )(u, hist, dw, db, lng, lnb)


def _attn_sample_kernel(qm_ref, kc_ref, vc_ref, kn_ref, vn_ref, sink_ref, o_ref, *, wc):
    q = qm_ref[...] * ATT_SCALE
    s = jnp.einsum("bhc,bsc->bhs", q.astype(BF16), kc_ref[...].astype(BF16), preferred_element_type=F32)
    kpos = lax.broadcasted_iota(jnp.int32, s.shape, 2)
    s = jnp.where((kpos <= wc) & (wc - kpos < WINDOW), s, NEG_INF)
    s_new = jnp.sum(q * kn_ref[...], axis=-1, keepdims=True)
    sink = sink_ref[...][None]
    m = jnp.maximum(jnp.maximum(jnp.max(s, axis=-1, keepdims=True), s_new), sink)
    p = jnp.exp(s - m)
    p_new = jnp.exp(s_new - m)
    den = jnp.sum(p, axis=-1, keepdims=True) + p_new + jnp.exp(sink - m)
    o = jnp.einsum("bhs,bsc->bhc", p.astype(BF16), vc_ref[...].astype(BF16), preferred_element_type=F32)
    o_ref[...] = (o + p_new * vn_ref[...]) / den


def _attn_sample(qm, kc, vc, kn, vn, sink, *, bb):
    n, wc, _ = kc.shape
    blk3 = lambda i: (i, 0, 0)
    return pl.pallas_call(
        functools.partial(_attn_sample_kernel, wc=wc),
        grid=(n // bb,),
        in_specs=[pl.BlockSpec((bb, N_Q_HEADS, KV_W), blk3),
                  pl.BlockSpec((bb, wc, KV_W), blk3),
                  pl.BlockSpec((bb, wc, KV_W), blk3),
                  pl.BlockSpec((bb, 1, KV_W), blk3),
                  pl.BlockSpec((bb, 1, KV_W), blk3),
                  pl.BlockSpec((N_Q_HEADS, 1), lambda i: (0, 0))],
        out_specs=pl.BlockSpec((bb, N_Q_HEADS, KV_W), blk3),
        out_shape=jax.ShapeDtypeStruct((n, N_Q_HEADS, KV_W), F32),
        compiler_params=_cparams(1),
        name="attn_sample",
    )(qm, kc, vc, kn, vn, sink)


def _ssd_sample_pre_kernel(uzx_ref, udt_ref, hist_ref, cw_ref, cb_ref, dtb_ref, alog_ref, expand_ref,
                           xbc_ref, xdt_ref, da_ref):
    nh = SSD_CONV - 1
    xbc = uzx_ref[:, SSD_INNER:SSD_ZX]
    acc = cb_ref[...] + jnp.sum(hist_ref[...] * cw_ref[0:nh, :], axis=1) + xbc * cw_ref[nh:nh + 1, :]
    conv = _silu(acc)
    xbc_ref[...] = conv
    dt = _softplus(udt_ref[...] + dtb_ref[...])
    da_ref[...] = jnp.exp(dt * (-jnp.exp(alog_ref[...])))
    xdt_ref[...] = conv[:, 0:SSD_INNER] * _dot_sel_right(dt, expand_ref[...])


def _ssd_sample_pre(uzx, udt, hist, cw, cb, dtb, alog, expand, *, bb):
    n = uzx.shape[0]
    const = lambda i: (0, 0)
    row = lambda i: (i, 0)
    return pl.pallas_call(
        _ssd_sample_pre_kernel,
        grid=(n // bb,),
        in_specs=[pl.BlockSpec((bb, SSD_ZX), row),
                  pl.BlockSpec((bb, LANES), row),
                  pl.BlockSpec((bb, SSD_CONV - 1, SSD_CONV_CH), lambda i: (i, 0, 0)),
                  pl.BlockSpec((SSD_CONV, SSD_CONV_CH), const),
                  pl.BlockSpec((1, SSD_CONV_CH), const),
                  pl.BlockSpec((1, LANES), const),
                  pl.BlockSpec((1, LANES), const),
                  pl.BlockSpec((LANES, SSD_INNER), const)],
        out_specs=[pl.BlockSpec((bb, SSD_CONV_CH), row),
                   pl.BlockSpec((bb, SSD_INNER), row),
                   pl.BlockSpec((bb, LANES), row)],
        out_shape=[jax.ShapeDtypeStruct((n, SSD_CONV_CH), F32),
                   jax.ShapeDtypeStruct((n, SSD_INNER), F32),
                   jax.ShapeDtypeStruct((n, LANES), F32)],
        compiler_params=_cparams(1),
        name="ssd_sample_pre",
    )(uzx, udt, hist, cw, cb, dtb, alog, expand)


def _ssd_sample_state_kernel(da_ref, st_ref, xdt_ref, xbc_ref, *rest, bb, has_prev):
    so_ref, yt_ref, xt_ref, xr_ref, xb_ref, yr_ref = rest[1:] if has_prev else rest
    i = pl.program_id(0)

    @pl.when(i == 0)
    def _():
        xt_ref[...] = xdt_ref[...].T
        yt_ref[...] = jnp.zeros_like(yt_ref)
        yr_ref[...] = jnp.zeros_like(yr_ref)

    xr_ref[...] = pltpu.roll(xt_ref[...], (LANES - i * bb) % LANES, 1)
    lane_g = lax.broadcasted_iota(jnp.int32, (SSD_GW, LANES), 1)
    for j in range(bb):
        b = i * bb + j
        xb_ref[...] = jnp.broadcast_to(xr_ref[:, j:j + 1], xb_ref.shape)
        for g in range(SSD_GROUPS):
            b_row = xbc_ref[j:j + 1, SSD_INNER + g * SSD_STATE:SSD_INNER + (g + 1) * SSD_STATE]
            c_row = xbc_ref[j:j + 1, SSD_INNER + SSD_BC + g * SSD_STATE:SSD_INNER + SSD_BC + (g + 1) * SSD_STATE]
            c_rows = jnp.broadcast_to(c_row, (LANES, SSD_STATE)).astype(BF16)
            parts = []
            for k in range(SSD_HPG):
                h = g * SSD_HPG + k
                hs = slice(h * SSD_HEAD_DIM, (h + 1) * SSD_HEAD_DIM)
                hn = st_ref[0, j, hs, :] * da_ref[b, h] + xb_ref[hs, :] * b_row
                so_ref[0, j, hs, :] = hn
                parts.append(hn.astype(BF16))
            y = _dot_nt(jnp.concatenate(parts, axis=0), c_rows)
            rows = slice(g * SSD_GW, (g + 1) * SSD_GW)
            yr_ref[rows, :] = jnp.where(lane_g == j, y, yr_ref[rows, :])
    lane = lax.broadcasted_iota(jnp.int32, yt_ref.shape, 1)
    mine = (lane >= i * bb) & (lane < (i + 1) * bb)
    yt_ref[...] = jnp.where(mine, pltpu.roll(yr_ref[...], i * bb, 1), yt_ref[...])


def _ssd_sample_state(da, state_all, xdt, xbc, prev_out, *, layer_idx, bb):
    n = state_all.shape[1]
    has_prev = prev_out is not None
    slab = lambda i, s: (layer_idx, i, 0, 0)
    in_specs = [pl.BlockSpec((1, bb, SSD_INNER, SSD_STATE), slab),
                pl.BlockSpec((n, SSD_INNER), lambda i, s: (0, 0)),
                pl.BlockSpec((bb, SSD_CONV_CH), lambda i, s: (i, 0))]
    operands = [da, state_all, xdt, xbc]
    if has_prev:
        in_specs.append(pl.BlockSpec(memory_space=pl.ANY))
        operands.append(prev_out)
    grid_spec = pltpu.PrefetchScalarGridSpec(
        num_scalar_prefetch=1,
        grid=(n // bb,),
        in_specs=in_specs,
        out_specs=[pl.BlockSpec((1, bb, SSD_INNER, SSD_STATE), slab),
                   pl.BlockSpec((SSD_INNER, n), lambda i, s: (0, 0))],
        scratch_shapes=[pltpu.VMEM((SSD_INNER, n), F32),
                        pltpu.VMEM((SSD_INNER, n), F32),
                        pltpu.VMEM((SSD_INNER, SSD_STATE), F32),
                        pltpu.VMEM((SSD_INNER, n), F32)],
    )
    return pl.pallas_call(
        functools.partial(_ssd_sample_state_kernel, bb=bb, has_prev=has_prev),
        grid_spec=grid_spec,
        out_shape=[jax.ShapeDtypeStruct(state_all.shape, F32),
                   jax.ShapeDtypeStruct((SSD_INNER, n), F32)],
        input_output_aliases={len(operands) - 1: 0} if has_prev else {},
        compiler_params=_cparams(1),
        name="ssd_sample_state",
    )(*operands)


def _ssd_sample_post_kernel(yt_ref, xbc_ref, uzx_ref, dskip_ref, ng_ref, o_ref):
    y = yt_ref[...].T + dskip_ref[...] * xbc_ref[:, 0:SSD_INNER]
    y = y * _silu(uzx_ref[:, 0:SSD_INNER])
    for g in range(SSD_GROUPS):
        cols = slice(g * SSD_GW, (g + 1) * SSD_GW)
        o_ref[:, cols] = _rms(y[:, cols], ng_ref[:, cols])


def _ssd_sample_post(yt, xbc, uzx, dskip, ng):
    n = xbc.shape[0]
    const = lambda i: (0, 0)
    return pl.pallas_call(
        _ssd_sample_post_kernel,
        grid=(1,),
        in_specs=[pl.BlockSpec((SSD_INNER, n), const),
                  pl.BlockSpec((n, SSD_CONV_CH), const),
                  pl.BlockSpec((n, SSD_ZX), const),
                  pl.BlockSpec((1, SSD_INNER), const),
                  pl.BlockSpec((1, SSD_INNER), const)],
        out_specs=pl.BlockSpec((n, SSD_INNER), const),
        out_shape=jax.ShapeDtypeStruct((n, SSD_INNER), F32),
        compiler_params=_cparams(1),
        name="ssd_sample_post",
    )(yt, xbc, uzx, dskip, ng)


def _pad_lanes(v):
    return jnp.pad(v.astype(F32), (0, LANES - v.shape[0]))[None, :]


def _head_expand(v):
    return jnp.repeat(v.astype(F32), SSD_HEAD_DIM)[None, :]


def _prep_layer(P, layer):
    i = layer // 2
    g = P["norm_g"][layer]
    w = {"g": [g[k][None, :] for k in range(4)],
         "w_up": P["mlp_w_up"][layer].astype(BF16),
         "w_down": P["mlp_w_down"][layer].astype(BF16)}
    if layer % 2 == 0:
        w.update(w_in=P["ab_w_in"][i].astype(BF16), w_out=P["ab_w_out"][i].astype(BF16),
                 dw=P["conf_dw_w"][i], db=P["conf_dw_b"][i][None, :],
                 lng=P["conf_ln_g"][i][None, :], lnb=P["conf_ln_b"][i][None, :],
                 sinks=P["attn_sinks"][i])
    else:
        w_in = P["ssd_w_in"][i]
        w_dt = jnp.pad(w_in[:, SSD_ZX:], ((0, 0), (0, LANES - SSD_HEADS)))
        w.update(w_zx=w_in[:, :SSD_ZX].astype(BF16), w_dt=w_dt.astype(BF16),
                 w_out=P["ssd_w_out"][i].astype(BF16),
                 cw=P["ssd_conv_w"][i], cb=P["ssd_conv_b"][i][None, :],
                 dtb=_pad_lanes(P["ssd_dt_bias"][i]), alog=_pad_lanes(P["ssd_a_log"][i]),
                 dskip=_head_expand(P["ssd_d"][i]), ng=P["ssd_norm_g"][i][None, :])
    return w


def _trunk_prompt(x3, W):
    bsz, t, d = x3.shape
    x = x3.reshape(bsz * t, d)
    tri = jnp.tril(jnp.ones((SSD_CHUNK, SSD_CHUNK), F32)).astype(BF16)
    win_k, win_v, conf_rows, ssm_states, ssd_rows = [], [], [], [], []
    tq = 512
    for layer in range(4):
        w = W[layer]
        g = w["g"]
        if layer % 2 == 0:
            u = _norm_matmul(x, g[0], w["w_in"], tm=1024, tn=AB_IN // 2)
            x, tails = _ab_prompt(u, x, w["sinks"], w["dw"], w["db"], w["lng"], w["lnb"], w["w_out"], g[1],
                                  bsz=bsz, t=t, tq=tq)
            u3 = u.reshape(bsz, t, AB_IN)
            wp = min(WINDOW, t)
            k_off = GLU_W + Q_W
            win_k.append(u3[:, t - wp:, k_off:k_off + KV_W].reshape(bsz, wp, N_KV_HEADS, HEAD_DIM))
            win_v.append(u3[:, t - wp:, k_off + KV_W:].reshape(bsz, wp, N_KV_HEADS, HEAD_DIM))
            tails = tails.reshape(bsz, t // tq, CONV_HALO, CONF_CH)
            conf_rows.append(tails[:, -1, CONV_HALO - (CONF_KERNEL - 1):, :])
        else:
            y, state, tail = _ssd_prompt(x, g[0], w["w_zx"], w["w_dt"], w["cw"], w["cb"], w["dtb"], w["alog"],
                                         w["dskip"], w["ng"], tri, bsz=bsz, t=t)
            x = _proj_norm_res(y, w["w_out"], x, g[1], tm=512)
            ssm_states.append(state.reshape(bsz, SSD_HEADS, SSD_HEAD_DIM, SSD_STATE))
            ssd_rows.append(tail.reshape(bsz, SSD_HALO, SSD_CONV_CH)[:, SSD_HALO - (SSD_CONV - 1):, :])
        x = _mlp(x, g[2], w["w_up"], w["w_down"], g[3], tm=512)
    return (x.reshape(bsz, t, d), jnp.stack(win_k), jnp.stack(win_v), jnp.stack(conf_rows),
            jnp.stack(ssm_states), jnp.stack(ssd_rows))


def _trunk_sample(x3, W, cache_k, cache_v, conf_hist, ssm, ssd_hist):
    n, t, d = x3.shape
    x = x3.reshape(n, d)
    expand = jnp.repeat(jnp.eye(LANES, SSD_HEADS, dtype=F32), SSD_HEAD_DIM, axis=1).astype(BF16)
    win_k, win_v, conf_rows, ssd_rows = [], [], [], []
    ssm_out = None
    head_group = jnp.arange(N_Q_HEADS) // Q_PER_KV
    for layer in range(4):
        w = W[layer]
        g = w["g"]
        i = layer // 2
        if layer % 2 == 0:
            u = _norm_matmul(x, g[0], w["w_in"], tm=n, tn=AB_IN // 2)
            c_out, a = _conf_sample(u, conf_hist[i], w["dw"], w["db"], w["lng"], w["lnb"], bb=32)
            k_off = GLU_W + Q_W
            q = u[:, GLU_W:k_off].reshape(n, N_Q_HEADS, HEAD_DIM)
            k_new = u[:, k_off:k_off + KV_W]
            v_new = u[:, k_off + KV_W:]
            lane_group = jnp.arange(KV_W) // HEAD_DIM
            qm = jnp.where(head_group[None, :, None] == lane_group[None, None, :],
                           jnp.tile(q, (1, 1, N_KV_HEADS)), 0.0)
            wc = cache_k.shape[2]
            o = _attn_sample(qm, cache_k[i].reshape(n, wc, KV_W), cache_v[i].reshape(n, wc, KV_W),
                             k_new[:, None, :], v_new[:, None, :], w["sinks"][:, None], bb=32)
            o = o.reshape(n, N_Q_HEADS, N_KV_HEADS, HEAD_DIM)
            a_out = jnp.take_along_axis(o, head_group[None, :, None, None], axis=2).reshape(n, Q_W)
            x = _proj_norm_res(jnp.concatenate([c_out, a_out], axis=-1), w["w_out"], x, g[1], tm=n)
            win_k.append(k_new.reshape(n, 1, N_KV_HEADS, HEAD_DIM))
            win_v.append(v_new.reshape(n, 1, N_KV_HEADS, HEAD_DIM))
            conf_rows.append(a[:, None, :])
        else:
            uzx = _norm_matmul(x, g[0], w["w_zx"], tm=n, tn=1024)
            udt = _norm_matmul(x, g[0], w["w_dt"], tm=n, tn=LANES)
            xbc, xdt, da = _ssd_sample_pre(uzx, udt, ssd_hist[i], w["cw"], w["cb"], w["dtb"], w["alog"], expand,
                                           bb=32)
            ssm_out, yt = _ssd_sample_state(da, ssm.reshape(ssm.shape[0], n, SSD_INNER, SSD_STATE), xdt, xbc,
                                            ssm_out, layer_idx=i, bb=8)
            y = _ssd_sample_post(yt, xbc, uzx, w["dskip"], w["ng"])
            x = _proj_norm_res(y, w["w_out"], x, g[1], tm=n)
            ssd_rows.append(uzx[:, None, SSD_INNER:])
        x = _mlp(x, g[2], w["w_up"], w["w_down"], g[3], tm=n)
    return (x.reshape(n, t, d), jnp.stack(win_k), jnp.stack(win_v), jnp.stack(conf_rows),
            ssm_out.reshape(ssm.shape), jnp.stack(ssd_rows))


def kernel(x_prompt, x_sample, cache_win_k, cache_win_v, state_conf_conv, state_ssm, state_ssd_conv,
           norm_g, ab_w_in, conf_dw_w, conf_dw_b, conf_ln_g, conf_ln_b, attn_sinks, ab_w_out,
           ssd_w_in, ssd_conv_w, ssd_conv_b, ssd_dt_bias, ssd_a_log, ssd_d, ssd_norm_g, ssd_w_out,
           mlp_w_up, mlp_w_down):
    P = {"norm_g": norm_g, "ab_w_in": ab_w_in, "conf_dw_w": conf_dw_w, "conf_dw_b": conf_dw_b,
         "conf_ln_g": conf_ln_g, "conf_ln_b": conf_ln_b, "attn_sinks": attn_sinks, "ab_w_out": ab_w_out,
         "ssd_w_in": ssd_w_in, "ssd_conv_w": ssd_conv_w, "ssd_conv_b": ssd_conv_b, "ssd_dt_bias": ssd_dt_bias,
         "ssd_a_log": ssd_a_log, "ssd_d": ssd_d, "ssd_norm_g": ssd_norm_g, "ssd_w_out": ssd_w_out,
         "mlp_w_up": mlp_w_up, "mlp_w_down": mlp_w_down}
    W = [_prep_layer(P, layer) for layer in range(4)]
    y_p, wk_p, wv_p, cc_p, ssm_p, sc_p = _trunk_prompt(x_prompt, W)
    y_s, wk_s, wv_s, cc_s, ssm_s, sc_s = _trunk_sample(x_sample, W, cache_win_k, cache_win_v,
                                                       state_conf_conv, state_ssm, state_ssd_conv)
    return (y_p, y_s, wk_p, wv_p, wk_s, wv_s, cc_p, cc_s, ssm_p, ssm_s, sc_p, sc_s)
```

```python
import functools

import jax
import jax.numpy as jnp
from jax import lax
from jax.experimental import pallas as pl
from jax.experimental.pallas import tpu as pltpu

F32 = jnp.float32
BF16 = jnp.bfloat16

D_MODEL = 1024
CONF_CH = 512
CONF_KERNEL = 31
HEAD_DIM = 64
N_Q_HEADS = 8
N_KV_HEADS = 2
Q_PER_KV = N_Q_HEADS // N_KV_HEADS
WINDOW = 128
ATT_BLOCK = 128
ATT_SCALE = HEAD_DIM ** -0.5
Q_W = N_Q_HEADS * HEAD_DIM
KV_W = N_KV_HEADS * HEAD_DIM
GLU_W = 2 * CONF_CH
AB_IN = GLU_W + Q_W + 2 * KV_W
SSD_INNER = 2048
SSD_HEAD_DIM = 64
SSD_HEADS = 32
SSD_GROUPS = 4
SSD_HPG = SSD_HEADS // SSD_GROUPS
SSD_STATE = 128
SSD_CONV = 4
SSD_BC = SSD_GROUPS * SSD_STATE
SSD_CONV_CH = SSD_INNER + 2 * SSD_BC
SSD_ZX = SSD_INNER + SSD_CONV_CH
SSD_GW = SSD_INNER // SSD_GROUPS
SSD_CHUNK = 128
D_FF = 4096
RMS_EPS = 1e-6
LN_EPS = 1e-5

LANES = 128
SUBLANES = 8
VMEM_LIMIT = 56 * 1024 * 1024

NEG_INF = float("-inf")


def _cparams(n_axes):
    return pltpu.CompilerParams(dimension_semantics=("arbitrary",) * n_axes,
                                vmem_limit_bytes=VMEM_LIMIT)


def _rms(x, g):
    return x * lax.rsqrt(jnp.mean(x * x, axis=-1, keepdims=True) + RMS_EPS) * g


def _sigmoid(x):
    return 1.0 / (1.0 + jnp.exp(-x))


def _silu(x):
    return x * _sigmoid(x)


def _log1p(e):
    u = 1.0 + e
    tiny = u == 1.0
    return jnp.where(tiny, e, jnp.log(u) * (e / jnp.where(tiny, 1.0, u - 1.0)))


def _softplus(x):
    return jnp.maximum(x, 0.0) + _log1p(jnp.exp(-jnp.abs(x)))


def _dot(a, b):
    return jnp.dot(a, b, preferred_element_type=F32)


def _dot_nt(a, b):
    return lax.dot_general(a, b, (((1,), (1,)), ((), ())), preferred_element_type=F32)


def _dot_tn(a, b):
    return lax.dot_general(a, b, (((0,), (0,)), ((), ())), preferred_element_type=F32)


def _split3(x):
    hi = x.astype(BF16)
    r1 = x - hi.astype(F32)
    mid = r1.astype(BF16)
    lo = (r1 - mid.astype(F32)).astype(BF16)
    return hi, mid, lo


def _dot_sel_left(sel, x):
    hi, mid, lo = _split3(x)
    return _dot(sel, hi) + _dot(sel, mid) + _dot(sel, lo)


def _dot_sel_right(x, sel):
    hi, mid, lo = _split3(x)
    return _dot(hi, sel) + _dot(mid, sel) + _dot(lo, sel)


def _norm_matmul_kernel(x_ref, g_ref, w_ref, o_ref, hn_ref):
    @pl.when(pl.program_id(1) == 0)
    def _():
        hn_ref[...] = _rms(x_ref[...], g_ref[...]).astype(BF16)

    o_ref[...] = _dot(hn_ref[...], w_ref[...])


def _norm_matmul(x, g, w, *, tm, tn):
    m, d = x.shape
    n = w.shape[1]
    return pl.pallas_call(
        _norm_matmul_kernel,
        grid=(m // tm, n // tn),
        in_specs=[pl.BlockSpec((tm, d), lambda i, j: (i, 0)),
                  pl.BlockSpec((1, d), lambda i, j: (0, 0)),
                  pl.BlockSpec((d, tn), lambda i, j: (0, j))],
        out_specs=pl.BlockSpec((tm, tn), lambda i, j: (i, j)),
        out_shape=jax.ShapeDtypeStruct((m, n), F32),
        scratch_shapes=[pltpu.VMEM((tm, d), BF16)],
        compiler_params=_cparams(2),
        name="norm_matmul",
    )(x, g, w)


def _proj_norm_res_kernel(a_ref, w_ref, x_ref, g_ref, o_ref):
    y = _dot(a_ref[...].astype(BF16), w_ref[...])
    o_ref[...] = x_ref[...] + _rms(y, g_ref[...])


def _proj_norm_res(a, w, x, g, *, tm):
    m, k = a.shape
    d = w.shape[1]
    return pl.pallas_call(
        _proj_norm_res_kernel,
        grid=(m // tm,),
        in_specs=[pl.BlockSpec((tm, k), lambda i: (i, 0)),
                  pl.BlockSpec((k, d), lambda i: (0, 0)),
                  pl.BlockSpec((tm, d), lambda i: (i, 0)),
                  pl.BlockSpec((1, d), lambda i: (0, 0))],
        out_specs=pl.BlockSpec((tm, d), lambda i: (i, 0)),
        out_shape=jax.ShapeDtypeStruct((m, d), F32),
        compiler_params=_cparams(1),
        name="proj_norm_res",
    )(a, w, x, g)


def _mlp_block(x, gpre_ref, wup_ref, wdn_ref, gpost_ref, tf):
    h = _rms(x, gpre_ref[...]).astype(BF16)
    f = None
    for c in range(D_FF // tf):
        a = jnp.maximum(_dot(h, wup_ref[:, c * tf:(c + 1) * tf]), 0.0)
        part = _dot((a * a).astype(BF16), wdn_ref[c * tf:(c + 1) * tf, :])
        f = part if f is None else f + part
    return x + _rms(f, gpost_ref[...])


def _mlp_kernel(x_ref, gpre_ref, wup_ref, wdn_ref, gpost_ref, o_ref, *, tf):
    o_ref[...] = _mlp_block(x_ref[...], gpre_ref, wup_ref, wdn_ref, gpost_ref, tf)


def _proj_mlp_kernel(a_ref, wo_ref, x_ref, gmix_ref, gpre_ref, wup_ref, wdn_ref, gpost_ref, o_ref, *, tf):
    x_mid = x_ref[...] + _rms(_dot(a_ref[...].astype(BF16), wo_ref[...]), gmix_ref[...])
    o_ref[...] = _mlp_block(x_mid, gpre_ref, wup_ref, wdn_ref, gpost_ref, tf)


def _mlp(x, gpre, wup, wdn, gpost, *, tm, tf=1024):
    m, d = x.shape
    return pl.pallas_call(
        functools.partial(_mlp_kernel, tf=tf),
        grid=(m // tm,),
        in_specs=[pl.BlockSpec((tm, d), lambda i: (i, 0)),
                  pl.BlockSpec((1, d), lambda i: (0, 0)),
                  pl.BlockSpec((d, D_FF), lambda i: (0, 0)),
                  pl.BlockSpec((D_FF, d), lambda i: (0, 0)),
                  pl.BlockSpec((1, d), lambda i: (0, 0))],
        out_specs=pl.BlockSpec((tm, d), lambda i: (i, 0)),
        out_shape=jax.ShapeDtypeStruct((m, d), F32),
        compiler_params=_cparams(1),
        name="mlp",
    )(x, gpre, wup, wdn, gpost)


def _proj_mlp(a, wo, x, gmix, gpre, wup, wdn, gpost, *, tm, tf=1024):
    m, d = x.shape
    k = a.shape[1]
    row = lambda i: (i, 0)
    const = lambda i: (0, 0)
    resident = pl.Buffered(1)
    return pl.pallas_call(
        functools.partial(_proj_mlp_kernel, tf=tf),
        grid=(m // tm,),
        in_specs=[pl.BlockSpec((tm, k), row),
                  pl.BlockSpec((k, d), const, pipeline_mode=resident),
                  pl.BlockSpec((tm, d), row),
                  pl.BlockSpec((1, d), const),
                  pl.BlockSpec((1, d), const),
                  pl.BlockSpec((d, D_FF), const, pipeline_mode=resident),
                  pl.BlockSpec((D_FF, d), const, pipeline_mode=resident),
                  pl.BlockSpec((1, d), const)],
        out_specs=pl.BlockSpec((tm, d), row),
        out_shape=jax.ShapeDtypeStruct((m, d), F32),
        compiler_params=_cparams(1),
        name="proj_mlp",
    )(a, wo, x, gmix, gpre, wup, wdn, gpost)


CONV_HALO = 32
CONV_ROWS = 64
AB_PROJ_COLS = 256


def _glu(u):
    return u[:, :CONF_CH] * _sigmoid(u[:, CONF_CH:GLU_W])


def _ab_prompt_kernel(sink_ref, x_ref, gin_ref, win_ref, dw_ref, db_ref, lng_ref, lnb_ref,
                      wout_ref, g_ref, o_ref, tail_ref, kvtail_ref,
                      u_ref, ukv_prev_ref, ap_ref, sh_ref, mix_ref, s_ref, p_ref, *, tq):
    i = pl.program_id(1)
    has_prev = i > 0
    n_ap = CONV_HALO + tq
    k_off = GLU_W + Q_W

    @pl.when(i == 0)
    def _():
        ap_ref[tq:n_ap, :] = jnp.zeros((CONV_HALO, CONF_CH), F32)
        u_ref[tq - ATT_BLOCK:tq, k_off:AB_IN] = jnp.zeros((ATT_BLOCK, 2 * KV_W), F32)

    ap_ref[0:CONV_HALO, :] = ap_ref[tq:n_ap, :]
    ukv_prev_ref[...] = u_ref[tq - ATT_BLOCK:tq, k_off:AB_IN]

    hn = _rms(x_ref[...], gin_ref[...]).astype(BF16)
    for cc in range(AB_IN // AB_PROJ_COLS):
        cols = slice(cc * AB_PROJ_COLS, (cc + 1) * AB_PROJ_COLS)
        u_ref[:, cols] = _dot(hn, win_ref[:, cols])
    kvtail_ref[...] = u_ref[tq - ATT_BLOCK:tq, k_off:AB_IN]

    ap_ref[CONV_HALO:n_ap, :] = _glu(u_ref[:, 0:GLU_W])
    ap_ref[n_ap:n_ap + SUBLANES, :] = jnp.zeros((SUBLANES, CONF_CH), F32)
    tail_ref[...] = ap_ref[tq:tq + CONV_HALO, :]
    for r in range(1, SUBLANES):
        sh_ref[r - 1] = ap_ref[r:r + n_ap, :]

    first_tap = CONV_HALO - (CONF_KERNEL - 1)
    for rc in range(tq // CONV_ROWS):
        acc = jnp.broadcast_to(db_ref[...], (CONV_ROWS, CONF_CH))
        for k in range(CONF_KERNEL):
            tile, r = divmod(first_tap + k, SUBLANES)
            start = rc * CONV_ROWS + tile * SUBLANES
            if r == 0:
                rows = ap_ref[start:start + CONV_ROWS, :]
            else:
                rows = sh_ref[r - 1, start:start + CONV_ROWS, :]
            acc = acc + rows * dw_ref[k:k + 1, :]
        mu = jnp.mean(acc, axis=-1, keepdims=True)
        xc = acc - mu
        var = jnp.mean(xc * xc, axis=-1, keepdims=True)
        ln = xc * lax.rsqrt(var + LN_EPS) * lng_ref[...] + lnb_ref[...]
        mix_ref[rc * CONV_ROWS:(rc + 1) * CONV_ROWS, 0:CONF_CH] = _silu(ln).astype(BF16)

    n_blk = tq // ATT_BLOCK
    n_keys = 2 * ATT_BLOCK
    grp_rows = Q_PER_KV * ATT_BLOCK
    blk_rows = N_Q_HEADS * ATT_BLOCK
    k_off = GLU_W + Q_W
    v_off = k_off + KV_W

    def kv_rows(j, off):
        rows = slice(j * ATT_BLOCK, (j + 1) * ATT_BLOCK)
        if j == 0:
            prev = ukv_prev_ref[:, off - k_off:off - k_off + KV_W]
        else:
            prev = u_ref[(j - 1) * ATT_BLOCK:j * ATT_BLOCK, off:off + KV_W]
        return jnp.concatenate([prev, u_ref[rows, off:off + KV_W]], axis=0)

    col0 = lax.broadcasted_iota(jnp.int32, (grp_rows, n_keys), 1) == 0
    for j in range(n_blk):
        rows = slice(j * ATT_BLOCK, (j + 1) * ATT_BLOCK)
        q = (u_ref[rows, GLU_W:GLU_W + Q_W] * ATT_SCALE).astype(BF16)
        kk = kv_rows(j, k_off).astype(BF16)
        for g in range(N_KV_HEADS):
            heads = [g * Q_PER_KV + hh for hh in range(Q_PER_KV)]
            qg = jnp.concatenate([q[:, h * HEAD_DIM:(h + 1) * HEAD_DIM] for h in heads], axis=0)
            s = _dot_nt(qg, kk[:, g * HEAD_DIM:(g + 1) * HEAD_DIM])
            sink = jnp.concatenate([jnp.full((ATT_BLOCK, 1), sink_ref[h], F32) for h in heads], axis=0)
            base = j * blk_rows + g * grp_rows
            s_ref[base:base + grp_rows, :] = jnp.where(col0, sink, s)

    qi = lax.broadcasted_iota(jnp.int32, (ATT_BLOCK, n_keys), 0)
    kc = lax.broadcasted_iota(jnp.int32, (ATT_BLOCK, n_keys), 1)
    visible = ((kc > qi) & (kc <= qi + WINDOW)) | (kc == 0)
    visible_first = visible & ((kc >= ATT_BLOCK) | (kc == 0) | has_prev)
    bias = jnp.where(visible, 0.0, NEG_INF)
    bias_first = jnp.where(visible_first, 0.0, NEG_INF)
    for lo, hi, bb in ((0, blk_rows, bias_first), (blk_rows, n_blk * blk_rows, bias)):
        sm = s_ref[lo:hi, :].reshape((hi - lo) // ATT_BLOCK, ATT_BLOCK, n_keys) + bb
        m = jnp.max(sm, axis=-1, keepdims=True)
        p_ref[lo:hi, :] = jnp.exp(sm - m).astype(BF16).reshape(hi - lo, n_keys)

    vrow = lax.broadcasted_iota(jnp.int32, (n_keys, HEAD_DIM), 0)
    ones = jnp.ones((n_keys, HEAD_DIM), BF16)
    lo_half = lax.broadcasted_iota(jnp.int32, (ATT_BLOCK, 2 * HEAD_DIM), 1) < HEAD_DIM
    for j in range(n_blk):
        rows = slice(j * ATT_BLOCK, (j + 1) * ATT_BLOCK)
        vv = kv_rows(j, v_off)
        for g in range(N_KV_HEADS):
            vg = jnp.where(vrow > 0, vv[:, g * HEAD_DIM:(g + 1) * HEAD_DIM], 0.0).astype(BF16)
            v_ext = (jnp.concatenate([vg, ones], axis=1), jnp.concatenate([ones, vg], axis=1))
            for pr in range(Q_PER_KV // 2):
                normed = []
                for par in range(2):
                    base = j * blk_rows + g * grp_rows + (2 * pr + par) * ATT_BLOCK
                    oe = _dot(p_ref[base:base + ATT_BLOCK, :], v_ext[par])
                    normed.append(oe / pltpu.roll(oe, HEAD_DIM, 1))
                c0 = CONF_CH + (g * Q_PER_KV + 2 * pr) * HEAD_DIM
                mix_ref[rows, c0:c0 + 2 * HEAD_DIM] = jnp.where(lo_half, normed[0], normed[1]).astype(BF16)

    y = _dot(mix_ref[...], wout_ref[...])
    o_ref[...] = x_ref[...] + _rms(y, g_ref[...])


def _ab_prompt(x, gin, win, sinks, dw, db, lng, lnb, wout, g, *, bsz, t, tq):
    nt = t // tq
    m = bsz * t
    d = x.shape[1]

    def cur(b, i, s):
        return (b * nt + i, 0)

    const = lambda b, i, s: (0, 0)
    resident = pl.Buffered(1)
    grid_spec = pltpu.PrefetchScalarGridSpec(
        num_scalar_prefetch=1,
        grid=(bsz, nt),
        in_specs=[pl.BlockSpec((tq, d), cur),
                  pl.BlockSpec((1, d), const),
                  pl.BlockSpec(win.shape, const, pipeline_mode=resident),
                  pl.BlockSpec(dw.shape, const),
                  pl.BlockSpec((1, CONF_CH), const),
                  pl.BlockSpec((1, CONF_CH), const),
                  pl.BlockSpec((1, CONF_CH), const),
                  pl.BlockSpec(wout.shape, const, pipeline_mode=resident),
                  pl.BlockSpec((1, d), const)],
        out_specs=[pl.BlockSpec((tq, d), cur),
                   pl.BlockSpec((CONV_HALO, CONF_CH), cur),
                   pl.BlockSpec((ATT_BLOCK, 2 * KV_W), cur)],
        scratch_shapes=[pltpu.VMEM((tq, AB_IN), F32),
                        pltpu.VMEM((ATT_BLOCK, 2 * KV_W), F32),
                        pltpu.VMEM((CONV_HALO + tq + SUBLANES, CONF_CH), F32),
                        pltpu.VMEM((SUBLANES - 1, CONV_HALO + tq, CONF_CH), F32),
                        pltpu.VMEM((tq, CONF_CH + Q_W), BF16),
                        pltpu.VMEM((tq * N_Q_HEADS, 2 * ATT_BLOCK), F32),
                        pltpu.VMEM((tq * N_Q_HEADS, 2 * ATT_BLOCK), BF16)],
    )
    return pl.pallas_call(
        functools.partial(_ab_prompt_kernel, tq=tq),
        grid_spec=grid_spec,
        out_shape=[jax.ShapeDtypeStruct((m, d), F32),
                   jax.ShapeDtypeStruct((bsz * nt * CONV_HALO, CONF_CH), F32),
                   jax.ShapeDtypeStruct((bsz * nt * ATT_BLOCK, 2 * KV_W), F32)],
        compiler_params=_cparams(2),
        name="ab_prompt",
    )(sinks, x, gin, win, dw, db, lng, lnb, wout, g)


SSD_HALO = SUBLANES
SSD_CONV_COLS = 256
SSD_PROJ_COLS = 512


def _ssd_conv_masks(ch):
    sub = lax.broadcasted_iota(jnp.int32, (SUBLANES, ch), 0)
    return [sub < back for back in range(SSD_CONV)]


def _ssd_conv_silu(halo, cur, w, b, wrapped):
    rows, ch = cur.shape
    tiles = [halo] + [cur[r:r + SUBLANES, :] for r in range(0, rows, SUBLANES)]
    acc = b + cur * w[SSD_CONV - 1:SSD_CONV, :]
    for back in range(1, SSD_CONV):
        rolled = [pltpu.roll(v, back, 0) for v in tiles]
        shifted = jnp.concatenate([jnp.where(wrapped[back], rolled[i], rolled[i + 1])
                                   for i in range(len(tiles) - 1)], axis=0)
        acc = acc + shifted * w[SSD_CONV - 1 - back:SSD_CONV - back, :]
    return _silu(acc)


def _ssd_prompt_kernel(x0_ref, xn_ref, g_ref, wzx_ref, wdt_ref, cw_ref, cb_ref, dtb_ref, alog_ref, dskip_ref,
                       ng_ref, tri_ref, y_ref, state_ref, tail_ref,
                       ua_ref, dta_ref, ub_ref, dtb2_ref, hn_ref, halo_ref, xbc_ref, h_ref, xdec_ref, *, n_chunks):
    c = pl.program_id(1)
    n = pl.program_id(0) * n_chunks + c

    def in_proj_pieces(x_ref, u_ref, udt_ref):
        def norm():
            hn_ref[...] = _rms(x_ref[...], g_ref[...]).astype(BF16)

        def zx(cc):
            cols = slice(cc * SSD_PROJ_COLS, (cc + 1) * SSD_PROJ_COLS)
            u_ref[:, cols] = _dot(hn_ref[...], wzx_ref[:, cols])

        def dt():
            udt_ref[...] = _dot(hn_ref[...], wdt_ref[...])

        return [norm] + [functools.partial(zx, cc) for cc in range(SSD_ZX // SSD_PROJ_COLS)] + [dt]

    @pl.when(n == 0)
    def _():
        for piece in in_proj_pieces(x0_ref, ua_ref, dta_ref):
            piece()

    @pl.when(c == 0)
    def _():
        halo_ref[...] = jnp.zeros_like(halo_ref)
        h_ref[...] = jnp.zeros_like(h_ref)

    refs = (cw_ref, cb_ref, dtb_ref, alog_ref, dskip_ref, ng_ref, tri_ref, y_ref, state_ref, tail_ref,
            halo_ref, xbc_ref, h_ref, xdec_ref)

    @pl.when(n % 2 == 0)
    def _():
        _ssd_chunk(ua_ref, dta_ref, *refs, c=c, n_chunks=n_chunks,
                   fillers=in_proj_pieces(xn_ref, ub_ref, dtb2_ref))

    @pl.when(n % 2 == 1)
    def _():
        _ssd_chunk(ub_ref, dtb2_ref, *refs, c=c, n_chunks=n_chunks,
                   fillers=in_proj_pieces(xn_ref, ua_ref, dta_ref))


def _ssd_chunk(z_ref, udt_ref, cw_ref, cb_ref, dtb_ref, alog_ref, dskip_ref, ng_ref, tri_ref, y_ref, state_ref,
               tail_ref, halo_ref, xbc_ref, h_ref, xdec_ref, *, c, n_chunks, fillers):
    u_ref = z_ref
    L = SSD_CHUNK
    n_stages = SSD_CONV_CH // SSD_CONV_COLS + SSD_HEADS // 2
    fillers = list(fillers)
    n_fill = len(fillers)
    stage = [0]

    def fill():
        stage[0] += 1
        while fillers and n_fill - len(fillers) < (stage[0] * n_fill + n_stages - 1) // n_stages:
            fillers.pop(0)()

    wrapped = _ssd_conv_masks(SSD_CONV_COLS)
    for cc in range(SSD_CONV_CH // SSD_CONV_COLS):
        fill()
        cols = slice(cc * SSD_CONV_COLS, (cc + 1) * SSD_CONV_COLS)
        ucols = slice(SSD_INNER + cc * SSD_CONV_COLS, SSD_INNER + (cc + 1) * SSD_CONV_COLS)
        xbc_ref[:, cols] = _ssd_conv_silu(halo_ref[:, cols], u_ref[:, ucols], cw_ref[:, cols], cb_ref[:, cols],
                                          wrapped)
    halo_ref[...] = u_ref[L - SSD_HALO:L, SSD_INNER:SSD_ZX]

    dt = _softplus(udt_ref[...] + dtb_ref[...])
    a = -jnp.exp(alog_ref[...])
    cs = _dot_sel_left(tri_ref[...], dt * a)
    cs_t = cs.T

    row = lax.broadcasted_iota(jnp.int32, (L, LANES), 0)
    lane = lax.broadcasted_iota(jnp.int32, (L, LANES), 1)
    causal = row >= lane
    lo_half = lane < SSD_HEAD_DIM

    def bc(v, h):
        return jnp.broadcast_to(v[:, h:h + 1], (L, LANES))

    for g in range(SSD_GROUPS):
        b_g = xbc_ref[:, SSD_INNER + g * SSD_STATE:SSD_INNER + (g + 1) * SSD_STATE].astype(BF16)
        c_g = xbc_ref[:, SSD_INNER + SSD_BC + g * SSD_STATE:SSD_INNER + SSD_BC + (g + 1) * SSD_STATE].astype(BF16)
        hrows = slice(g * SSD_GW, (g + 1) * SSD_GW)
        cb = _dot_nt(c_g, b_g)
        yoff_raw = _dot_nt(c_g, h_ref[hrows, :].astype(BF16))
        end_decay = []
        gated = []
        for pr in range(SSD_HPG // 2):
            fill()
            h1 = g * SSD_HPG + 2 * pr
            h2 = h1 + 1
            cols = slice(h1 * SSD_HEAD_DIM, (h1 + 2) * SSD_HEAD_DIM)
            a1 = bc(cs, h1)
            a2 = bc(cs, h2)
            m1 = cb * jnp.exp(jnp.where(causal, a1 - cs_t[h1:h1 + 1, :], NEG_INF))
            m2 = cb * jnp.exp(jnp.where(causal, a2 - cs_t[h2:h2 + 1, :], NEG_INF))
            xs = xbc_ref[:, cols]
            xc = xs * jnp.where(lo_half, bc(dt, h1), bc(dt, h2))
            lhs = jnp.concatenate([m1, m2], axis=1).astype(BF16)
            rhs = jnp.concatenate([jnp.where(lo_half, xc, 0.0), jnp.where(lo_half, 0.0, xc)], axis=0).astype(BF16)
            y_diag = _dot(lhs, rhs)
            csp = jnp.where(lo_half, a1, a2)
            y_off = jnp.exp(csp) * yoff_raw[:, 2 * pr * SSD_HEAD_DIM:(2 * pr + 2) * SSD_HEAD_DIM]
            xdec_ref[:, cols] = (xc * jnp.exp(csp[L - 1:L, :] - csp)).astype(BF16)
            end_decay += [jnp.exp(a1[L - 1:L, :]), jnp.exp(a2[L - 1:L, :])]
            y = y_diag + y_off + dskip_ref[:, cols] * xs
            gated.append(y * _silu(z_ref[:, cols]))
        ss = sum(jnp.sum(v * v, axis=-1, keepdims=True) for v in gated)
        scale = lax.rsqrt(ss * (1.0 / SSD_GW) + RMS_EPS)
        for pr in range(SSD_HPG // 2):
            cols = slice(g * SSD_GW + pr * LANES, g * SSD_GW + (pr + 1) * LANES)
            y_ref[:, cols] = (gated[pr] * scale * ng_ref[:, cols]).astype(y_ref.dtype)
        st = _dot_tn(xdec_ref[:, hrows], b_g)
        for k in range(SSD_HPG):
            hs = slice(g * SSD_GW + k * SSD_HEAD_DIM, g * SSD_GW + (k + 1) * SSD_HEAD_DIM)
            h_ref[hs, :] = h_ref[hs, :] * end_decay[k] + st[k * SSD_HEAD_DIM:(k + 1) * SSD_HEAD_DIM, :]

    assert not fillers

    @pl.when(c == n_chunks - 1)
    def _():
        state_ref[0] = h_ref[...]
        tail_ref[...] = halo_ref[...]


def _ssd_prompt(x, g, wzx, wdt, cw, cb, dtb, alog, dskip, ng, tri, *, bsz, t):
    nc = t // SSD_CHUNK
    L = SSD_CHUNK
    d = x.shape[1]
    last = bsz * nc - 1
    cur = lambda b, c: (b * nc + c, 0)
    nxt = lambda b, c: (jnp.minimum(b * nc + c + 1, last), 0)
    const = lambda b, c: (0, 0)
    resident = pl.Buffered(1)
    return pl.pallas_call(
        functools.partial(_ssd_prompt_kernel, n_chunks=nc),
        grid=(bsz, nc),
        in_specs=[pl.BlockSpec((L, d), const),
                  pl.BlockSpec((L, d), nxt),
                  pl.BlockSpec((1, d), const),
                  pl.BlockSpec((d, SSD_ZX), const, pipeline_mode=resident),
                  pl.BlockSpec((d, LANES), const, pipeline_mode=resident),
                  pl.BlockSpec((SSD_CONV, SSD_CONV_CH), const),
                  pl.BlockSpec((1, SSD_CONV_CH), const),
                  pl.BlockSpec((1, LANES), const),
                  pl.BlockSpec((1, LANES), const),
                  pl.BlockSpec((1, SSD_INNER), const),
                  pl.BlockSpec((1, SSD_INNER), const),
                  pl.BlockSpec((L, L), const)],
        out_specs=[pl.BlockSpec((L, SSD_INNER), cur),
                   pl.BlockSpec((1, SSD_INNER, SSD_STATE), lambda b, c: (b, 0, 0)),
                   pl.BlockSpec((SSD_HALO, SSD_CONV_CH), lambda b, c: (b, 0))],
        out_shape=[jax.ShapeDtypeStruct((bsz * t, SSD_INNER), BF16),
                   jax.ShapeDtypeStruct((bsz, SSD_INNER, SSD_STATE), F32),
                   jax.ShapeDtypeStruct((bsz * SSD_HALO, SSD_CONV_CH), F32)],
        scratch_shapes=[pltpu.VMEM((L, SSD_ZX), F32),
                        pltpu.VMEM((L, LANES), F32),
                        pltpu.VMEM((L, SSD_ZX), F32),
                        pltpu.VMEM((L, LANES), F32),
                        pltpu.VMEM((L, d), BF16),
                        pltpu.VMEM((SSD_HALO, SSD_CONV_CH), F32),
                        pltpu.VMEM((L, SSD_CONV_CH), F32),
                        pltpu.VMEM((SSD_INNER, SSD_STATE), F32),
                        pltpu.VMEM((L, SSD_INNER), BF16)],
        compiler_params=_cparams(2),
        name="ssd_prompt",
    )(x, x, g, wzx, wdt, cw, cb, dtb, alog, dskip, ng, tri)


def _conf_sample_kernel(u_ref, hist_ref, dw_ref, db_ref, lng_ref, lnb_ref, c_ref, a_ref):
    a = _glu(u_ref[:, 0:GLU_W])
    a_ref[...] = a
    nh = CONF_KERNEL - 1
    acc = db_ref[...] + jnp.sum(hist_ref[...] * dw_ref[0:nh, :], axis=1) + a * dw_ref[nh:nh + 1, :]
    mu = jnp.mean(acc, axis=-1, keepdims=True)
    xc = acc - mu
    var = jnp.mean(xc * xc, axis=-1, keepdims=True)
    c_ref[...] = _silu(xc * lax.rsqrt(var + LN_EPS) * lng_ref[...] + lnb_ref[...])


def _conf_sample(u, hist, dw, db, lng, lnb, *, bb):
    n = u.shape[0]
    nh = CONF_KERNEL - 1
    const = lambda i: (0, 0)
    return pl.pallas_call(
        _conf_sample_kernel,
        grid=(n // bb,),
        in_specs=[pl.BlockSpec((bb, AB_IN), lambda i: (i, 0)),
                  pl.BlockSpec((bb, nh, CONF_CH), lambda i: (i, 0, 0)),
                  pl.BlockSpec(dw.shape, const),
                  pl.BlockSpec((1, CONF_CH), const),
                  pl.BlockSpec((1, CONF_CH), const),
                  pl.BlockSpec((1, CONF_CH), const)],
        out_specs=[pl.BlockSpec((bb, CONF_CH), lambda i: (i, 0)),
                   pl.BlockSpec((bb, CONF_CH), lambda i: (i, 0))],
        out_shape=[jax.ShapeDtypeStruct((n, CONF_CH), F32),
                   jax.ShapeDtypeStruct((n, CONF_CH), F32)],
        compiler_params=_cparams(1),
        name="conf_sample",
    )(u, hist, dw, db, lng, lnb)


def _attn_sample_kernel(qm_ref, kc_ref, vc_ref, kn_ref, vn_ref, sink_ref, o_ref, *, wc):
    q = qm_ref[...] * ATT_SCALE
    s = jnp.einsum("bhc,bsc->bhs", q.astype(BF16), kc_ref[...].astype(BF16), preferred_element_type=F32)
    kpos = lax.broadcasted_iota(jnp.int32, s.shape, 2)
    s = jnp.where((kpos <= wc) & (wc - kpos < WINDOW), s, NEG_INF)
    s_new = jnp.sum(q * kn_ref[...], axis=-1, keepdims=True)
    sink = sink_ref[...][None]
    m = jnp.maximum(jnp.maximum(jnp.max(s, axis=-1, keepdims=True), s_new), sink)
    p = jnp.exp(s - m)
    p_new = jnp.exp(s_new - m)
    den = jnp.sum(p, axis=-1, keepdims=True) + p_new + jnp.exp(sink - m)
    o = jnp.einsum("bhs,bsc->bhc", p.astype(BF16), vc_ref[...].astype(BF16), preferred_element_type=F32)
    o_ref[...] = (o + p_new * vn_ref[...]) / den


def _attn_sample(qm, kc, vc, kn, vn, sink, *, bb):
    n, wc, _ = kc.shape
    blk3 = lambda i: (i, 0, 0)
    return pl.pallas_call(
        functools.partial(_attn_sample_kernel, wc=wc),
        grid=(n // bb,),
        in_specs=[pl.BlockSpec((bb, N_Q_HEADS, KV_W), blk3),
                  pl.BlockSpec((bb, wc, KV_W), blk3),
                  pl.BlockSpec((bb, wc, KV_W), blk3),
                  pl.BlockSpec((bb, 1, KV_W), blk3),
                  pl.BlockSpec((bb, 1, KV_W), blk3),
                  pl.BlockSpec((N_Q_HEADS, 1), lambda i: (0, 0))],
        out_specs=pl.BlockSpec((bb, N_Q_HEADS, KV_W), blk3),
        out_shape=jax.ShapeDtypeStruct((n, N_Q_HEADS, KV_W), F32),
        compiler_params=_cparams(1),
        name="attn_sample",
    )(qm, kc, vc, kn, vn, sink)


def _ssd_sample_pre_kernel(uzx_ref, udt_ref, hist_ref, cw_ref, cb_ref, dtb_ref, alog_ref, expand_ref,
                           xbc_ref, xdt_ref, da_ref):
    nh = SSD_CONV - 1
    xbc = uzx_ref[:, SSD_INNER:SSD_ZX]
    acc = cb_ref[...] + jnp.sum(hist_ref[...] * cw_ref[0:nh, :], axis=1) + xbc * cw_ref[nh:nh + 1, :]
    conv = _silu(acc)
    xbc_ref[...] = conv
    dt = _softplus(udt_ref[...] + dtb_ref[...])
    da_ref[...] = jnp.exp(dt * (-jnp.exp(alog_ref[...])))
    xdt_ref[...] = conv[:, 0:SSD_INNER] * _dot_sel_right(dt, expand_ref[...])


def _ssd_sample_pre(uzx, udt, hist, cw, cb, dtb, alog, expand, *, bb):
    n = uzx.shape[0]
    const = lambda i: (0, 0)
    row = lambda i: (i, 0)
    return pl.pallas_call(
        _ssd_sample_pre_kernel,
        grid=(n // bb,),
        in_specs=[pl.BlockSpec((bb, SSD_ZX), row),
                  pl.BlockSpec((bb, LANES), row),
                  pl.BlockSpec((bb, SSD_CONV - 1, SSD_CONV_CH), lambda i: (i, 0, 0)),
                  pl.BlockSpec((SSD_CONV, SSD_CONV_CH), const),
                  pl.BlockSpec((1, SSD_CONV_CH), const),
                  pl.BlockSpec((1, LANES), const),
                  pl.BlockSpec((1, LANES), const),
                  pl.BlockSpec((LANES, SSD_INNER), const)],
        out_specs=[pl.BlockSpec((bb, SSD_CONV_CH), row),
                   pl.BlockSpec((bb, SSD_INNER), row),
                   pl.BlockSpec((bb, LANES), row)],
        out_shape=[jax.ShapeDtypeStruct((n, SSD_CONV_CH), F32),
                   jax.ShapeDtypeStruct((n, SSD_INNER), F32),
                   jax.ShapeDtypeStruct((n, LANES), F32)],
        compiler_params=_cparams(1),
        name="ssd_sample_pre",
    )(uzx, udt, hist, cw, cb, dtb, alog, expand)


def _ssd_sample_state_kernel(da_ref, st_ref, xdt_ref, xbc_ref, *rest, bb, has_prev):
    so_ref, yt_ref, xt_ref, xr_ref, xb_ref, yr_ref = rest[1:] if has_prev else rest
    i = pl.program_id(0)

    @pl.when(i == 0)
    def _():
        xt_ref[...] = xdt_ref[...].T
        yt_ref[...] = jnp.zeros_like(yt_ref)
        yr_ref[...] = jnp.zeros_like(yr_ref)

    xr_ref[...] = pltpu.roll(xt_ref[...], (LANES - i * bb) % LANES, 1)
    lane_g = lax.broadcasted_iota(jnp.int32, (SSD_GW, LANES), 1)
    for j in range(bb):
        b = i * bb + j
        xb_ref[...] = jnp.broadcast_to(xr_ref[:, j:j + 1], xb_ref.shape)
        for g in range(SSD_GROUPS):
            b_row = xbc_ref[j:j + 1, SSD_INNER + g * SSD_STATE:SSD_INNER + (g + 1) * SSD_STATE]
            c_row = xbc_ref[j:j + 1, SSD_INNER + SSD_BC + g * SSD_STATE:SSD_INNER + SSD_BC + (g + 1) * SSD_STATE]
            c_rows = jnp.broadcast_to(c_row, (LANES, SSD_STATE)).astype(BF16)
            parts = []
            for k in range(SSD_HPG):
                h = g * SSD_HPG + k
                hs = slice(h * SSD_HEAD_DIM, (h + 1) * SSD_HEAD_DIM)
                hn = st_ref[0, j, hs, :] * da_ref[b, h] + xb_ref[hs, :] * b_row
                so_ref[0, j, hs, :] = hn
                parts.append(hn.astype(BF16))
            y = _dot_nt(jnp.concatenate(parts, axis=0), c_rows)
            rows = slice(g * SSD_GW, (g + 1) * SSD_GW)
            yr_ref[rows, :] = jnp.where(lane_g == j, y, yr_ref[rows, :])
    lane = lax.broadcasted_iota(jnp.int32, yt_ref.shape, 1)
    mine = (lane >= i * bb) & (lane < (i + 1) * bb)
    yt_ref[...] = jnp.where(mine, pltpu.roll(yr_ref[...], i * bb, 1), yt_ref[...])


def _ssd_sample_state(da, state_all, xdt, xbc, prev_out, *, layer_idx, bb):
    n = state_all.shape[1]
    has_prev = prev_out is not None
    slab = lambda i, s: (layer_idx, i, 0, 0)
    in_specs = [pl.BlockSpec((1, bb, SSD_INNER, SSD_STATE), slab),
                pl.BlockSpec((n, SSD_INNER), lambda i, s: (0, 0)),
                pl.BlockSpec((bb, SSD_CONV_CH), lambda i, s: (i, 0))]
    operands = [da, state_all, xdt, xbc]
    if has_prev:
        in_specs.append(pl.BlockSpec(memory_space=pl.ANY))
        operands.append(prev_out)
    grid_spec = pltpu.PrefetchScalarGridSpec(
        num_scalar_prefetch=1,
        grid=(n // bb,),
        in_specs=in_specs,
        out_specs=[pl.BlockSpec((1, bb, SSD_INNER, SSD_STATE), slab),
                   pl.BlockSpec((SSD_INNER, n), lambda i, s: (0, 0))],
        scratch_shapes=[pltpu.VMEM((SSD_INNER, n), F32),
                        pltpu.VMEM((SSD_INNER, n), F32),
                        pltpu.VMEM((SSD_INNER, SSD_STATE), F32),
                        pltpu.VMEM((SSD_INNER, n), F32)],
    )
    return pl.pallas_call(
        functools.partial(_ssd_sample_state_kernel, bb=bb, has_prev=has_prev),
        grid_spec=grid_spec,
        out_shape=[jax.ShapeDtypeStruct(state_all.shape, F32),
                   jax.ShapeDtypeStruct((SSD_INNER, n), F32)],
        input_output_aliases={len(operands) - 1: 0} if has_prev else {},
        compiler_params=_cparams(1),
        name="ssd_sample_state",
    )(*operands)


def _ssd_sample_post_kernel(yt_ref, xbc_ref, uzx_ref, dskip_ref, ng_ref, o_ref):
    y = yt_ref[...].T + dskip_ref[...] * xbc_ref[:, 0:SSD_INNER]
    y = y * _silu(uzx_ref[:, 0:SSD_INNER])
    for g in range(SSD_GROUPS):
        cols = slice(g * SSD_GW, (g + 1) * SSD_GW)
        o_ref[:, cols] = _rms(y[:, cols], ng_ref[:, cols])


def _ssd_sample_post(yt, xbc, uzx, dskip, ng):
    n = xbc.shape[0]
    const = lambda i: (0, 0)
    return pl.pallas_call(
        _ssd_sample_post_kernel,
        grid=(1,),
        in_specs=[pl.BlockSpec((SSD_INNER, n), const),
                  pl.BlockSpec((n, SSD_CONV_CH), const),
                  pl.BlockSpec((n, SSD_ZX), const),
                  pl.BlockSpec((1, SSD_INNER), const),
                  pl.BlockSpec((1, SSD_INNER), const)],
        out_specs=pl.BlockSpec((n, SSD_INNER), const),
        out_shape=jax.ShapeDtypeStruct((n, SSD_INNER), F32),
        compiler_params=_cparams(1),
        name="ssd_sample_post",
    )(yt, xbc, uzx, dskip, ng)


def _pad_lanes(v):
    return jnp.pad(v.astype(F32), (0, LANES - v.shape[0]))[None, :]


def _head_expand(v):
    return jnp.repeat(v.astype(F32), SSD_HEAD_DIM)[None, :]


def _prep_layer(P, layer):
    i = layer // 2
    g = P["norm_g"][layer]
    w = {"g": [g[k][None, :] for k in range(4)],
         "w_up": P["mlp_w_up"][layer].astype(BF16),
         "w_down": P["mlp_w_down"][layer].astype(BF16)}
    if layer % 2 == 0:
        w.update(w_in=P["ab_w_in"][i].astype(BF16), w_out=P["ab_w_out"][i].astype(BF16),
                 dw=P["conf_dw_w"][i], db=P["conf_dw_b"][i][None, :],
                 lng=P["conf_ln_g"][i][None, :], lnb=P["conf_ln_b"][i][None, :],
                 sinks=P["attn_sinks"][i])
    else:
        w_in = P["ssd_w_in"][i]
        w_dt = jnp.pad(w_in[:, SSD_ZX:], ((0, 0), (0, LANES - SSD_HEADS)))
        w.update(w_zx=w_in[:, :SSD_ZX].astype(BF16), w_dt=w_dt.astype(BF16),
                 w_out=P["ssd_w_out"][i].astype(BF16),
                 cw=P["ssd_conv_w"][i], cb=P["ssd_conv_b"][i][None, :],
                 dtb=_pad_lanes(P["ssd_dt_bias"][i]), alog=_pad_lanes(P["ssd_a_log"][i]),
                 dskip=_head_expand(P["ssd_d"][i]), ng=P["ssd_norm_g"][i][None, :])
    return w


def _trunk_prompt(x3, W):
    bsz, t, d = x3.shape
    x = x3.reshape(bsz * t, d)
    tri = jnp.tril(jnp.ones((SSD_CHUNK, SSD_CHUNK), F32)).astype(BF16)
    win_k, win_v, conf_rows, ssm_states, ssd_rows = [], [], [], [], []
    tq = 512
    for layer in range(4):
        w = W[layer]
        g = w["g"]
        if layer % 2 == 0:
            x, tails, kv = _ab_prompt(x, g[0], w["w_in"], w["sinks"], w["dw"], w["db"], w["lng"], w["lnb"],
                                      w["w_out"], g[1], bsz=bsz, t=t, tq=tq)
            assert t >= WINDOW == ATT_BLOCK
            kv = kv.reshape(bsz, t // tq, ATT_BLOCK, 2 * KV_W)[:, -1]
            win_k.append(kv[:, :, :KV_W].reshape(bsz, WINDOW, N_KV_HEADS, HEAD_DIM))
            win_v.append(kv[:, :, KV_W:].reshape(bsz, WINDOW, N_KV_HEADS, HEAD_DIM))
            tails = tails.reshape(bsz, t // tq, CONV_HALO, CONF_CH)
            conf_rows.append(tails[:, -1, CONV_HALO - (CONF_KERNEL - 1):, :])
            x = _mlp(x, g[2], w["w_up"], w["w_down"], g[3], tm=512)
        else:
            y, state, tail = _ssd_prompt(x, g[0], w["w_zx"], w["w_dt"], w["cw"], w["cb"], w["dtb"], w["alog"],
                                         w["dskip"], w["ng"], tri, bsz=bsz, t=t)
            x = _proj_mlp(y, w["w_out"], x, g[1], g[2], w["w_up"], w["w_down"], g[3], tm=512)
            ssm_states.append(state.reshape(bsz, SSD_HEADS, SSD_HEAD_DIM, SSD_STATE))
            ssd_rows.append(tail.reshape(bsz, SSD_HALO, SSD_CONV_CH)[:, SSD_HALO - (SSD_CONV - 1):, :])
    return (x.reshape(bsz, t, d), jnp.stack(win_k), jnp.stack(win_v), jnp.stack(conf_rows),
            jnp.stack(ssm_states), jnp.stack(ssd_rows))


def _trunk_sample(x3, W, cache_k, cache_v, conf_hist, ssm, ssd_hist):
    n, t, d = x3.shape
    x = x3.reshape(n, d)
    expand = jnp.repeat(jnp.eye(LANES, SSD_HEADS, dtype=F32), SSD_HEAD_DIM, axis=1).astype(BF16)
    win_k, win_v, conf_rows, ssd_rows = [], [], [], []
    ssm_out = None
    head_group = jnp.arange(N_Q_HEADS) // Q_PER_KV
    for layer in range(4):
        w = W[layer]
        g = w["g"]
        i = layer // 2
        if layer % 2 == 0:
            u = _norm_matmul(x, g[0], w["w_in"], tm=n, tn=AB_IN // 2)
            c_out, a = _conf_sample(u, conf_hist[i], w["dw"], w["db"], w["lng"], w["lnb"], bb=32)
            k_off = GLU_W + Q_W
            q = u[:, GLU_W:k_off].reshape(n, N_Q_HEADS, HEAD_DIM)
            k_new = u[:, k_off:k_off + KV_W]
            v_new = u[:, k_off + KV_W:]
            lane_group = jnp.arange(KV_W) // HEAD_DIM
            qm = jnp.where(head_group[None, :, None] == lane_group[None, None, :],
                           jnp.tile(q, (1, 1, N_KV_HEADS)), 0.0)
            wc = cache_k.shape[2]
            o = _attn_sample(qm, cache_k[i].reshape(n, wc, KV_W), cache_v[i].reshape(n, wc, KV_W),
                             k_new[:, None, :], v_new[:, None, :], w["sinks"][:, None], bb=32)
            o = o.reshape(n, N_Q_HEADS, N_KV_HEADS, HEAD_DIM)
            a_out = jnp.take_along_axis(o, head_group[None, :, None, None], axis=2).reshape(n, Q_W)
            x = _proj_norm_res(jnp.concatenate([c_out, a_out], axis=-1), w["w_out"], x, g[1], tm=n)
            win_k.append(k_new.reshape(n, 1, N_KV_HEADS, HEAD_DIM))
            win_v.append(v_new.reshape(n, 1, N_KV_HEADS, HEAD_DIM))
            conf_rows.append(a[:, None, :])
        else:
            uzx = _norm_matmul(x, g[0], w["w_zx"], tm=n, tn=1024)
            udt = _norm_matmul(x, g[0], w["w_dt"], tm=n, tn=LANES)
            xbc, xdt, da = _ssd_sample_pre(uzx, udt, ssd_hist[i], w["cw"], w["cb"], w["dtb"], w["alog"], expand,
                                           bb=32)
            ssm_out, yt = _ssd_sample_state(da, ssm.reshape(ssm.shape[0], n, SSD_INNER, SSD_STATE), xdt, xbc,
                                            ssm_out, layer_idx=i, bb=8)
            y = _ssd_sample_post(yt, xbc, uzx, w["dskip"], w["ng"])
            x = _proj_norm_res(y, w["w_out"], x, g[1], tm=n)
            ssd_rows.append(uzx[:, None, SSD_INNER:])
        x = _mlp(x, g[2], w["w_up"], w["w_down"], g[3], tm=n)
    return (x.reshape(n, t, d), jnp.stack(win_k), jnp.stack(win_v), jnp.stack(conf_rows),
            ssm_out.reshape(ssm.shape), jnp.stack(ssd_rows))


def kernel(x_prompt, x_sample, cache_win_k, cache_win_v, state_conf_conv, state_ssm, state_ssd_conv,
           norm_g, ab_w_in, conf_dw_w, conf_dw_b, conf_ln_g, conf_ln_b, attn_sinks, ab_w_out,
           ssd_w_in, ssd_conv_w, ssd_conv_b, ssd_dt_bias, ssd_a_log, ssd_d, ssd_norm_g, ssd_w_out,
           mlp_w_up, mlp_w_down):
    P = {"norm_g": norm_g, "ab_w_in": ab_w_in, "conf_dw_w": conf_dw_w, "conf_dw_b": conf_dw_b,
         "conf_ln_g": conf_ln_g, "conf_ln_b": conf_ln_b, "attn_sinks": attn_sinks, "ab_w_out": ab_w_out,
         "ssd_w_in": ssd_w_in, "ssd_conv_w": ssd_conv_w, "ssd_conv_b": ssd_conv_b, "ssd_dt_bias": ssd_dt_bias,
         "ssd_a_log": ssd_a_log, "ssd_d": ssd_d, "ssd_norm_g": ssd_norm_g, "ssd_w_out": ssd_w_out,
         "mlp_w_up": mlp_w_up, "mlp_w_down": mlp_w_down}
    W = [_prep_layer(P, layer) for layer in range(4)]
    y_p, wk_p, wv_p, cc_p, ssm_p, sc_p = _trunk_prompt(x_prompt, W)
    y_s, wk_s, wv_s, cc_s, ssm_s, sc_s = _trunk_sample(x_sample, W, cache_win_k, cache_win_v,
                                                       state_conf_conv, state_ssm, state_ssd_conv)
    return (y_p, y_s, wk_p, wv_p, wk_s, wv_s, cc_p, cc_s, ssm_p, ssm_s, sc_p, sc_s)
```

```python
import functools

import jax
import jax.numpy as jnp
from jax import lax
from jax.experimental import pallas as pl
from jax.experimental.pallas import tpu as pltpu

F32 = jnp.float32
BF16 = jnp.bfloat16

D_MODEL = 1024
CONF_CH = 512
CONF_KERNEL = 31
HEAD_DIM = 64
N_Q_HEADS = 8
N_KV_HEADS = 2
Q_PER_KV = N_Q_HEADS // N_KV_HEADS
WINDOW = 128
ATT_BLOCK = 128
ATT_SCALE = HEAD_DIM ** -0.5
Q_W = N_Q_HEADS * HEAD_DIM
KV_W = N_KV_HEADS * HEAD_DIM
GLU_W = 2 * CONF_CH
AB_IN = GLU_W + Q_W + 2 * KV_W
SSD_INNER = 2048
SSD_HEAD_DIM = 64
SSD_HEADS = 32
SSD_GROUPS = 4
SSD_HPG = SSD_HEADS // SSD_GROUPS
SSD_STATE = 128
SSD_CONV = 4
SSD_BC = SSD_GROUPS * SSD_STATE
SSD_CONV_CH = SSD_INNER + 2 * SSD_BC
SSD_ZX = SSD_INNER + SSD_CONV_CH
SSD_GW = SSD_INNER // SSD_GROUPS
SSD_CHUNK = 128
D_FF = 4096
RMS_EPS = 1e-6
LN_EPS = 1e-5

LANES = 128
SUBLANES = 8
VMEM_LIMIT = 56 * 1024 * 1024

NEG_INF = float("-inf")


def _cparams(n_axes):
    return pltpu.CompilerParams(dimension_semantics=("arbitrary",) * n_axes,
                                vmem_limit_bytes=VMEM_LIMIT)


def _rms(x, g):
    return x * lax.rsqrt(jnp.mean(x * x, axis=-1, keepdims=True) + RMS_EPS) * g


def _sigmoid(x):
    return 1.0 / (1.0 + jnp.exp(-x))


def _silu(x):
    return x * _sigmoid(x)


def _log1p(e):
    u = 1.0 + e
    tiny = u == 1.0
    return jnp.where(tiny, e, jnp.log(u) * (e / jnp.where(tiny, 1.0, u - 1.0)))


def _softplus(x):
    return jnp.maximum(x, 0.0) + _log1p(jnp.exp(-jnp.abs(x)))


def _dot(a, b):
    return jnp.dot(a, b, preferred_element_type=F32)


def _dot_nt(a, b):
    return lax.dot_general(a, b, (((1,), (1,)), ((), ())), preferred_element_type=F32)


def _dot_tn(a, b):
    return lax.dot_general(a, b, (((0,), (0,)), ((), ())), preferred_element_type=F32)


def _split3(x):
    hi = x.astype(BF16)
    r1 = x - hi.astype(F32)
    mid = r1.astype(BF16)
    lo = (r1 - mid.astype(F32)).astype(BF16)
    return hi, mid, lo


def _dot_sel_left(sel, x):
    hi, mid, lo = _split3(x)
    return _dot(sel, hi) + _dot(sel, mid) + _dot(sel, lo)


def _dot_sel_right(x, sel):
    hi, mid, lo = _split3(x)
    return _dot(hi, sel) + _dot(mid, sel) + _dot(lo, sel)


def _norm_matmul_kernel(x_ref, g_ref, w_ref, o_ref, hn_ref):
    @pl.when(pl.program_id(1) == 0)
    def _():
        hn_ref[...] = _rms(x_ref[...], g_ref[...]).astype(BF16)

    o_ref[...] = _dot(hn_ref[...], w_ref[...])


def _norm_matmul(x, g, w, *, tm, tn):
    m, d = x.shape
    n = w.shape[1]
    return pl.pallas_call(
        _norm_matmul_kernel,
        grid=(m // tm, n // tn),
        in_specs=[pl.BlockSpec((tm, d), lambda i, j: (i, 0)),
                  pl.BlockSpec((1, d), lambda i, j: (0, 0)),
                  pl.BlockSpec((d, tn), lambda i, j: (0, j))],
        out_specs=pl.BlockSpec((tm, tn), lambda i, j: (i, j)),
        out_shape=jax.ShapeDtypeStruct((m, n), F32),
        scratch_shapes=[pltpu.VMEM((tm, d), BF16)],
        compiler_params=_cparams(2),
        name="norm_matmul",
    )(x, g, w)


def _proj_norm_res_kernel(a_ref, w_ref, x_ref, g_ref, o_ref):
    y = _dot(a_ref[...].astype(BF16), w_ref[...])
    o_ref[...] = x_ref[...] + _rms(y, g_ref[...])


def _proj_norm_res(a, w, x, g, *, tm):
    m, k = a.shape
    d = w.shape[1]
    return pl.pallas_call(
        _proj_norm_res_kernel,
        grid=(m // tm,),
        in_specs=[pl.BlockSpec((tm, k), lambda i: (i, 0)),
                  pl.BlockSpec((k, d), lambda i: (0, 0)),
                  pl.BlockSpec((tm, d), lambda i: (i, 0)),
                  pl.BlockSpec((1, d), lambda i: (0, 0))],
        out_specs=pl.BlockSpec((tm, d), lambda i: (i, 0)),
        out_shape=jax.ShapeDtypeStruct((m, d), F32),
        compiler_params=_cparams(1),
        name="proj_norm_res",
    )(a, w, x, g)


def _mlp_block(x, gpre_ref, wup_ref, wdn_ref, gpost_ref, tf):
    h = _rms(x, gpre_ref[...]).astype(BF16)
    f = None
    for c in range(D_FF // tf):
        a = jnp.maximum(_dot(h, wup_ref[:, c * tf:(c + 1) * tf]), 0.0)
        part = _dot((a * a).astype(BF16), wdn_ref[c * tf:(c + 1) * tf, :])
        f = part if f is None else f + part
    return x + _rms(f, gpost_ref[...])


def _mlp_kernel(x_ref, gpre_ref, wup_ref, wdn_ref, gpost_ref, o_ref, *, tf):
    o_ref[...] = _mlp_block(x_ref[...], gpre_ref, wup_ref, wdn_ref, gpost_ref, tf)


def _proj_mlp_kernel(a_ref, wo_ref, x_ref, gmix_ref, gpre_ref, wup_ref, wdn_ref, gpost_ref, o_ref, *, tf):
    x_mid = x_ref[...] + _rms(_dot(a_ref[...].astype(BF16), wo_ref[...]), gmix_ref[...])
    o_ref[...] = _mlp_block(x_mid, gpre_ref, wup_ref, wdn_ref, gpost_ref, tf)


def _mlp(x, gpre, wup, wdn, gpost, *, tm, tf=1024):
    m, d = x.shape
    return pl.pallas_call(
        functools.partial(_mlp_kernel, tf=tf),
        grid=(m // tm,),
        in_specs=[pl.BlockSpec((tm, d), lambda i: (i, 0)),
                  pl.BlockSpec((1, d), lambda i: (0, 0)),
                  pl.BlockSpec((d, D_FF), lambda i: (0, 0)),
                  pl.BlockSpec((D_FF, d), lambda i: (0, 0)),
                  pl.BlockSpec((1, d), lambda i: (0, 0))],
        out_specs=pl.BlockSpec((tm, d), lambda i: (i, 0)),
        out_shape=jax.ShapeDtypeStruct((m, d), F32),
        compiler_params=_cparams(1),
        name="mlp",
    )(x, gpre, wup, wdn, gpost)


def _proj_mlp(a, wo, x, gmix, gpre, wup, wdn, gpost, *, tm, tf=1024):
    m, d = x.shape
    k = a.shape[1]
    row = lambda i: (i, 0)
    const = lambda i: (0, 0)
    resident = pl.Buffered(1)
    return pl.pallas_call(
        functools.partial(_proj_mlp_kernel, tf=tf),
        grid=(m // tm,),
        in_specs=[pl.BlockSpec((tm, k), row),
                  pl.BlockSpec((k, d), const, pipeline_mode=resident),
                  pl.BlockSpec((tm, d), row),
                  pl.BlockSpec((1, d), const),
                  pl.BlockSpec((1, d), const),
                  pl.BlockSpec((d, D_FF), const, pipeline_mode=resident),
                  pl.BlockSpec((D_FF, d), const, pipeline_mode=resident),
                  pl.BlockSpec((1, d), const)],
        out_specs=pl.BlockSpec((tm, d), row),
        out_shape=jax.ShapeDtypeStruct((m, d), F32),
        compiler_params=_cparams(1),
        name="proj_mlp",
    )(a, wo, x, gmix, gpre, wup, wdn, gpost)


CONV_HALO = 32
CONV_ROWS = 64
AB_PROJ_COLS = 256


def _glu(u):
    return u[:, :CONF_CH] * _sigmoid(u[:, CONF_CH:GLU_W])


def _ab_prompt_kernel(sink_ref, x_ref, gin_ref, win_ref, dw_ref, db_ref, lng_ref, lnb_ref,
                      wout_ref, g_ref, o_ref, tail_ref, kvtail_ref,
                      u_ref, ukv_prev_ref, ap_ref, sh_ref, mix_ref, s_ref, p_ref, *, tq):
    i = pl.program_id(1)
    has_prev = i > 0
    n_ap = CONV_HALO + tq
    k_off = GLU_W + Q_W

    @pl.when(i == 0)
    def _():
        ap_ref[tq:n_ap, :] = jnp.zeros((CONV_HALO, CONF_CH), F32)
        u_ref[tq - ATT_BLOCK:tq, k_off:AB_IN] = jnp.zeros((ATT_BLOCK, 2 * KV_W), F32)

    ap_ref[0:CONV_HALO, :] = ap_ref[tq:n_ap, :]
    ukv_prev_ref[...] = u_ref[tq - ATT_BLOCK:tq, k_off:AB_IN]

    hn = _rms(x_ref[...], gin_ref[...]).astype(BF16)
    for cc in range(AB_IN // AB_PROJ_COLS):
        cols = slice(cc * AB_PROJ_COLS, (cc + 1) * AB_PROJ_COLS)
        u_ref[:, cols] = _dot(hn, win_ref[:, cols])
    kvtail_ref[...] = u_ref[tq - ATT_BLOCK:tq, k_off:AB_IN]

    ap_ref[CONV_HALO:n_ap, :] = _glu(u_ref[:, 0:GLU_W])
    ap_ref[n_ap:n_ap + SUBLANES, :] = jnp.zeros((SUBLANES, CONF_CH), F32)
    tail_ref[...] = ap_ref[tq:tq + CONV_HALO, :]
    for r in range(1, SUBLANES):
        sh_ref[r - 1] = ap_ref[r:r + n_ap, :]

    first_tap = CONV_HALO - (CONF_KERNEL - 1)
    for rc in range(tq // CONV_ROWS):
        acc = jnp.broadcast_to(db_ref[...], (CONV_ROWS, CONF_CH))
        for k in range(CONF_KERNEL):
            tile, r = divmod(first_tap + k, SUBLANES)
            start = rc * CONV_ROWS + tile * SUBLANES
            if r == 0:
                rows = ap_ref[start:start + CONV_ROWS, :]
            else:
                rows = sh_ref[r - 1, start:start + CONV_ROWS, :]
            acc = acc + rows * dw_ref[k:k + 1, :]
        mu = jnp.mean(acc, axis=-1, keepdims=True)
        xc = acc - mu
        var = jnp.mean(xc * xc, axis=-1, keepdims=True)
        ln = xc * lax.rsqrt(var + LN_EPS) * lng_ref[...] + lnb_ref[...]
        mix_ref[rc * CONV_ROWS:(rc + 1) * CONV_ROWS, 0:CONF_CH] = _silu(ln).astype(BF16)

    n_blk = tq // ATT_BLOCK
    n_keys = 2 * ATT_BLOCK
    grp_rows = Q_PER_KV * ATT_BLOCK
    blk_rows = N_Q_HEADS * ATT_BLOCK
    k_off = GLU_W + Q_W
    v_off = k_off + KV_W

    def kv_rows(j, off):
        rows = slice(j * ATT_BLOCK, (j + 1) * ATT_BLOCK)
        if j == 0:
            prev = ukv_prev_ref[:, off - k_off:off - k_off + KV_W]
        else:
            prev = u_ref[(j - 1) * ATT_BLOCK:j * ATT_BLOCK, off:off + KV_W]
        return jnp.concatenate([prev, u_ref[rows, off:off + KV_W]], axis=0)

    col0 = lax.broadcasted_iota(jnp.int32, (grp_rows, n_keys), 1) == 0
    for j in range(n_blk):
        rows = slice(j * ATT_BLOCK, (j + 1) * ATT_BLOCK)
        q = (u_ref[rows, GLU_W:GLU_W + Q_W] * ATT_SCALE).astype(BF16)
        kk = kv_rows(j, k_off).astype(BF16)
        for g in range(N_KV_HEADS):
            heads = [g * Q_PER_KV + hh for hh in range(Q_PER_KV)]
            qg = jnp.concatenate([q[:, h * HEAD_DIM:(h + 1) * HEAD_DIM] for h in heads], axis=0)
            s = _dot_nt(qg, kk[:, g * HEAD_DIM:(g + 1) * HEAD_DIM])
            sink = jnp.concatenate([jnp.full((ATT_BLOCK, 1), sink_ref[h], F32) for h in heads], axis=0)
            base = j * blk_rows + g * grp_rows
            s_ref[base:base + grp_rows, :] = jnp.where(col0, sink, s)

    qi = lax.broadcasted_iota(jnp.int32, (ATT_BLOCK, n_keys), 0)
    kc = lax.broadcasted_iota(jnp.int32, (ATT_BLOCK, n_keys), 1)
    visible = ((kc > qi) & (kc <= qi + WINDOW)) | (kc == 0)
    visible_first = visible & ((kc >= ATT_BLOCK) | (kc == 0) | has_prev)
    bias = jnp.where(visible, 0.0, NEG_INF)
    bias_first = jnp.where(visible_first, 0.0, NEG_INF)
    for lo, hi, bb in ((0, blk_rows, bias_first), (blk_rows, n_blk * blk_rows, bias)):
        sm = s_ref[lo:hi, :].reshape((hi - lo) // ATT_BLOCK, ATT_BLOCK, n_keys) + bb
        m = jnp.max(sm, axis=-1, keepdims=True)
        p_ref[lo:hi, :] = jnp.exp(sm - m).astype(BF16).reshape(hi - lo, n_keys)

    vrow = lax.broadcasted_iota(jnp.int32, (n_keys, HEAD_DIM), 0)
    ones = jnp.ones((n_keys, HEAD_DIM), BF16)
    lo_half = lax.broadcasted_iota(jnp.int32, (ATT_BLOCK, 2 * HEAD_DIM), 1) < HEAD_DIM
    for j in range(n_blk):
        rows = slice(j * ATT_BLOCK, (j + 1) * ATT_BLOCK)
        vv = kv_rows(j, v_off)
        for g in range(N_KV_HEADS):
            vg = jnp.where(vrow > 0, vv[:, g * HEAD_DIM:(g + 1) * HEAD_DIM], 0.0).astype(BF16)
            v_ext = (jnp.concatenate([vg, ones], axis=1), jnp.concatenate([ones, vg], axis=1))
            for pr in range(Q_PER_KV // 2):
                normed = []
                for par in range(2):
                    base = j * blk_rows + g * grp_rows + (2 * pr + par) * ATT_BLOCK
                    oe = _dot(p_ref[base:base + ATT_BLOCK, :], v_ext[par])
                    normed.append(oe / pltpu.roll(oe, HEAD_DIM, 1))
                c0 = CONF_CH + (g * Q_PER_KV + 2 * pr) * HEAD_DIM
                mix_ref[rows, c0:c0 + 2 * HEAD_DIM] = jnp.where(lo_half, normed[0], normed[1]).astype(BF16)

    y = _dot(mix_ref[...], wout_ref[...])
    o_ref[...] = x_ref[...] + _rms(y, g_ref[...])


def _ab_prompt(x, gin, win, sinks, dw, db, lng, lnb, wout, g, *, bsz, t, tq):
    nt = t // tq
    m = bsz * t
    d = x.shape[1]

    def cur(b, i, s):
        return (b * nt + i, 0)

    const = lambda b, i, s: (0, 0)
    resident = pl.Buffered(1)
    grid_spec = pltpu.PrefetchScalarGridSpec(
        num_scalar_prefetch=1,
        grid=(bsz, nt),
        in_specs=[pl.BlockSpec((tq, d), cur),
                  pl.BlockSpec((1, d), const),
                  pl.BlockSpec(win.shape, const, pipeline_mode=resident),
                  pl.BlockSpec(dw.shape, const),
                  pl.BlockSpec((1, CONF_CH), const),
                  pl.BlockSpec((1, CONF_CH), const),
                  pl.BlockSpec((1, CONF_CH), const),
                  pl.BlockSpec(wout.shape, const, pipeline_mode=resident),
                  pl.BlockSpec((1, d), const)],
        out_specs=[pl.BlockSpec((tq, d), cur),
                   pl.BlockSpec((CONV_HALO, CONF_CH), cur),
                   pl.BlockSpec((ATT_BLOCK, 2 * KV_W), cur)],
        scratch_shapes=[pltpu.VMEM((tq, AB_IN), F32),
                        pltpu.VMEM((ATT_BLOCK, 2 * KV_W), F32),
                        pltpu.VMEM((CONV_HALO + tq + SUBLANES, CONF_CH), F32),
                        pltpu.VMEM((SUBLANES - 1, CONV_HALO + tq, CONF_CH), F32),
                        pltpu.VMEM((tq, CONF_CH + Q_W), BF16),
                        pltpu.VMEM((tq * N_Q_HEADS, 2 * ATT_BLOCK), F32),
                        pltpu.VMEM((tq * N_Q_HEADS, 2 * ATT_BLOCK), BF16)],
    )
    return pl.pallas_call(
        functools.partial(_ab_prompt_kernel, tq=tq),
        grid_spec=grid_spec,
        out_shape=[jax.ShapeDtypeStruct((m, d), F32),
                   jax.ShapeDtypeStruct((bsz * nt * CONV_HALO, CONF_CH), F32),
                   jax.ShapeDtypeStruct((bsz * nt * ATT_BLOCK, 2 * KV_W), F32)],
        compiler_params=_cparams(2),
        name="ab_prompt",
    )(sinks, x, gin, win, dw, db, lng, lnb, wout, g)


SSD_HALO = SUBLANES
SSD_CONV_COLS = 256
SSD_PROJ_COLS = 512
SSD_STEP_CHUNKS = 2


def _ssd_conv_masks(ch):
    sub = lax.broadcasted_iota(jnp.int32, (SUBLANES, ch), 0)
    return [sub < back for back in range(SSD_CONV)]


def _ssd_conv_silu(halo, cur, w, b, wrapped):
    rows, ch = cur.shape
    tiles = [halo] + [cur[r:r + SUBLANES, :] for r in range(0, rows, SUBLANES)]
    acc = b + cur * w[SSD_CONV - 1:SSD_CONV, :]
    for back in range(1, SSD_CONV):
        rolled = [pltpu.roll(v, back, 0) for v in tiles]
        shifted = jnp.concatenate([jnp.where(wrapped[back], rolled[i], rolled[i + 1])
                                   for i in range(len(tiles) - 1)], axis=0)
        acc = acc + shifted * w[SSD_CONV - 1 - back:SSD_CONV - back, :]
    return _silu(acc)


def _ssd_prompt_kernel(x0_ref, xn_ref, g_ref, wzx_ref, wdt_ref, cw_ref, cb_ref, dtb_ref, alog_ref, dskip_ref,
                       ng_ref, tri_ref, y_ref, state_ref, tail_ref,
                       ua_ref, dta_ref, ub_ref, dtb2_ref, hn_ref, halo_ref, xbc_ref, h_ref, xdec_ref, *, n_steps):
    c = pl.program_id(1)
    n = pl.program_id(0) * n_steps + c

    def in_proj_pieces(x_ref, u_ref, udt_ref):
        def norm():
            hn_ref[...] = _rms(x_ref[...], g_ref[...]).astype(BF16)

        def zx(cc):
            cols = slice(cc * SSD_PROJ_COLS, (cc + 1) * SSD_PROJ_COLS)
            u_ref[:, cols] = _dot(hn_ref[...], wzx_ref[:, cols])

        def dt():
            udt_ref[...] = _dot(hn_ref[...], wdt_ref[...])

        return [norm] + [functools.partial(zx, cc) for cc in range(SSD_ZX // SSD_PROJ_COLS)] + [dt]

    @pl.when(n == 0)
    def _():
        for piece in in_proj_pieces(x0_ref, ua_ref, dta_ref):
            piece()

    @pl.when(c == 0)
    def _():
        halo_ref[...] = jnp.zeros_like(halo_ref)
        h_ref[...] = jnp.zeros_like(h_ref)

    refs = (cw_ref, cb_ref, dtb_ref, alog_ref, dskip_ref, ng_ref, tri_ref, y_ref,
            halo_ref, xbc_ref, h_ref, xdec_ref)

    def step(u_ref, udt_ref, fillers):
        n_stages = SSD_STEP_CHUNKS * (SSD_CONV_CH // SSD_CONV_COLS + SSD_HEADS // 2)
        n_fill = len(fillers)
        stage = [0]

        def fill():
            stage[0] += 1
            while fillers and n_fill - len(fillers) < (stage[0] * n_fill + n_stages - 1) // n_stages:
                fillers.pop(0)()

        for k in range(SSD_STEP_CHUNKS):
            _ssd_chunk(u_ref, udt_ref, k * SSD_CHUNK, *refs, fill=fill)
        assert not fillers

    @pl.when(n % 2 == 0)
    def _():
        step(ua_ref, dta_ref, in_proj_pieces(xn_ref, ub_ref, dtb2_ref))

    @pl.when(n % 2 == 1)
    def _():
        step(ub_ref, dtb2_ref, in_proj_pieces(xn_ref, ua_ref, dta_ref))

    @pl.when(c == n_steps - 1)
    def _():
        state_ref[0] = h_ref[...]
        tail_ref[...] = halo_ref[...]


def _ssd_chunk(u_ref, udt_ref, r0, cw_ref, cb_ref, dtb_ref, alog_ref, dskip_ref, ng_ref, tri_ref, y_ref,
               halo_ref, xbc_ref, h_ref, xdec_ref, *, fill):
    L = SSD_CHUNK
    crows = slice(r0, r0 + L)
    wrapped = _ssd_conv_masks(SSD_CONV_COLS)
    for cc in range(SSD_CONV_CH // SSD_CONV_COLS):
        fill()
        cols = slice(cc * SSD_CONV_COLS, (cc + 1) * SSD_CONV_COLS)
        ucols = slice(SSD_INNER + cc * SSD_CONV_COLS, SSD_INNER + (cc + 1) * SSD_CONV_COLS)
        xbc_ref[:, cols] = _ssd_conv_silu(halo_ref[:, cols], u_ref[crows, ucols], cw_ref[:, cols], cb_ref[:, cols],
                                          wrapped)
    halo_ref[...] = u_ref[r0 + L - SSD_HALO:r0 + L, SSD_INNER:SSD_ZX]

    dt = _softplus(udt_ref[crows, :] + dtb_ref[...])
    a = -jnp.exp(alog_ref[...])
    cs = _dot_sel_left(tri_ref[...], dt * a)
    cs_t = cs.T

    row = lax.broadcasted_iota(jnp.int32, (L, LANES), 0)
    lane = lax.broadcasted_iota(jnp.int32, (L, LANES), 1)
    causal = row >= lane
    lo_half = lane < SSD_HEAD_DIM

    def bc(v, h):
        return jnp.broadcast_to(v[:, h:h + 1], (L, LANES))

    for g in range(SSD_GROUPS):
        b_g = xbc_ref[:, SSD_INNER + g * SSD_STATE:SSD_INNER + (g + 1) * SSD_STATE].astype(BF16)
        c_g = xbc_ref[:, SSD_INNER + SSD_BC + g * SSD_STATE:SSD_INNER + SSD_BC + (g + 1) * SSD_STATE].astype(BF16)
        hrows = slice(g * SSD_GW, (g + 1) * SSD_GW)
        cb = _dot_nt(c_g, b_g)
        yoff_raw = _dot_nt(c_g, h_ref[hrows, :].astype(BF16))
        end_decay = []
        gated = []
        for pr in range(SSD_HPG // 2):
            fill()
            h1 = g * SSD_HPG + 2 * pr
            h2 = h1 + 1
            cols = slice(h1 * SSD_HEAD_DIM, (h1 + 2) * SSD_HEAD_DIM)
            a1 = bc(cs, h1)
            a2 = bc(cs, h2)
            m1 = cb * jnp.exp(jnp.where(causal, a1 - cs_t[h1:h1 + 1, :], NEG_INF))
            m2 = cb * jnp.exp(jnp.where(causal, a2 - cs_t[h2:h2 + 1, :], NEG_INF))
            xs = xbc_ref[:, cols]
            xc = xs * jnp.where(lo_half, bc(dt, h1), bc(dt, h2))
            lhs = jnp.concatenate([m1, m2], axis=1).astype(BF16)
            rhs = jnp.concatenate([jnp.where(lo_half, xc, 0.0), jnp.where(lo_half, 0.0, xc)], axis=0).astype(BF16)
            y_diag = _dot(lhs, rhs)
            csp = jnp.where(lo_half, a1, a2)
            y_off = jnp.exp(csp) * yoff_raw[:, 2 * pr * SSD_HEAD_DIM:(2 * pr + 2) * SSD_HEAD_DIM]
            xdec_ref[:, cols] = (xc * jnp.exp(csp[L - 1:L, :] - csp)).astype(BF16)
            end_decay += [jnp.exp(a1[L - 1:L, :]), jnp.exp(a2[L - 1:L, :])]
            y = y_diag + y_off + dskip_ref[:, cols] * xs
            gated.append(y * _silu(u_ref[crows, cols]))
        ss = sum(jnp.sum(v * v, axis=-1, keepdims=True) for v in gated)
        scale = lax.rsqrt(ss * (1.0 / SSD_GW) + RMS_EPS)
        for pr in range(SSD_HPG // 2):
            cols = slice(g * SSD_GW + pr * LANES, g * SSD_GW + (pr + 1) * LANES)
            y_ref[crows, cols] = (gated[pr] * scale * ng_ref[:, cols]).astype(y_ref.dtype)
        st = _dot_tn(xdec_ref[:, hrows], b_g)
        for k in range(SSD_HPG):
            hs = slice(g * SSD_GW + k * SSD_HEAD_DIM, g * SSD_GW + (k + 1) * SSD_HEAD_DIM)
            h_ref[hs, :] = h_ref[hs, :] * end_decay[k] + st[k * SSD_HEAD_DIM:(k + 1) * SSD_HEAD_DIM, :]


def _ssd_prompt(x, g, wzx, wdt, cw, cb, dtb, alog, dskip, ng, tri, *, bsz, t):
    rows = SSD_STEP_CHUNKS * SSD_CHUNK
    nc = t // rows
    L = SSD_CHUNK
    d = x.shape[1]
    last = bsz * nc - 1
    cur = lambda b, c: (b * nc + c, 0)
    nxt = lambda b, c: (jnp.minimum(b * nc + c + 1, last), 0)
    const = lambda b, c: (0, 0)
    resident = pl.Buffered(1)
    return pl.pallas_call(
        functools.partial(_ssd_prompt_kernel, n_steps=nc),
        grid=(bsz, nc),
        in_specs=[pl.BlockSpec((rows, d), const),
                  pl.BlockSpec((rows, d), nxt),
                  pl.BlockSpec((1, d), const),
                  pl.BlockSpec((d, SSD_ZX), const, pipeline_mode=resident),
                  pl.BlockSpec((d, LANES), const, pipeline_mode=resident),
                  pl.BlockSpec((SSD_CONV, SSD_CONV_CH), const),
                  pl.BlockSpec((1, SSD_CONV_CH), const),
                  pl.BlockSpec((1, LANES), const),
                  pl.BlockSpec((1, LANES), const),
                  pl.BlockSpec((1, SSD_INNER), const),
                  pl.BlockSpec((1, SSD_INNER), const),
                  pl.BlockSpec((L, L), const)],
        out_specs=[pl.BlockSpec((rows, SSD_INNER), cur),
                   pl.BlockSpec((1, SSD_INNER, SSD_STATE), lambda b, c: (b, 0, 0)),
                   pl.BlockSpec((SSD_HALO, SSD_CONV_CH), lambda b, c: (b, 0))],
        out_shape=[jax.ShapeDtypeStruct((bsz * t, SSD_INNER), BF16),
                   jax.ShapeDtypeStruct((bsz, SSD_INNER, SSD_STATE), F32),
                   jax.ShapeDtypeStruct((bsz * SSD_HALO, SSD_CONV_CH), F32)],
        scratch_shapes=[pltpu.VMEM((rows, SSD_ZX), F32),
                        pltpu.VMEM((rows, LANES), F32),
                        pltpu.VMEM((rows, SSD_ZX), F32),
                        pltpu.VMEM((rows, LANES), F32),
                        pltpu.VMEM((rows, d), BF16),
                        pltpu.VMEM((SSD_HALO, SSD_CONV_CH), F32),
                        pltpu.VMEM((L, SSD_CONV_CH), F32),
                        pltpu.VMEM((SSD_INNER, SSD_STATE), F32),
                        pltpu.VMEM((L, SSD_INNER), BF16)],
        compiler_params=_cparams(2),
        name="ssd_prompt",
    )(x, x, g, wzx, wdt, cw, cb, dtb, alog, dskip, ng, tri)


def _conf_sample_kernel(u_ref, hist_ref, dw_ref, db_ref, lng_ref, lnb_ref, c_ref, a_ref):
    a = _glu(u_ref[:, 0:GLU_W])
    a_ref[...] = a
    nh = CONF_KERNEL - 1
    acc = db_ref[...] + jnp.sum(hist_ref[...] * dw_ref[0:nh, :], axis=1) + a * dw_ref[nh:nh + 1, :]
    mu = jnp.mean(acc, axis=-1, keepdims=True)
    xc = acc - mu
    var = jnp.mean(xc * xc, axis=-1, keepdims=True)
    c_ref[...] = _silu(xc * lax.rsqrt(var + LN_EPS) * lng_ref[...] + lnb_ref[...])


def _conf_sample(u, hist, dw, db, lng, lnb, *, bb):
    n = u.shape[0]
    nh = CONF_KERNEL - 1
    const = lambda i: (0, 0)
    return pl.pallas_call(
        _conf_sample_kernel,
        grid=(n // bb,),
        in_specs=[pl.BlockSpec((bb, AB_IN), lambda i: (i, 0)),
                  pl.BlockSpec((bb, nh, CONF_CH), lambda i: (i, 0, 0)),
                  pl.BlockSpec(dw.shape, const),
                  pl.BlockSpec((1, CONF_CH), const),
                  pl.BlockSpec((1, CONF_CH), const),
                  pl.BlockSpec((1, CONF_CH), const)],
        out_specs=[pl.BlockSpec((bb, CONF_CH), lambda i: (i, 0)),
                   pl.BlockSpec((bb, CONF_CH), lambda i: (i, 0))],
        out_shape=[jax.ShapeDtypeStruct((n, CONF_CH), F32),
                   jax.ShapeDtypeStruct((n, CONF_CH), F32)],
        compiler_params=_cparams(1),
        name="conf_sample",
    )(u, hist, dw, db, lng, lnb)


def _attn_sample_kernel(qm_ref, kc_ref, vc_ref, kn_ref, vn_ref, sink_ref, o_ref, *, wc):
    q = qm_ref[...] * ATT_SCALE
    s = jnp.einsum("bhc,bsc->bhs", q.astype(BF16), kc_ref[...].astype(BF16), preferred_element_type=F32)
    kpos = lax.broadcasted_iota(jnp.int32, s.shape, 2)
    s = jnp.where((kpos <= wc) & (wc - kpos < WINDOW), s, NEG_INF)
    s_new = jnp.sum(q * kn_ref[...], axis=-1, keepdims=True)
    sink = sink_ref[...][None]
    m = jnp.maximum(jnp.maximum(jnp.max(s, axis=-1, keepdims=True), s_new), sink)
    p = jnp.exp(s - m)
    p_new = jnp.exp(s_new - m)
    den = jnp.sum(p, axis=-1, keepdims=True) + p_new + jnp.exp(sink - m)
    o = jnp.einsum("bhs,bsc->bhc", p.astype(BF16), vc_ref[...].astype(BF16), preferred_element_type=F32)
    o_ref[...] = (o + p_new * vn_ref[...]) / den


def _attn_sample(qm, kc, vc, kn, vn, sink, *, bb):
    n, wc, _ = kc.shape
    blk3 = lambda i: (i, 0, 0)
    return pl.pallas_call(
        functools.partial(_attn_sample_kernel, wc=wc),
        grid=(n // bb,),
        in_specs=[pl.BlockSpec((bb, N_Q_HEADS, KV_W), blk3),
                  pl.BlockSpec((bb, wc, KV_W), blk3),
                  pl.BlockSpec((bb, wc, KV_W), blk3),
                  pl.BlockSpec((bb, 1, KV_W), blk3),
                  pl.BlockSpec((bb, 1, KV_W), blk3),
                  pl.BlockSpec((N_Q_HEADS, 1), lambda i: (0, 0))],
        out_specs=pl.BlockSpec((bb, N_Q_HEADS, KV_W), blk3),
        out_shape=jax.ShapeDtypeStruct((n, N_Q_HEADS, KV_W), F32),
        compiler_params=_cparams(1),
        name="attn_sample",
    )(qm, kc, vc, kn, vn, sink)


def _ssd_sample_pre_kernel(uzx_ref, udt_ref, hist_ref, cw_ref, cb_ref, dtb_ref, alog_ref, expand_ref,
                           xbc_ref, xdt_ref, da_ref):
    nh = SSD_CONV - 1
    xbc = uzx_ref[:, SSD_INNER:SSD_ZX]
    acc = cb_ref[...] + jnp.sum(hist_ref[...] * cw_ref[0:nh, :], axis=1) + xbc * cw_ref[nh:nh + 1, :]
    conv = _silu(acc)
    xbc_ref[...] = conv
    dt = _softplus(udt_ref[...] + dtb_ref[...])
    da_ref[...] = jnp.exp(dt * (-jnp.exp(alog_ref[...])))
    xdt_ref[...] = conv[:, 0:SSD_INNER] * _dot_sel_right(dt, expand_ref[...])


def _ssd_sample_pre(uzx, udt, hist, cw, cb, dtb, alog, expand, *, bb):
    n = uzx.shape[0]
    const = lambda i: (0, 0)
    row = lambda i: (i, 0)
    return pl.pallas_call(
        _ssd_sample_pre_kernel,
        grid=(n // bb,),
        in_specs=[pl.BlockSpec((bb, SSD_ZX), row),
                  pl.BlockSpec((bb, LANES), row),
                  pl.BlockSpec((bb, SSD_CONV - 1, SSD_CONV_CH), lambda i: (i, 0, 0)),
                  pl.BlockSpec((SSD_CONV, SSD_CONV_CH), const),
                  pl.BlockSpec((1, SSD_CONV_CH), const),
                  pl.BlockSpec((1, LANES), const),
                  pl.BlockSpec((1, LANES), const),
                  pl.BlockSpec((LANES, SSD_INNER), const)],
        out_specs=[pl.BlockSpec((bb, SSD_CONV_CH), row),
                   pl.BlockSpec((bb, SSD_INNER), row),
                   pl.BlockSpec((bb, LANES), row)],
        out_shape=[jax.ShapeDtypeStruct((n, SSD_CONV_CH), F32),
                   jax.ShapeDtypeStruct((n, SSD_INNER), F32),
                   jax.ShapeDtypeStruct((n, LANES), F32)],
        compiler_params=_cparams(1),
        name="ssd_sample_pre",
    )(uzx, udt, hist, cw, cb, dtb, alog, expand)


def _ssd_sample_state_kernel(da_ref, st_ref, xdt_ref, xbc_ref, *rest, bb, has_prev):
    so_ref, yt_ref, xt_ref, xr_ref, xb_ref, yr_ref = rest[1:] if has_prev else rest
    i = pl.program_id(0)

    @pl.when(i == 0)
    def _():
        xt_ref[...] = xdt_ref[...].T
        yt_ref[...] = jnp.zeros_like(yt_ref)
        yr_ref[...] = jnp.zeros_like(yr_ref)

    xr_ref[...] = pltpu.roll(xt_ref[...], (LANES - i * bb) % LANES, 1)
    lane_g = lax.broadcasted_iota(jnp.int32, (SSD_GW, LANES), 1)
    for j in range(bb):
        b = i * bb + j
        xb_ref[...] = jnp.broadcast_to(xr_ref[:, j:j + 1], xb_ref.shape)
        for g in range(SSD_GROUPS):
            b_row = xbc_ref[j:j + 1, SSD_INNER + g * SSD_STATE:SSD_INNER + (g + 1) * SSD_STATE]
            c_row = xbc_ref[j:j + 1, SSD_INNER + SSD_BC + g * SSD_STATE:SSD_INNER + SSD_BC + (g + 1) * SSD_STATE]
            c_rows = jnp.broadcast_to(c_row, (LANES, SSD_STATE)).astype(BF16)
            parts = []
            for k in range(SSD_HPG):
                h = g * SSD_HPG + k
                hs = slice(h * SSD_HEAD_DIM, (h + 1) * SSD_HEAD_DIM)
                hn = st_ref[0, j, hs, :] * da_ref[b, h] + xb_ref[hs, :] * b_row
                so_ref[0, j, hs, :] = hn
                parts.append(hn.astype(BF16))
            y = _dot_nt(jnp.concatenate(parts, axis=0), c_rows)
            rows = slice(g * SSD_GW, (g + 1) * SSD_GW)
            yr_ref[rows, :] = jnp.where(lane_g == j, y, yr_ref[rows, :])
    lane = lax.broadcasted_iota(jnp.int32, yt_ref.shape, 1)
    mine = (lane >= i * bb) & (lane < (i + 1) * bb)
    yt_ref[...] = jnp.where(mine, pltpu.roll(yr_ref[...], i * bb, 1), yt_ref[...])


def _ssd_sample_state(da, state_all, xdt, xbc, prev_out, *, layer_idx, bb):
    n = state_all.shape[1]
    has_prev = prev_out is not None
    slab = lambda i, s: (layer_idx, i, 0, 0)
    in_specs = [pl.BlockSpec((1, bb, SSD_INNER, SSD_STATE), slab),
                pl.BlockSpec((n, SSD_INNER), lambda i, s: (0, 0)),
                pl.BlockSpec((bb, SSD_CONV_CH), lambda i, s: (i, 0))]
    operands = [da, state_all, xdt, xbc]
    if has_prev:
        in_specs.append(pl.BlockSpec(memory_space=pl.ANY))
        operands.append(prev_out)
    grid_spec = pltpu.PrefetchScalarGridSpec(
        num_scalar_prefetch=1,
        grid=(n // bb,),
        in_specs=in_specs,
        out_specs=[pl.BlockSpec((1, bb, SSD_INNER, SSD_STATE), slab),
                   pl.BlockSpec((SSD_INNER, n), lambda i, s: (0, 0))],
        scratch_shapes=[pltpu.VMEM((SSD_INNER, n), F32),
                        pltpu.VMEM((SSD_INNER, n), F32),
                        pltpu.VMEM((SSD_INNER, SSD_STATE), F32),
                        pltpu.VMEM((SSD_INNER, n), F32)],
    )
    return pl.pallas_call(
        functools.partial(_ssd_sample_state_kernel, bb=bb, has_prev=has_prev),
        grid_spec=grid_spec,
        out_shape=[jax.ShapeDtypeStruct(state_all.shape, F32),
                   jax.ShapeDtypeStruct((SSD_INNER, n), F32)],
        input_output_aliases={len(operands) - 1: 0} if has_prev else {},
        compiler_params=_cparams(1),
        name="ssd_sample_state",
    )(*operands)


def _ssd_sample_post_kernel(yt_ref, xbc_ref, uzx_ref, dskip_ref, ng_ref, o_ref):
    y = yt_ref[...].T + dskip_ref[...] * xbc_ref[:, 0:SSD_INNER]
    y = y * _silu(uzx_ref[:, 0:SSD_INNER])
    for g in range(SSD_GROUPS):
        cols = slice(g * SSD_GW, (g + 1) * SSD_GW)
        o_ref[:, cols] = _rms(y[:, cols], ng_ref[:, cols])


def _ssd_sample_post(yt, xbc, uzx, dskip, ng):
    n = xbc.shape[0]
    const = lambda i: (0, 0)
    return pl.pallas_call(
        _ssd_sample_post_kernel,
        grid=(1,),
        in_specs=[pl.BlockSpec((SSD_INNER, n), const),
                  pl.BlockSpec((n, SSD_CONV_CH), const),
                  pl.BlockSpec((n, SSD_ZX), const),
                  pl.BlockSpec((1, SSD_INNER), const),
                  pl.BlockSpec((1, SSD_INNER), const)],
        out_specs=pl.BlockSpec((n, SSD_INNER), const),
        out_shape=jax.ShapeDtypeStruct((n, SSD_INNER), F32),
        compiler_params=_cparams(1),
        name="ssd_sample_post",
    )(yt, xbc, uzx, dskip, ng)


def _pad_lanes(v):
    return jnp.pad(v.astype(F32), (0, LANES - v.shape[0]))[None, :]


def _head_expand(v):
    return jnp.repeat(v.astype(F32), SSD_HEAD_DIM)[None, :]


def _prep_layer(P, layer):
    i = layer // 2
    g = P["norm_g"][layer]
    w = {"g": [g[k][None, :] for k in range(4)],
         "w_up": P["mlp_w_up"][layer].astype(BF16),
         "w_down": P["mlp_w_down"][layer].astype(BF16)}
    if layer % 2 == 0:
        w.update(w_in=P["ab_w_in"][i].astype(BF16), w_out=P["ab_w_out"][i].astype(BF16),
                 dw=P["conf_dw_w"][i], db=P["conf_dw_b"][i][None, :],
                 lng=P["conf_ln_g"][i][None, :], lnb=P["conf_ln_b"][i][None, :],
                 sinks=P["attn_sinks"][i])
    else:
        w_in = P["ssd_w_in"][i]
        w_dt = jnp.pad(w_in[:, SSD_ZX:], ((0, 0), (0, LANES - SSD_HEADS)))
        w.update(w_zx=w_in[:, :SSD_ZX].astype(BF16), w_dt=w_dt.astype(BF16),
                 w_out=P["ssd_w_out"][i].astype(BF16),
                 cw=P["ssd_conv_w"][i], cb=P["ssd_conv_b"][i][None, :],
                 dtb=_pad_lanes(P["ssd_dt_bias"][i]), alog=_pad_lanes(P["ssd_a_log"][i]),
                 dskip=_head_expand(P["ssd_d"][i]), ng=P["ssd_norm_g"][i][None, :])
    return w


def _trunk_prompt(x3, W):
    bsz, t, d = x3.shape
    x = x3.reshape(bsz * t, d)
    tri = jnp.tril(jnp.ones((SSD_CHUNK, SSD_CHUNK), F32)).astype(BF16)
    win_k, win_v, conf_rows, ssm_states, ssd_rows = [], [], [], [], []
    tq = 512
    for layer in range(4):
        w = W[layer]
        g = w["g"]
        if layer % 2 == 0:
            x, tails, kv = _ab_prompt(x, g[0], w["w_in"], w["sinks"], w["dw"], w["db"], w["lng"], w["lnb"],
                                      w["w_out"], g[1], bsz=bsz, t=t, tq=tq)
            assert t >= WINDOW == ATT_BLOCK
            kv = kv.reshape(bsz, t // tq, ATT_BLOCK, 2 * KV_W)[:, -1]
            win_k.append(kv[:, :, :KV_W].reshape(bsz, WINDOW, N_KV_HEADS, HEAD_DIM))
            win_v.append(kv[:, :, KV_W:].reshape(bsz, WINDOW, N_KV_HEADS, HEAD_DIM))
            tails = tails.reshape(bsz, t // tq, CONV_HALO, CONF_CH)
            conf_rows.append(tails[:, -1, CONV_HALO - (CONF_KERNEL - 1):, :])
            x = _mlp(x, g[2], w["w_up"], w["w_down"], g[3], tm=512)
        else:
            y, state, tail = _ssd_prompt(x, g[0], w["w_zx"], w["w_dt"], w["cw"], w["cb"], w["dtb"], w["alog"],
                                         w["dskip"], w["ng"], tri, bsz=bsz, t=t)
            x = _proj_mlp(y, w["w_out"], x, g[1], g[2], w["w_up"], w["w_down"], g[3], tm=512)
            ssm_states.append(state.reshape(bsz, SSD_HEADS, SSD_HEAD_DIM, SSD_STATE))
            ssd_rows.append(tail.reshape(bsz, SSD_HALO, SSD_CONV_CH)[:, SSD_HALO - (SSD_CONV - 1):, :])
    return (x.reshape(bsz, t, d), jnp.stack(win_k), jnp.stack(win_v), jnp.stack(conf_rows),
            jnp.stack(ssm_states), jnp.stack(ssd_rows))


def _trunk_sample(x3, W, cache_k, cache_v, conf_hist, ssm, ssd_hist):
    n, t, d = x3.shape
    x = x3.reshape(n, d)
    expand = jnp.repeat(jnp.eye(LANES, SSD_HEADS, dtype=F32), SSD_HEAD_DIM, axis=1).astype(BF16)
    win_k, win_v, conf_rows, ssd_rows = [], [], [], []
    ssm_out = None
    head_group = jnp.arange(N_Q_HEADS) // Q_PER_KV
    for layer in range(4):
        w = W[layer]
        g = w["g"]
        i = layer // 2
        if layer % 2 == 0:
            u = _norm_matmul(x, g[0], w["w_in"], tm=n, tn=AB_IN // 2)
            c_out, a = _conf_sample(u, conf_hist[i], w["dw"], w["db"], w["lng"], w["lnb"], bb=32)
            k_off = GLU_W + Q_W
            q = u[:, GLU_W:k_off].reshape(n, N_Q_HEADS, HEAD_DIM)
            k_new = u[:, k_off:k_off + KV_W]
            v_new = u[:, k_off + KV_W:]
            lane_group = jnp.arange(KV_W) // HEAD_DIM
            qm = jnp.where(head_group[None, :, None] == lane_group[None, None, :],
                           jnp.tile(q, (1, 1, N_KV_HEADS)), 0.0)
            wc = cache_k.shape[2]
            o = _attn_sample(qm, cache_k[i].reshape(n, wc, KV_W), cache_v[i].reshape(n, wc, KV_W),
                             k_new[:, None, :], v_new[:, None, :], w["sinks"][:, None], bb=32)
            o = o.reshape(n, N_Q_HEADS, N_KV_HEADS, HEAD_DIM)
            a_out = jnp.take_along_axis(o, head_group[None, :, None, None], axis=2).reshape(n, Q_W)
            x = _proj_norm_res(jnp.concatenate([c_out, a_out], axis=-1), w["w_out"], x, g[1], tm=n)
            win_k.append(k_new.reshape(n, 1, N_KV_HEADS, HEAD_DIM))
            win_v.append(v_new.reshape(n, 1, N_KV_HEADS, HEAD_DIM))
            conf_rows.append(a[:, None, :])
        else:
            uzx = _norm_matmul(x, g[0], w["w_zx"], tm=n, tn=1024)
            udt = _norm_matmul(x, g[0], w["w_dt"], tm=n, tn=LANES)
            xbc, xdt, da = _ssd_sample_pre(uzx, udt, ssd_hist[i], w["cw"], w["cb"], w["dtb"], w["alog"], expand,
                                           bb=32)
            ssm_out, yt = _ssd_sample_state(da, ssm.reshape(ssm.shape[0], n, SSD_INNER, SSD_STATE), xdt, xbc,
                                            ssm_out, layer_idx=i, bb=8)
            y = _ssd_sample_post(yt, xbc, uzx, w["dskip"], w["ng"])
            x = _proj_norm_res(y, w["w_out"], x, g[1], tm=n)
            ssd_rows.append(uzx[:, None, SSD_INNER:])
        x = _mlp(x, g[2], w["w_up"], w["w_down"], g[3], tm=n)
    return (x.reshape(n, t, d), jnp.stack(win_k), jnp.stack(win_v), jnp.stack(conf_rows),
            ssm_out.reshape(ssm.shape), jnp.stack(ssd_rows))


def kernel(x_prompt, x_sample, cache_win_k, cache_win_v, state_conf_conv, state_ssm, state_ssd_conv,
           norm_g, ab_w_in, conf_dw_w, conf_dw_b, conf_ln_g, conf_ln_b, attn_sinks, ab_w_out,
           ssd_w_in, ssd_conv_w, ssd_conv_b, ssd_dt_bias, ssd_a_log, ssd_d, ssd_norm_g, ssd_w_out,
           mlp_w_up, mlp_w_down):
    P = {"norm_g": norm_g, "ab_w_in": ab_w_in, "conf_dw_w": conf_dw_w, "conf_dw_b": conf_dw_b,
         "conf_ln_g": conf_ln_g, "conf_ln_b": conf_ln_b, "attn_sinks": attn_sinks, "ab_w_out": ab_w_out,
         "ssd_w_in": ssd_w_in, "ssd_conv_w": ssd_conv_w, "ssd_conv_b": ssd_conv_b, "ssd_dt_bias": ssd_dt_bias,
         "ssd_a_log": ssd_a_log, "ssd_d": ssd_d, "ssd_norm_g": ssd_norm_g, "ssd_w_out": ssd_w_out,
         "mlp_w_up": mlp_w_up, "mlp_w_down": mlp_w_down}
    W = [_prep_layer(P, layer) for layer in range(4)]
    y_p, wk_p, wv_p, cc_p, ssm_p, sc_p = _trunk_prompt(x_prompt, W)
    y_s, wk_s, wv_s, cc_s, ssm_s, sc_s = _trunk_sample(x_sample, W, cache_win_k, cache_win_v,
                                                       state_conf_conv, state_ssm, state_ssd_conv)
    return (y_p, y_s, wk_p, wv_p, wk_s, wv_s, cc_p, cc_s, ssm_p, ssm_s, sc_p, sc_s)
```

```python
import functools

import jax
import jax.numpy as jnp
from jax import lax
from jax.experimental import pallas as pl
from jax.experimental.pallas import tpu as pltpu

F32 = jnp.float32
BF16 = jnp.bfloat16

D_MODEL = 1024
CONF_CH = 512
CONF_KERNEL = 31
HEAD_DIM = 64
N_Q_HEADS = 8
N_KV_HEADS = 2
Q_PER_KV = N_Q_HEADS // N_KV_HEADS
WINDOW = 128
ATT_BLOCK = 128
ATT_SCALE = HEAD_DIM ** -0.5
Q_W = N_Q_HEADS * HEAD_DIM
KV_W = N_KV_HEADS * HEAD_DIM
GLU_W = 2 * CONF_CH
AB_IN = GLU_W + Q_W + 2 * KV_W
SSD_INNER = 2048
SSD_HEAD_DIM = 64
SSD_HEADS = 32
SSD_GROUPS = 4
SSD_HPG = SSD_HEADS // SSD_GROUPS
SSD_STATE = 128
SSD_CONV = 4
SSD_BC = SSD_GROUPS * SSD_STATE
SSD_CONV_CH = SSD_INNER + 2 * SSD_BC
SSD_ZX = SSD_INNER + SSD_CONV_CH
SSD_GW = SSD_INNER // SSD_GROUPS
SSD_CHUNK = 128
D_FF = 4096
RMS_EPS = 1e-6
LN_EPS = 1e-5

LANES = 128
SUBLANES = 8
VMEM_LIMIT = 56 * 1024 * 1024

NEG_INF = float("-inf")


def _cparams(n_axes):
    return pltpu.CompilerParams(dimension_semantics=("arbitrary",) * n_axes,
                                vmem_limit_bytes=VMEM_LIMIT)


def _layer_spec(stacked, block, index_map, **kw):
    layer = stacked[1]
    return pl.BlockSpec((None,) + tuple(block), lambda *g: (layer,) + tuple(index_map(*g)), **kw)


def _rms(x, g):
    return x * lax.rsqrt(jnp.mean(x * x, axis=-1, keepdims=True) + RMS_EPS) * g


def _sigmoid(x):
    return 1.0 / (1.0 + jnp.exp(-x))


def _silu(x):
    return x * _sigmoid(x)


def _log1p(e):
    u = 1.0 + e
    tiny = u == 1.0
    return jnp.where(tiny, e, jnp.log(u) * (e / jnp.where(tiny, 1.0, u - 1.0)))


def _softplus(x):
    return jnp.maximum(x, 0.0) + _log1p(jnp.exp(-jnp.abs(x)))


def _dot(a, b):
    return jnp.dot(a, b, preferred_element_type=F32)


def _dot_nt(a, b):
    return lax.dot_general(a, b, (((1,), (1,)), ((), ())), preferred_element_type=F32)


def _dot_tn(a, b):
    return lax.dot_general(a, b, (((0,), (0,)), ((), ())), preferred_element_type=F32)


def _split3(x):
    hi = x.astype(BF16)
    r1 = x - hi.astype(F32)
    mid = r1.astype(BF16)
    lo = (r1 - mid.astype(F32)).astype(BF16)
    return hi, mid, lo


def _dot_sel_left(sel, x):
    hi, mid, lo = _split3(x)
    return _dot(sel, hi) + _dot(sel, mid) + _dot(sel, lo)


def _dot_sel_right(x, sel):
    hi, mid, lo = _split3(x)
    return _dot(hi, sel) + _dot(mid, sel) + _dot(lo, sel)


def _norm_matmul_kernel(x_ref, g_ref, w_ref, o_ref, hn_ref):
    @pl.when(pl.program_id(1) == 0)
    def _():
        hn_ref[...] = _rms(x_ref[...], g_ref[...]).astype(BF16)

    o_ref[...] = _dot(hn_ref[...], w_ref[...])


def _norm_matmul(x, g, w, *, n, tm, tn):
    m, d = x.shape
    return pl.pallas_call(
        _norm_matmul_kernel,
        grid=(m // tm, n // tn),
        in_specs=[pl.BlockSpec((tm, d), lambda i, j: (i, 0)),
                  pl.BlockSpec((1, d), lambda i, j: (0, 0)),
                  _layer_spec(w, (d, tn), lambda i, j: (0, j))],
        out_specs=pl.BlockSpec((tm, tn), lambda i, j: (i, j)),
        out_shape=jax.ShapeDtypeStruct((m, n), F32),
        scratch_shapes=[pltpu.VMEM((tm, d), BF16)],
        compiler_params=_cparams(2),
        name="norm_matmul",
    )(x, g, w[0])


def _proj_norm_res_kernel(a_ref, w_ref, x_ref, g_ref, o_ref):
    y = _dot(a_ref[...].astype(BF16), w_ref[...])
    o_ref[...] = x_ref[...] + _rms(y, g_ref[...])


def _proj_norm_res(a, w, x, g, *, tm):
    m, k = a.shape
    d = x.shape[1]
    return pl.pallas_call(
        _proj_norm_res_kernel,
        grid=(m // tm,),
        in_specs=[pl.BlockSpec((tm, k), lambda i: (i, 0)),
                  _layer_spec(w, (k, d), lambda i: (0, 0)),
                  pl.BlockSpec((tm, d), lambda i: (i, 0)),
                  pl.BlockSpec((1, d), lambda i: (0, 0))],
        out_specs=pl.BlockSpec((tm, d), lambda i: (i, 0)),
        out_shape=jax.ShapeDtypeStruct((m, d), F32),
        compiler_params=_cparams(1),
        name="proj_norm_res",
    )(a, w[0], x, g)


def _mlp_block(x, gpre_ref, wup_ref, wdn_ref, gpost_ref, tf):
    h = _rms(x, gpre_ref[...]).astype(BF16)
    f = None
    for c in range(D_FF // tf):
        a = jnp.maximum(_dot(h, wup_ref[:, c * tf:(c + 1) * tf]), 0.0)
        part = _dot((a * a).astype(BF16), wdn_ref[c * tf:(c + 1) * tf, :])
        f = part if f is None else f + part
    return x + _rms(f, gpost_ref[...])


def _mlp_kernel(x_ref, gpre_ref, wup_ref, wdn_ref, gpost_ref, o_ref, *, tf):
    o_ref[...] = _mlp_block(x_ref[...], gpre_ref, wup_ref, wdn_ref, gpost_ref, tf)


def _proj_mlp_kernel(a_ref, wo_ref, x_ref, gmix_ref, gpre_ref, wup_ref, wdn_ref, gpost_ref, o_ref, *, tf):
    x_mid = x_ref[...] + _rms(_dot(a_ref[...].astype(BF16), wo_ref[...]), gmix_ref[...])
    o_ref[...] = _mlp_block(x_mid, gpre_ref, wup_ref, wdn_ref, gpost_ref, tf)


def _mlp(x, gpre, wup, wdn, gpost, *, tm, tf=1024):
    m, d = x.shape
    return pl.pallas_call(
        functools.partial(_mlp_kernel, tf=tf),
        grid=(m // tm,),
        in_specs=[pl.BlockSpec((tm, d), lambda i: (i, 0)),
                  pl.BlockSpec((1, d), lambda i: (0, 0)),
                  _layer_spec(wup, (d, D_FF), lambda i: (0, 0)),
                  _layer_spec(wdn, (D_FF, d), lambda i: (0, 0)),
                  pl.BlockSpec((1, d), lambda i: (0, 0))],
        out_specs=pl.BlockSpec((tm, d), lambda i: (i, 0)),
        out_shape=jax.ShapeDtypeStruct((m, d), F32),
        compiler_params=_cparams(1),
        name="mlp",
    )(x, gpre, wup[0], wdn[0], gpost)


def _proj_mlp(a, wo, x, gmix, gpre, wup, wdn, gpost, *, tm, tf=1024):
    m, d = x.shape
    k = a.shape[1]
    row = lambda i: (i, 0)
    const = lambda i: (0, 0)
    resident = pl.Buffered(1)
    return pl.pallas_call(
        functools.partial(_proj_mlp_kernel, tf=tf),
        grid=(m // tm,),
        in_specs=[pl.BlockSpec((tm, k), row),
                  _layer_spec(wo, (k, d), const, pipeline_mode=resident),
                  pl.BlockSpec((tm, d), row),
                  pl.BlockSpec((1, d), const),
                  pl.BlockSpec((1, d), const),
                  _layer_spec(wup, (d, D_FF), const, pipeline_mode=resident),
                  _layer_spec(wdn, (D_FF, d), const, pipeline_mode=resident),
                  pl.BlockSpec((1, d), const)],
        out_specs=pl.BlockSpec((tm, d), row),
        out_shape=jax.ShapeDtypeStruct((m, d), F32),
        compiler_params=_cparams(1),
        name="proj_mlp",
    )(a, wo[0], x, gmix, gpre, wup[0], wdn[0], gpost)


CONV_HALO = 32
CONV_ROWS = 64
AB_PROJ_COLS = 256


def _glu(u):
    return u[:, :CONF_CH] * _sigmoid(u[:, CONF_CH:GLU_W])


def _ab_prompt_kernel(sink_ref, x_ref, gin_ref, win_ref, dw_ref, db_ref, lng_ref, lnb_ref,
                      wout_ref, g_ref, o_ref, tail_ref, kvtail_ref,
                      u_ref, ukv_prev_ref, ap_ref, sh_ref, mix_ref, s_ref, p_ref, *, tq):
    i = pl.program_id(1)
    has_prev = i > 0
    n_ap = CONV_HALO + tq
    k_off = GLU_W + Q_W

    @pl.when(i == 0)
    def _():
        ap_ref[tq:n_ap, :] = jnp.zeros((CONV_HALO, CONF_CH), F32)
        u_ref[tq - ATT_BLOCK:tq, k_off:AB_IN] = jnp.zeros((ATT_BLOCK, 2 * KV_W), F32)

    ap_ref[0:CONV_HALO, :] = ap_ref[tq:n_ap, :]
    ukv_prev_ref[...] = u_ref[tq - ATT_BLOCK:tq, k_off:AB_IN]

    hn = _rms(x_ref[...], gin_ref[...]).astype(BF16)
    for cc in range(AB_IN // AB_PROJ_COLS):
        cols = slice(cc * AB_PROJ_COLS, (cc + 1) * AB_PROJ_COLS)
        u_ref[:, cols] = _dot(hn, win_ref[:, cols])
    kvtail_ref[...] = u_ref[tq - ATT_BLOCK:tq, k_off:AB_IN]

    ap_ref[CONV_HALO:n_ap, :] = _glu(u_ref[:, 0:GLU_W])
    ap_ref[n_ap:n_ap + SUBLANES, :] = jnp.zeros((SUBLANES, CONF_CH), F32)
    tail_ref[...] = ap_ref[tq:tq + CONV_HALO, :]
    for r in range(1, SUBLANES):
        sh_ref[r - 1] = ap_ref[r:r + n_ap, :]

    first_tap = CONV_HALO - (CONF_KERNEL - 1)
    for rc in range(tq // CONV_ROWS):
        acc = jnp.broadcast_to(db_ref[...], (CONV_ROWS, CONF_CH))
        for k in range(CONF_KERNEL):
            tile, r = divmod(first_tap + k, SUBLANES)
            start = rc * CONV_ROWS + tile * SUBLANES
            if r == 0:
                rows = ap_ref[start:start + CONV_ROWS, :]
            else:
                rows = sh_ref[r - 1, start:start + CONV_ROWS, :]
            acc = acc + rows * dw_ref[k:k + 1, :]
        mu = jnp.mean(acc, axis=-1, keepdims=True)
        xc = acc - mu
        var = jnp.mean(xc * xc, axis=-1, keepdims=True)
        ln = xc * lax.rsqrt(var + LN_EPS) * lng_ref[...] + lnb_ref[...]
        mix_ref[rc * CONV_ROWS:(rc + 1) * CONV_ROWS, 0:CONF_CH] = _silu(ln).astype(BF16)

    n_blk = tq // ATT_BLOCK
    n_keys = 2 * ATT_BLOCK
    grp_rows = Q_PER_KV * ATT_BLOCK
    blk_rows = N_Q_HEADS * ATT_BLOCK
    k_off = GLU_W + Q_W
    v_off = k_off + KV_W

    def kv_rows(j, off):
        rows = slice(j * ATT_BLOCK, (j + 1) * ATT_BLOCK)
        if j == 0:
            prev = ukv_prev_ref[:, off - k_off:off - k_off + KV_W]
        else:
            prev = u_ref[(j - 1) * ATT_BLOCK:j * ATT_BLOCK, off:off + KV_W]
        return jnp.concatenate([prev, u_ref[rows, off:off + KV_W]], axis=0)

    col0 = lax.broadcasted_iota(jnp.int32, (grp_rows, n_keys), 1) == 0
    for j in range(n_blk):
        rows = slice(j * ATT_BLOCK, (j + 1) * ATT_BLOCK)
        q = (u_ref[rows, GLU_W:GLU_W + Q_W] * ATT_SCALE).astype(BF16)
        kk = kv_rows(j, k_off).astype(BF16)
        for g in range(N_KV_HEADS):
            heads = [g * Q_PER_KV + hh for hh in range(Q_PER_KV)]
            qg = jnp.concatenate([q[:, h * HEAD_DIM:(h + 1) * HEAD_DIM] for h in heads], axis=0)
            s = _dot_nt(qg, kk[:, g * HEAD_DIM:(g + 1) * HEAD_DIM])
            sink = jnp.concatenate([jnp.full((ATT_BLOCK, 1), sink_ref[h], F32) for h in heads], axis=0)
            base = j * blk_rows + g * grp_rows
            s_ref[base:base + grp_rows, :] = jnp.where(col0, sink, s)

    qi = lax.broadcasted_iota(jnp.int32, (ATT_BLOCK, n_keys), 0)
    kc = lax.broadcasted_iota(jnp.int32, (ATT_BLOCK, n_keys), 1)
    visible = ((kc > qi) & (kc <= qi + WINDOW)) | (kc == 0)
    visible_first = visible & ((kc >= ATT_BLOCK) | (kc == 0) | has_prev)
    bias = jnp.where(visible, 0.0, NEG_INF)
    bias_first = jnp.where(visible_first, 0.0, NEG_INF)
    for lo, hi, bb in ((0, blk_rows, bias_first), (blk_rows, n_blk * blk_rows, bias)):
        sm = s_ref[lo:hi, :].reshape((hi - lo) // ATT_BLOCK, ATT_BLOCK, n_keys) + bb
        m = jnp.max(sm, axis=-1, keepdims=True)
        p_ref[lo:hi, :] = jnp.exp(sm - m).astype(BF16).reshape(hi - lo, n_keys)

    vrow = lax.broadcasted_iota(jnp.int32, (n_keys, HEAD_DIM), 0)
    ones = jnp.ones((n_keys, HEAD_DIM), BF16)
    lo_half = lax.broadcasted_iota(jnp.int32, (ATT_BLOCK, 2 * HEAD_DIM), 1) < HEAD_DIM
    for j in range(n_blk):
        rows = slice(j * ATT_BLOCK, (j + 1) * ATT_BLOCK)
        vv = kv_rows(j, v_off)
        for g in range(N_KV_HEADS):
            vg = jnp.where(vrow > 0, vv[:, g * HEAD_DIM:(g + 1) * HEAD_DIM], 0.0).astype(BF16)
            v_ext = (jnp.concatenate([vg, ones], axis=1), jnp.concatenate([ones, vg], axis=1))
            for pr in range(Q_PER_KV // 2):
                normed = []
                for par in range(2):
                    base = j * blk_rows + g * grp_rows + (2 * pr + par) * ATT_BLOCK
                    oe = _dot(p_ref[base:base + ATT_BLOCK, :], v_ext[par])
                    normed.append(oe / pltpu.roll(oe, HEAD_DIM, 1))
                c0 = CONF_CH + (g * Q_PER_KV + 2 * pr) * HEAD_DIM
                mix_ref[rows, c0:c0 + 2 * HEAD_DIM] = jnp.where(lo_half, normed[0], normed[1]).astype(BF16)

    y = _dot(mix_ref[...], wout_ref[...])
    o_ref[...] = x_ref[...] + _rms(y, g_ref[...])


def _ab_prompt(x, gin, win, sinks, dw, db, lng, lnb, wout, g, *, bsz, t, tq):
    nt = t // tq
    m = bsz * t
    d = x.shape[1]

    def cur(b, i, s):
        return (b * nt + i, 0)

    const = lambda b, i, s: (0, 0)
    resident = pl.Buffered(1)
    grid_spec = pltpu.PrefetchScalarGridSpec(
        num_scalar_prefetch=1,
        grid=(bsz, nt),
        in_specs=[pl.BlockSpec((tq, d), cur),
                  pl.BlockSpec((1, d), const),
                  _layer_spec(win, (d, AB_IN), const, pipeline_mode=resident),
                  pl.BlockSpec(dw.shape, const),
                  pl.BlockSpec((1, CONF_CH), const),
                  pl.BlockSpec((1, CONF_CH), const),
                  pl.BlockSpec((1, CONF_CH), const),
                  _layer_spec(wout, (CONF_CH + Q_W, d), const, pipeline_mode=resident),
                  pl.BlockSpec((1, d), const)],
        out_specs=[pl.BlockSpec((tq, d), cur),
                   pl.BlockSpec((CONV_HALO, CONF_CH), cur),
                   pl.BlockSpec((ATT_BLOCK, 2 * KV_W), cur)],
        scratch_shapes=[pltpu.VMEM((tq, AB_IN), F32),
                        pltpu.VMEM((ATT_BLOCK, 2 * KV_W), F32),
                        pltpu.VMEM((CONV_HALO + tq + SUBLANES, CONF_CH), F32),
                        pltpu.VMEM((SUBLANES - 1, CONV_HALO + tq, CONF_CH), F32),
                        pltpu.VMEM((tq, CONF_CH + Q_W), BF16),
                        pltpu.VMEM((tq * N_Q_HEADS, 2 * ATT_BLOCK), F32),
                        pltpu.VMEM((tq * N_Q_HEADS, 2 * ATT_BLOCK), BF16)],
    )
    return pl.pallas_call(
        functools.partial(_ab_prompt_kernel, tq=tq),
        grid_spec=grid_spec,
        out_shape=[jax.ShapeDtypeStruct((m, d), F32),
                   jax.ShapeDtypeStruct((bsz * nt * CONV_HALO, CONF_CH), F32),
                   jax.ShapeDtypeStruct((bsz * nt * ATT_BLOCK, 2 * KV_W), F32)],
        compiler_params=_cparams(2),
        name="ab_prompt",
    )(sinks, x, gin, win[0], dw, db, lng, lnb, wout[0], g)


SSD_HALO = SUBLANES
SSD_CONV_COLS = 256
SSD_PROJ_COLS = 512
SSD_STEP_CHUNKS = 2


def _ssd_conv_masks(ch):
    sub = lax.broadcasted_iota(jnp.int32, (SUBLANES, ch), 0)
    return [sub < back for back in range(SSD_CONV)]


def _ssd_conv_silu(halo, cur, w, b, wrapped):
    rows, ch = cur.shape
    tiles = [halo] + [cur[r:r + SUBLANES, :] for r in range(0, rows, SUBLANES)]
    acc = b + cur * w[SSD_CONV - 1:SSD_CONV, :]
    for back in range(1, SSD_CONV):
        rolled = [pltpu.roll(v, back, 0) for v in tiles]
        shifted = jnp.concatenate([jnp.where(wrapped[back], rolled[i], rolled[i + 1])
                                   for i in range(len(tiles) - 1)], axis=0)
        acc = acc + shifted * w[SSD_CONV - 1 - back:SSD_CONV - back, :]
    return _silu(acc)


def _ssd_prompt_kernel(x0_ref, xn_ref, g_ref, wzx_ref, wdt_ref, cw_ref, cb_ref, dtb_ref, alog_ref, dskip_ref,
                       ng_ref, tri_ref, y_ref, state_ref, tail_ref,
                       ua_ref, dta_ref, ub_ref, dtb2_ref, hn_ref, halo_ref, xbc_ref, h_ref, xdec_ref, *, n_steps):
    c = pl.program_id(1)
    n = pl.program_id(0) * n_steps + c

    def in_proj_pieces(x_ref, u_ref, udt_ref):
        def norm():
            hn_ref[...] = _rms(x_ref[...], g_ref[...]).astype(BF16)

        def zx(cc):
            cols = slice(cc * SSD_PROJ_COLS, (cc + 1) * SSD_PROJ_COLS)
            u_ref[:, cols] = _dot(hn_ref[...], wzx_ref[:, cols])

        def dt():
            udt_ref[...] = _dot(hn_ref[...], wdt_ref[...])

        return [norm] + [functools.partial(zx, cc) for cc in range(SSD_ZX // SSD_PROJ_COLS)] + [dt]

    @pl.when(n == 0)
    def _():
        for piece in in_proj_pieces(x0_ref, ua_ref, dta_ref):
            piece()

    @pl.when(c == 0)
    def _():
        halo_ref[...] = jnp.zeros_like(halo_ref)
        h_ref[...] = jnp.zeros_like(h_ref)

    refs = (cw_ref, cb_ref, dtb_ref, alog_ref, dskip_ref, ng_ref, tri_ref, y_ref,
            halo_ref, xbc_ref, h_ref, xdec_ref)

    def step(u_ref, udt_ref, fillers):
        n_stages = SSD_STEP_CHUNKS * (SSD_CONV_CH // SSD_CONV_COLS + SSD_HEADS // 2)
        n_fill = len(fillers)
        stage = [0]

        def fill():
            stage[0] += 1
            while fillers and n_fill - len(fillers) < (stage[0] * n_fill + n_stages - 1) // n_stages:
                fillers.pop(0)()

        for k in range(SSD_STEP_CHUNKS):
            _ssd_chunk(u_ref, udt_ref, k * SSD_CHUNK, *refs, fill=fill)
        assert not fillers

    @pl.when(n % 2 == 0)
    def _():
        step(ua_ref, dta_ref, in_proj_pieces(xn_ref, ub_ref, dtb2_ref))

    @pl.when(n % 2 == 1)
    def _():
        step(ub_ref, dtb2_ref, in_proj_pieces(xn_ref, ua_ref, dta_ref))

    @pl.when(c == n_steps - 1)
    def _():
        state_ref[0] = h_ref[...]
        tail_ref[...] = halo_ref[...]


def _ssd_chunk(u_ref, udt_ref, r0, cw_ref, cb_ref, dtb_ref, alog_ref, dskip_ref, ng_ref, tri_ref, y_ref,
               halo_ref, xbc_ref, h_ref, xdec_ref, *, fill):
    L = SSD_CHUNK
    crows = slice(r0, r0 + L)
    wrapped = _ssd_conv_masks(SSD_CONV_COLS)
    for cc in range(SSD_CONV_CH // SSD_CONV_COLS):
        fill()
        cols = slice(cc * SSD_CONV_COLS, (cc + 1) * SSD_CONV_COLS)
        ucols = slice(SSD_INNER + cc * SSD_CONV_COLS, SSD_INNER + (cc + 1) * SSD_CONV_COLS)
        xbc_ref[:, cols] = _ssd_conv_silu(halo_ref[:, cols], u_ref[crows, ucols], cw_ref[:, cols], cb_ref[:, cols],
                                          wrapped)
    halo_ref[...] = u_ref[r0 + L - SSD_HALO:r0 + L, SSD_INNER:SSD_ZX]

    dt = _softplus(udt_ref[crows, :] + dtb_ref[...])
    a = -jnp.exp(alog_ref[...])
    cs = _dot_sel_left(tri_ref[...], dt * a)
    cs_t = cs.T

    row = lax.broadcasted_iota(jnp.int32, (L, LANES), 0)
    lane = lax.broadcasted_iota(jnp.int32, (L, LANES), 1)
    causal = row >= lane
    lo_half = lane < SSD_HEAD_DIM

    def bc(v, h):
        return jnp.broadcast_to(v[:, h:h + 1], (L, LANES))

    for g in range(SSD_GROUPS):
        b_g = xbc_ref[:, SSD_INNER + g * SSD_STATE:SSD_INNER + (g + 1) * SSD_STATE].astype(BF16)
        c_g = xbc_ref[:, SSD_INNER + SSD_BC + g * SSD_STATE:SSD_INNER + SSD_BC + (g + 1) * SSD_STATE].astype(BF16)
        hrows = slice(g * SSD_GW, (g + 1) * SSD_GW)
        cb = _dot_nt(c_g, b_g)
        yoff_raw = _dot_nt(c_g, h_ref[hrows, :].astype(BF16))
        end_decay = []
        gated = []
        for pr in range(SSD_HPG // 2):
            fill()
            h1 = g * SSD_HPG + 2 * pr
            h2 = h1 + 1
            cols = slice(h1 * SSD_HEAD_DIM, (h1 + 2) * SSD_HEAD_DIM)
            a1 = bc(cs, h1)
            a2 = bc(cs, h2)
            m1 = cb * jnp.exp(jnp.where(causal, a1 - cs_t[h1:h1 + 1, :], NEG_INF))
            m2 = cb * jnp.exp(jnp.where(causal, a2 - cs_t[h2:h2 + 1, :], NEG_INF))
            xs = xbc_ref[:, cols]
            xc = xs * jnp.where(lo_half, bc(dt, h1), bc(dt, h2))
            lhs = jnp.concatenate([m1, m2], axis=1).astype(BF16)
            rhs = jnp.concatenate([jnp.where(lo_half, xc, 0.0), jnp.where(lo_half, 0.0, xc)], axis=0).astype(BF16)
            y_diag = _dot(lhs, rhs)
            csp = jnp.where(lo_half, a1, a2)
            y_off = jnp.exp(csp) * yoff_raw[:, 2 * pr * SSD_HEAD_DIM:(2 * pr + 2) * SSD_HEAD_DIM]
            xdec_ref[:, cols] = (xc * jnp.exp(csp[L - 1:L, :] - csp)).astype(BF16)
            end_decay += [jnp.exp(a1[L - 1:L, :]), jnp.exp(a2[L - 1:L, :])]
            y = y_diag + y_off + dskip_ref[:, cols] * xs
            gated.append(y * _silu(u_ref[crows, cols]))
        ss = sum(jnp.sum(v * v, axis=-1, keepdims=True) for v in gated)
        scale = lax.rsqrt(ss * (1.0 / SSD_GW) + RMS_EPS)
        for pr in range(SSD_HPG // 2):
            cols = slice(g * SSD_GW + pr * LANES, g * SSD_GW + (pr + 1) * LANES)
            y_ref[crows, cols] = (gated[pr] * scale * ng_ref[:, cols]).astype(y_ref.dtype)
        st = _dot_tn(xdec_ref[:, hrows], b_g)
        for k in range(SSD_HPG):
            hs = slice(g * SSD_GW + k * SSD_HEAD_DIM, g * SSD_GW + (k + 1) * SSD_HEAD_DIM)
            h_ref[hs, :] = h_ref[hs, :] * end_decay[k] + st[k * SSD_HEAD_DIM:(k + 1) * SSD_HEAD_DIM, :]


def _ssd_prompt(x, g, wzx, wdt, cw, cb, dtb, alog, dskip, ng, tri, *, bsz, t):
    rows = SSD_STEP_CHUNKS * SSD_CHUNK
    nc = t // rows
    L = SSD_CHUNK
    d = x.shape[1]
    last = bsz * nc - 1
    cur = lambda b, c: (b * nc + c, 0)
    nxt = lambda b, c: (jnp.minimum(b * nc + c + 1, last), 0)
    const = lambda b, c: (0, 0)
    resident = pl.Buffered(1)
    return pl.pallas_call(
        functools.partial(_ssd_prompt_kernel, n_steps=nc),
        grid=(bsz, nc),
        in_specs=[pl.BlockSpec((rows, d), const),
                  pl.BlockSpec((rows, d), nxt),
                  pl.BlockSpec((1, d), const),
                  _layer_spec(wzx, (d, SSD_ZX), const, pipeline_mode=resident),
                  _layer_spec(wdt, (d, LANES), const, pipeline_mode=resident),
                  pl.BlockSpec((SSD_CONV, SSD_CONV_CH), const),
                  pl.BlockSpec((1, SSD_CONV_CH), const),
                  pl.BlockSpec((1, LANES), const),
                  pl.BlockSpec((1, LANES), const),
                  pl.BlockSpec((1, SSD_INNER), const),
                  pl.BlockSpec((1, SSD_INNER), const),
                  pl.BlockSpec((L, L), const)],
        out_specs=[pl.BlockSpec((rows, SSD_INNER), cur),
                   pl.BlockSpec((1, SSD_INNER, SSD_STATE), lambda b, c: (b, 0, 0)),
                   pl.BlockSpec((SSD_HALO, SSD_CONV_CH), lambda b, c: (b, 0))],
        out_shape=[jax.ShapeDtypeStruct((bsz * t, SSD_INNER), BF16),
                   jax.ShapeDtypeStruct((bsz, SSD_INNER, SSD_STATE), F32),
                   jax.ShapeDtypeStruct((bsz * SSD_HALO, SSD_CONV_CH), F32)],
        scratch_shapes=[pltpu.VMEM((rows, SSD_ZX), F32),
                        pltpu.VMEM((rows, LANES), F32),
                        pltpu.VMEM((rows, SSD_ZX), F32),
                        pltpu.VMEM((rows, LANES), F32),
                        pltpu.VMEM((rows, d), BF16),
                        pltpu.VMEM((SSD_HALO, SSD_CONV_CH), F32),
                        pltpu.VMEM((L, SSD_CONV_CH), F32),
                        pltpu.VMEM((SSD_INNER, SSD_STATE), F32),
                        pltpu.VMEM((L, SSD_INNER), BF16)],
        compiler_params=_cparams(2),
        name="ssd_prompt",
    )(x, x, g, wzx[0], wdt[0], cw, cb, dtb, alog, dskip, ng, tri)


def _conf_sample_kernel(u_ref, hist_ref, dw_ref, db_ref, lng_ref, lnb_ref, c_ref, a_ref):
    a = _glu(u_ref[:, 0:GLU_W])
    a_ref[...] = a
    nh = CONF_KERNEL - 1
    acc = db_ref[...] + jnp.sum(hist_ref[...] * dw_ref[0:nh, :], axis=1) + a * dw_ref[nh:nh + 1, :]
    mu = jnp.mean(acc, axis=-1, keepdims=True)
    xc = acc - mu
    var = jnp.mean(xc * xc, axis=-1, keepdims=True)
    c_ref[...] = _silu(xc * lax.rsqrt(var + LN_EPS) * lng_ref[...] + lnb_ref[...])


def _conf_sample(u, hist, dw, db, lng, lnb, *, bb):
    n = u.shape[0]
    nh = CONF_KERNEL - 1
    const = lambda i: (0, 0)
    return pl.pallas_call(
        _conf_sample_kernel,
        grid=(n // bb,),
        in_specs=[pl.BlockSpec((bb, AB_IN), lambda i: (i, 0)),
                  _layer_spec(hist, (bb, nh, CONF_CH), lambda i: (i, 0, 0)),
                  pl.BlockSpec(dw.shape, const),
                  pl.BlockSpec((1, CONF_CH), const),
                  pl.BlockSpec((1, CONF_CH), const),
                  pl.BlockSpec((1, CONF_CH), const)],
        out_specs=[pl.BlockSpec((bb, CONF_CH), lambda i: (i, 0)),
                   pl.BlockSpec((bb, CONF_CH), lambda i: (i, 0))],
        out_shape=[jax.ShapeDtypeStruct((n, CONF_CH), F32),
                   jax.ShapeDtypeStruct((n, CONF_CH), F32)],
        compiler_params=_cparams(1),
        name="conf_sample",
    )(u, hist[0], dw, db, lng, lnb)


def _attn_sample_kernel(qm_ref, kc_ref, vc_ref, kn_ref, vn_ref, sink_ref, o_ref, *, wc):
    q = qm_ref[...] * ATT_SCALE
    s = jnp.einsum("bhc,bsc->bhs", q.astype(BF16), kc_ref[...].astype(BF16), preferred_element_type=F32)
    kpos = lax.broadcasted_iota(jnp.int32, s.shape, 2)
    s = jnp.where((kpos <= wc) & (wc - kpos < WINDOW), s, NEG_INF)
    s_new = jnp.sum(q * kn_ref[...], axis=-1, keepdims=True)
    sink = sink_ref[...][None]
    m = jnp.maximum(jnp.maximum(jnp.max(s, axis=-1, keepdims=True), s_new), sink)
    p = jnp.exp(s - m)
    p_new = jnp.exp(s_new - m)
    den = jnp.sum(p, axis=-1, keepdims=True) + p_new + jnp.exp(sink - m)
    o = jnp.einsum("bhs,bsc->bhc", p.astype(BF16), vc_ref[...].astype(BF16), preferred_element_type=F32)
    o_ref[...] = (o + p_new * vn_ref[...]) / den


def _attn_sample(qm, kc, vc, kn, vn, sink, *, bb):
    _, n, wc, _ = kc[0].shape
    blk3 = lambda i: (i, 0, 0)
    return pl.pallas_call(
        functools.partial(_attn_sample_kernel, wc=wc),
        grid=(n // bb,),
        in_specs=[pl.BlockSpec((bb, N_Q_HEADS, KV_W), blk3),
                  _layer_spec(kc, (bb, wc, KV_W), blk3),
                  _layer_spec(vc, (bb, wc, KV_W), blk3),
                  pl.BlockSpec((bb, 1, KV_W), blk3),
                  pl.BlockSpec((bb, 1, KV_W), blk3),
                  pl.BlockSpec((N_Q_HEADS, 1), lambda i: (0, 0))],
        out_specs=pl.BlockSpec((bb, N_Q_HEADS, KV_W), blk3),
        out_shape=jax.ShapeDtypeStruct((n, N_Q_HEADS, KV_W), F32),
        compiler_params=_cparams(1),
        name="attn_sample",
    )(qm, kc[0], vc[0], kn, vn, sink)


def _ssd_sample_pre_kernel(uzx_ref, udt_ref, hist_ref, cw_ref, cb_ref, dtb_ref, alog_ref, expand_ref,
                           xbc_ref, xdt_ref, da_ref):
    nh = SSD_CONV - 1
    xbc = uzx_ref[:, SSD_INNER:SSD_ZX]
    acc = cb_ref[...] + jnp.sum(hist_ref[...] * cw_ref[0:nh, :], axis=1) + xbc * cw_ref[nh:nh + 1, :]
    conv = _silu(acc)
    xbc_ref[...] = conv
    dt = _softplus(udt_ref[...] + dtb_ref[...])
    da_ref[...] = jnp.exp(dt * (-jnp.exp(alog_ref[...])))
    xdt_ref[...] = conv[:, 0:SSD_INNER] * _dot_sel_right(dt, expand_ref[...])


def _ssd_sample_pre(uzx, udt, hist, cw, cb, dtb, alog, expand, *, bb):
    n = uzx.shape[0]
    const = lambda i: (0, 0)
    row = lambda i: (i, 0)
    return pl.pallas_call(
        _ssd_sample_pre_kernel,
        grid=(n // bb,),
        in_specs=[pl.BlockSpec((bb, SSD_ZX), row),
                  pl.BlockSpec((bb, LANES), row),
                  _layer_spec(hist, (bb, SSD_CONV - 1, SSD_CONV_CH), lambda i: (i, 0, 0)),
                  pl.BlockSpec((SSD_CONV, SSD_CONV_CH), const),
                  pl.BlockSpec((1, SSD_CONV_CH), const),
                  pl.BlockSpec((1, LANES), const),
                  pl.BlockSpec((1, LANES), const),
                  pl.BlockSpec((LANES, SSD_INNER), const)],
        out_specs=[pl.BlockSpec((bb, SSD_CONV_CH), row),
                   pl.BlockSpec((bb, SSD_INNER), row),
                   pl.BlockSpec((bb, LANES), row)],
        out_shape=[jax.ShapeDtypeStruct((n, SSD_CONV_CH), F32),
                   jax.ShapeDtypeStruct((n, SSD_INNER), F32),
                   jax.ShapeDtypeStruct((n, LANES), F32)],
        compiler_params=_cparams(1),
        name="ssd_sample_pre",
    )(uzx, udt, hist[0], cw, cb, dtb, alog, expand)


def _ssd_sample_state_kernel(da_ref, st_ref, xdt_ref, xbc_ref, *rest, bb, has_prev):
    so_ref, yt_ref, xt_ref, xr_ref, xb_ref, yr_ref = rest[1:] if has_prev else rest
    i = pl.program_id(0)

    @pl.when(i == 0)
    def _():
        xt_ref[...] = xdt_ref[...].T
        yt_ref[...] = jnp.zeros_like(yt_ref)
        yr_ref[...] = jnp.zeros_like(yr_ref)

    xr_ref[...] = pltpu.roll(xt_ref[...], (LANES - i * bb) % LANES, 1)
    lane_g = lax.broadcasted_iota(jnp.int32, (SSD_GW, LANES), 1)
    for j in range(bb):
        b = i * bb + j
        xb_ref[...] = jnp.broadcast_to(xr_ref[:, j:j + 1], xb_ref.shape)
        for g in range(SSD_GROUPS):
            b_row = xbc_ref[j:j + 1, SSD_INNER + g * SSD_STATE:SSD_INNER + (g + 1) * SSD_STATE]
            c_row = xbc_ref[j:j + 1, SSD_INNER + SSD_BC + g * SSD_STATE:SSD_INNER + SSD_BC + (g + 1) * SSD_STATE]
            c_rows = jnp.broadcast_to(c_row, (LANES, SSD_STATE)).astype(BF16)
            parts = []
            for k in range(SSD_HPG):
                h = g * SSD_HPG + k
                hs = slice(h * SSD_HEAD_DIM, (h + 1) * SSD_HEAD_DIM)
                hn = st_ref[0, j, hs, :] * da_ref[b, h] + xb_ref[hs, :] * b_row
                so_ref[0, j, hs, :] = hn
                parts.append(hn.astype(BF16))
            y = _dot_nt(jnp.concatenate(parts, axis=0), c_rows)
            rows = slice(g * SSD_GW, (g + 1) * SSD_GW)
            yr_ref[rows, :] = jnp.where(lane_g == j, y, yr_ref[rows, :])
    lane = lax.broadcasted_iota(jnp.int32, yt_ref.shape, 1)
    mine = (lane >= i * bb) & (lane < (i + 1) * bb)
    yt_ref[...] = jnp.where(mine, pltpu.roll(yr_ref[...], i * bb, 1), yt_ref[...])


def _ssd_sample_state(da, state_all, xdt, xbc, prev_out, *, layer_idx, bb):
    n = state_all.shape[1]
    has_prev = prev_out is not None
    slab = lambda i, s: (layer_idx, i, 0, 0)
    in_specs = [pl.BlockSpec((1, bb, SSD_INNER, SSD_STATE), slab),
                pl.BlockSpec((n, SSD_INNER), lambda i, s: (0, 0)),
                pl.BlockSpec((bb, SSD_CONV_CH), lambda i, s: (i, 0))]
    operands = [da, state_all, xdt, xbc]
    if has_prev:
        in_specs.append(pl.BlockSpec(memory_space=pl.ANY))
        operands.append(prev_out)
    grid_spec = pltpu.PrefetchScalarGridSpec(
        num_scalar_prefetch=1,
        grid=(n // bb,),
        in_specs=in_specs,
        out_specs=[pl.BlockSpec((1, bb, SSD_INNER, SSD_STATE), slab),
                   pl.BlockSpec((SSD_INNER, n), lambda i, s: (0, 0))],
        scratch_shapes=[pltpu.VMEM((SSD_INNER, n), F32),
                        pltpu.VMEM((SSD_INNER, n), F32),
                        pltpu.VMEM((SSD_INNER, SSD_STATE), F32),
                        pltpu.VMEM((SSD_INNER, n), F32)],
    )
    return pl.pallas_call(
        functools.partial(_ssd_sample_state_kernel, bb=bb, has_prev=has_prev),
        grid_spec=grid_spec,
        out_shape=[jax.ShapeDtypeStruct(state_all.shape, F32),
                   jax.ShapeDtypeStruct((SSD_INNER, n), F32)],
        input_output_aliases={len(operands) - 1: 0} if has_prev else {},
        compiler_params=_cparams(1),
        name="ssd_sample_state",
    )(*operands)


def _ssd_sample_post_kernel(yt_ref, xbc_ref, uzx_ref, dskip_ref, ng_ref, o_ref):
    y = yt_ref[...].T + dskip_ref[...] * xbc_ref[:, 0:SSD_INNER]
    y = y * _silu(uzx_ref[:, 0:SSD_INNER])
    for g in range(SSD_GROUPS):
        cols = slice(g * SSD_GW, (g + 1) * SSD_GW)
        o_ref[:, cols] = _rms(y[:, cols], ng_ref[:, cols])


def _ssd_sample_post(yt, xbc, uzx, dskip, ng):
    n = xbc.shape[0]
    const = lambda i: (0, 0)
    return pl.pallas_call(
        _ssd_sample_post_kernel,
        grid=(1,),
        in_specs=[pl.BlockSpec((SSD_INNER, n), const),
                  pl.BlockSpec((n, SSD_CONV_CH), const),
                  pl.BlockSpec((n, SSD_ZX), const),
                  pl.BlockSpec((1, SSD_INNER), const),
                  pl.BlockSpec((1, SSD_INNER), const)],
        out_specs=pl.BlockSpec((n, SSD_INNER), const),
        out_shape=jax.ShapeDtypeStruct((n, SSD_INNER), F32),
        compiler_params=_cparams(1),
        name="ssd_sample_post",
    )(yt, xbc, uzx, dskip, ng)


def _pad_lanes(v):
    return jnp.pad(v.astype(F32), (0, LANES - v.shape[0]))[None, :]


def _head_expand(v):
    return jnp.repeat(v.astype(F32), SSD_HEAD_DIM)[None, :]


def _prep_weights(P):
    stacks = {k: P[k].astype(BF16) for k in ("mlp_w_up", "mlp_w_down", "ab_w_in", "ab_w_out", "ssd_w_in",
                                             "ssd_w_out")}
    ssd_w_dt = jnp.pad(P["ssd_w_in"][:, :, SSD_ZX:], ((0, 0), (0, 0), (0, LANES - SSD_HEADS))).astype(BF16)
    W = []
    for layer in range(4):
        i = layer // 2
        g = P["norm_g"][layer]
        w = {"g": [g[k][None, :] for k in range(4)],
             "w_up": (stacks["mlp_w_up"], layer), "w_down": (stacks["mlp_w_down"], layer)}
        if layer % 2 == 0:
            w.update(w_in=(stacks["ab_w_in"], i), w_out=(stacks["ab_w_out"], i),
                     dw=P["conf_dw_w"][i], db=P["conf_dw_b"][i][None, :],
                     lng=P["conf_ln_g"][i][None, :], lnb=P["conf_ln_b"][i][None, :],
                     sinks=P["attn_sinks"][i])
        else:
            w.update(w_zx=(stacks["ssd_w_in"], i), w_dt=(ssd_w_dt, i), w_out=(stacks["ssd_w_out"], i),
                     cw=P["ssd_conv_w"][i], cb=P["ssd_conv_b"][i][None, :],
                     dtb=_pad_lanes(P["ssd_dt_bias"][i]), alog=_pad_lanes(P["ssd_a_log"][i]),
                     dskip=_head_expand(P["ssd_d"][i]), ng=P["ssd_norm_g"][i][None, :])
        W.append(w)
    return W


def _trunk_prompt(x3, W):
    bsz, t, d = x3.shape
    x = x3.reshape(bsz * t, d)
    tri = jnp.tril(jnp.ones((SSD_CHUNK, SSD_CHUNK), F32)).astype(BF16)
    win_k, win_v, conf_rows, ssm_states, ssd_rows = [], [], [], [], []
    tq = 512
    for layer in range(4):
        w = W[layer]
        g = w["g"]
        if layer % 2 == 0:
            x, tails, kv = _ab_prompt(x, g[0], w["w_in"], w["sinks"], w["dw"], w["db"], w["lng"], w["lnb"],
                                      w["w_out"], g[1], bsz=bsz, t=t, tq=tq)
            assert t >= WINDOW == ATT_BLOCK
            kv = kv.reshape(bsz, t // tq, ATT_BLOCK, 2 * KV_W)[:, -1]
            win_k.append(kv[:, :, :KV_W].reshape(bsz, WINDOW, N_KV_HEADS, HEAD_DIM))
            win_v.append(kv[:, :, KV_W:].reshape(bsz, WINDOW, N_KV_HEADS, HEAD_DIM))
            tails = tails.reshape(bsz, t // tq, CONV_HALO, CONF_CH)
            conf_rows.append(tails[:, -1, CONV_HALO - (CONF_KERNEL - 1):, :])
            x = _mlp(x, g[2], w["w_up"], w["w_down"], g[3], tm=512)
        else:
            y, state, tail = _ssd_prompt(x, g[0], w["w_zx"], w["w_dt"], w["cw"], w["cb"], w["dtb"], w["alog"],
                                         w["dskip"], w["ng"], tri, bsz=bsz, t=t)
            x = _proj_mlp(y, w["w_out"], x, g[1], g[2], w["w_up"], w["w_down"], g[3], tm=512)
            ssm_states.append(state.reshape(bsz, SSD_HEADS, SSD_HEAD_DIM, SSD_STATE))
            ssd_rows.append(tail.reshape(bsz, SSD_HALO, SSD_CONV_CH)[:, SSD_HALO - (SSD_CONV - 1):, :])
    return (x.reshape(bsz, t, d), jnp.stack(win_k), jnp.stack(win_v), jnp.stack(conf_rows),
            jnp.stack(ssm_states), jnp.stack(ssd_rows))


def _trunk_sample(x3, W, cache_k, cache_v, conf_hist, ssm, ssd_hist):
    n, t, d = x3.shape
    x = x3.reshape(n, d)
    expand = jnp.repeat(jnp.eye(LANES, SSD_HEADS, dtype=F32), SSD_HEAD_DIM, axis=1).astype(BF16)
    win_k, win_v, conf_rows, ssd_rows = [], [], [], []
    ssm_out = None
    head_group = jnp.arange(N_Q_HEADS) // Q_PER_KV
    for layer in range(4):
        w = W[layer]
        g = w["g"]
        i = layer // 2
        if layer % 2 == 0:
            u = _norm_matmul(x, g[0], w["w_in"], n=AB_IN, tm=n, tn=AB_IN // 2)
            c_out, a = _conf_sample(u, (conf_hist, i), w["dw"], w["db"], w["lng"], w["lnb"], bb=32)
            k_off = GLU_W + Q_W
            q = u[:, GLU_W:k_off].reshape(n, N_Q_HEADS, HEAD_DIM)
            k_new = u[:, k_off:k_off + KV_W]
            v_new = u[:, k_off + KV_W:]
            lane_group = jnp.arange(KV_W) // HEAD_DIM
            qm = jnp.where(head_group[None, :, None] == lane_group[None, None, :],
                           jnp.tile(q, (1, 1, N_KV_HEADS)), 0.0)
            o = _attn_sample(qm, (cache_k, i), (cache_v, i), k_new[:, None, :], v_new[:, None, :],
                             w["sinks"][:, None], bb=32)
            o = o.reshape(n, N_Q_HEADS, N_KV_HEADS, HEAD_DIM)
            a_out = jnp.take_along_axis(o, head_group[None, :, None, None], axis=2).reshape(n, Q_W)
            x = _proj_norm_res(jnp.concatenate([c_out, a_out], axis=-1), w["w_out"], x, g[1], tm=n)
            win_k.append(k_new.reshape(n, 1, N_KV_HEADS, HEAD_DIM))
            win_v.append(v_new.reshape(n, 1, N_KV_HEADS, HEAD_DIM))
            conf_rows.append(a[:, None, :])
        else:
            uzx = _norm_matmul(x, g[0], w["w_zx"], n=SSD_ZX, tm=n, tn=1024)
            udt = _norm_matmul(x, g[0], w["w_dt"], n=LANES, tm=n, tn=LANES)
            xbc, xdt, da = _ssd_sample_pre(uzx, udt, (ssd_hist, i), w["cw"], w["cb"], w["dtb"], w["alog"], expand,
                                           bb=32)
            ssm_out, yt = _ssd_sample_state(da, ssm.reshape(ssm.shape[0], n, SSD_INNER, SSD_STATE), xdt, xbc,
                                            ssm_out, layer_idx=i, bb=8)
            y = _ssd_sample_post(yt, xbc, uzx, w["dskip"], w["ng"])
            x = _proj_norm_res(y, w["w_out"], x, g[1], tm=n)
            ssd_rows.append(uzx[:, None, SSD_INNER:])
        x = _mlp(x, g[2], w["w_up"], w["w_down"], g[3], tm=n)
    return (x.reshape(n, t, d), jnp.stack(win_k), jnp.stack(win_v), jnp.stack(conf_rows),
            ssm_out.reshape(ssm.shape), jnp.stack(ssd_rows))


def kernel(x_prompt, x_sample, cache_win_k, cache_win_v, state_conf_conv, state_ssm, state_ssd_conv,
           norm_g, ab_w_in, conf_dw_w, conf_dw_b, conf_ln_g, conf_ln_b, attn_sinks, ab_w_out,
           ssd_w_in, ssd_conv_w, ssd_conv_b, ssd_dt_bias, ssd_a_log, ssd_d, ssd_norm_g, ssd_w_out,
           mlp_w_up, mlp_w_down):
    P = {"norm_g": norm_g, "ab_w_in": ab_w_in, "conf_dw_w": conf_dw_w, "conf_dw_b": conf_dw_b,
         "conf_ln_g": conf_ln_g, "conf_ln_b": conf_ln_b, "attn_sinks": attn_sinks, "ab_w_out": ab_w_out,
         "ssd_w_in": ssd_w_in, "ssd_conv_w": ssd_conv_w, "ssd_conv_b": ssd_conv_b, "ssd_dt_bias": ssd_dt_bias,
         "ssd_a_log": ssd_a_log, "ssd_d": ssd_d, "ssd_norm_g": ssd_norm_g, "ssd_w_out": ssd_w_out,
         "mlp_w_up": mlp_w_up, "mlp_w_down": mlp_w_down}
    W = _prep_weights(P)
    y_p, wk_p, wv_p, cc_p, ssm_p, sc_p = _trunk_prompt(x_prompt, W)
    cache_k = cache_win_k.reshape(cache_win_k.shape[:3] + (KV_W,))
    cache_v = cache_win_v.reshape(cache_win_v.shape[:3] + (KV_W,))
    y_s, wk_s, wv_s, cc_s, ssm_s, sc_s = _trunk_sample(x_sample, W, cache_k, cache_v,
                                                       state_conf_conv, state_ssm, state_ssd_conv)
    return (y_p, y_s, wk_p, wv_p, wk_s, wv_s, cc_p, cc_s, ssm_p, ssm_s, sc_p, sc_s)
```

```python
import functools

import jax
import jax.numpy as jnp
from jax import lax
from jax.experimental import pallas as pl
from jax.experimental.pallas import tpu as pltpu

F32 = jnp.float32
BF16 = jnp.bfloat16

D_MODEL = 1024
CONF_CH = 512
CONF_KERNEL = 31
HEAD_DIM = 64
N_Q_HEADS = 8
N_KV_HEADS = 2
Q_PER_KV = N_Q_HEADS // N_KV_HEADS
WINDOW = 128
ATT_BLOCK = 128
ATT_SCALE = HEAD_DIM ** -0.5
Q_W = N_Q_HEADS * HEAD_DIM
KV_W = N_KV_HEADS * HEAD_DIM
GLU_W = 2 * CONF_CH
AB_IN = GLU_W + Q_W + 2 * KV_W
SSD_INNER = 2048
SSD_HEAD_DIM = 64
SSD_HEADS = 32
SSD_GROUPS = 4
SSD_HPG = SSD_HEADS // SSD_GROUPS
SSD_STATE = 128
SSD_CONV = 4
SSD_BC = SSD_GROUPS * SSD_STATE
SSD_CONV_CH = SSD_INNER + 2 * SSD_BC
SSD_ZX = SSD_INNER + SSD_CONV_CH
SSD_GW = SSD_INNER // SSD_GROUPS
SSD_CHUNK = 128
D_FF = 4096
RMS_EPS = 1e-6
LN_EPS = 1e-5

LANES = 128
SUBLANES = 8
VMEM_LIMIT = 56 * 1024 * 1024

NEG_INF = float("-inf")


def _cparams(n_axes):
    return pltpu.CompilerParams(dimension_semantics=("arbitrary",) * n_axes,
                                vmem_limit_bytes=VMEM_LIMIT)


def _layer_spec(stacked, block, index_map, **kw):
    layer = stacked[1]
    return pl.BlockSpec((None,) + tuple(block), lambda *g: (layer,) + tuple(index_map(*g)), **kw)


def _rms(x, g):
    return x * lax.rsqrt(jnp.mean(x * x, axis=-1, keepdims=True) + RMS_EPS) * g


def _sigmoid(x):
    return 1.0 / (1.0 + jnp.exp(-x))


def _silu(x):
    return x * _sigmoid(x)


def _log1p(e):
    u = 1.0 + e
    tiny = u == 1.0
    return jnp.where(tiny, e, jnp.log(u) * (e / jnp.where(tiny, 1.0, u - 1.0)))


def _softplus(x):
    return jnp.maximum(x, 0.0) + _log1p(jnp.exp(-jnp.abs(x)))


def _dot(a, b):
    return jnp.dot(a, b, preferred_element_type=F32)


def _dot_nt(a, b):
    return lax.dot_general(a, b, (((1,), (1,)), ((), ())), preferred_element_type=F32)


def _dot_tn(a, b):
    return lax.dot_general(a, b, (((0,), (0,)), ((), ())), preferred_element_type=F32)


def _split3(x):
    hi = x.astype(BF16)
    r1 = x - hi.astype(F32)
    mid = r1.astype(BF16)
    lo = (r1 - mid.astype(F32)).astype(BF16)
    return hi, mid, lo


def _dot_sel_left(sel, x):
    hi, mid, lo = _split3(x)
    return _dot(sel, hi) + _dot(sel, mid) + _dot(sel, lo)


def _dot_sel_right(x, sel):
    hi, mid, lo = _split3(x)
    return _dot(hi, sel) + _dot(mid, sel) + _dot(lo, sel)


def _norm_matmul_kernel(x_ref, g_ref, w_ref, o_ref, hn_ref):
    @pl.when(pl.program_id(1) == 0)
    def _():
        hn_ref[...] = _rms(x_ref[...], g_ref[...]).astype(BF16)

    o_ref[...] = _dot(hn_ref[...], w_ref[...])


def _norm_matmul(x, g, w, *, n, tm, tn):
    m, d = x.shape
    return pl.pallas_call(
        _norm_matmul_kernel,
        grid=(m // tm, n // tn),
        in_specs=[pl.BlockSpec((tm, d), lambda i, j: (i, 0)),
                  pl.BlockSpec((1, d), lambda i, j: (0, 0)),
                  _layer_spec(w, (d, tn), lambda i, j: (0, j))],
        out_specs=pl.BlockSpec((tm, tn), lambda i, j: (i, j)),
        out_shape=jax.ShapeDtypeStruct((m, n), F32),
        scratch_shapes=[pltpu.VMEM((tm, d), BF16)],
        compiler_params=_cparams(2),
        name="norm_matmul",
    )(x, g, w[0])


def _proj_norm_res_kernel(a_ref, w_ref, x_ref, g_ref, o_ref):
    y = _dot(a_ref[...].astype(BF16), w_ref[...])
    o_ref[...] = x_ref[...] + _rms(y, g_ref[...])


def _proj_norm_res(a, w, x, g, *, tm):
    m, k = a.shape
    d = x.shape[1]
    return pl.pallas_call(
        _proj_norm_res_kernel,
        grid=(m // tm,),
        in_specs=[pl.BlockSpec((tm, k), lambda i: (i, 0)),
                  _layer_spec(w, (k, d), lambda i: (0, 0)),
                  pl.BlockSpec((tm, d), lambda i: (i, 0)),
                  pl.BlockSpec((1, d), lambda i: (0, 0))],
        out_specs=pl.BlockSpec((tm, d), lambda i: (i, 0)),
        out_shape=jax.ShapeDtypeStruct((m, d), F32),
        compiler_params=_cparams(1),
        name="proj_norm_res",
    )(a, w[0], x, g)


def _mlp_block(x, gpre_ref, wup_ref, wdn_ref, gpost_ref, tf):
    h = _rms(x, gpre_ref[...]).astype(BF16)
    f = None
    for c in range(D_FF // tf):
        a = jnp.maximum(_dot(h, wup_ref[:, c * tf:(c + 1) * tf]), 0.0)
        part = _dot((a * a).astype(BF16), wdn_ref[c * tf:(c + 1) * tf, :])
        f = part if f is None else f + part
    return x + _rms(f, gpost_ref[...])


def _mlp_kernel(x_ref, gpre_ref, wup_ref, wdn_ref, gpost_ref, o_ref, *, tf):
    o_ref[...] = _mlp_block(x_ref[...], gpre_ref, wup_ref, wdn_ref, gpost_ref, tf)


def _proj_mlp_kernel(a_ref, wo_ref, x_ref, gmix_ref, gpre_ref, wup_ref, wdn_ref, gpost_ref, o_ref, *, tf):
    x_mid = x_ref[...] + _rms(_dot(a_ref[...].astype(BF16), wo_ref[...]), gmix_ref[...])
    o_ref[...] = _mlp_block(x_mid, gpre_ref, wup_ref, wdn_ref, gpost_ref, tf)


def _mlp(x, gpre, wup, wdn, gpost, *, tm, tf=1024):
    m, d = x.shape
    return pl.pallas_call(
        functools.partial(_mlp_kernel, tf=tf),
        grid=(m // tm,),
        in_specs=[pl.BlockSpec((tm, d), lambda i: (i, 0)),
                  pl.BlockSpec((1, d), lambda i: (0, 0)),
                  _layer_spec(wup, (d, D_FF), lambda i: (0, 0), pipeline_mode=pl.Buffered(1)),
                  _layer_spec(wdn, (D_FF, d), lambda i: (0, 0), pipeline_mode=pl.Buffered(1)),
                  pl.BlockSpec((1, d), lambda i: (0, 0))],
        out_specs=pl.BlockSpec((tm, d), lambda i: (i, 0)),
        out_shape=jax.ShapeDtypeStruct((m, d), F32),
        compiler_params=_cparams(1),
        name="mlp",
    )(x, gpre, wup[0], wdn[0], gpost)


def _proj_mlp(a, wo, x, gmix, gpre, wup, wdn, gpost, *, tm, tf=1024):
    m, d = x.shape
    k = a.shape[1]
    row = lambda i: (i, 0)
    const = lambda i: (0, 0)
    resident = pl.Buffered(1)
    return pl.pallas_call(
        functools.partial(_proj_mlp_kernel, tf=tf),
        grid=(m // tm,),
        in_specs=[pl.BlockSpec((tm, k), row),
                  _layer_spec(wo, (k, d), const, pipeline_mode=resident),
                  pl.BlockSpec((tm, d), row),
                  pl.BlockSpec((1, d), const),
                  pl.BlockSpec((1, d), const),
                  _layer_spec(wup, (d, D_FF), const, pipeline_mode=resident),
                  _layer_spec(wdn, (D_FF, d), const, pipeline_mode=resident),
                  pl.BlockSpec((1, d), const)],
        out_specs=pl.BlockSpec((tm, d), row),
        out_shape=jax.ShapeDtypeStruct((m, d), F32),
        compiler_params=_cparams(1),
        name="proj_mlp",
    )(a, wo[0], x, gmix, gpre, wup[0], wdn[0], gpost)


CONV_HALO = 32
CONV_ROWS = 64
AB_PROJ_COLS = 256


def _glu(u):
    return u[:, :CONF_CH] * _sigmoid(u[:, CONF_CH:GLU_W])


def _ab_prompt_kernel(sink_ref, x_ref, gin_ref, win_ref, dw_ref, db_ref, lng_ref, lnb_ref,
                      wout_ref, g_ref, o_ref, tail_ref, kvtail_ref,
                      u_ref, ukv_prev_ref, ap_ref, sh_ref, mix_ref, s_ref, p_ref, *, tq):
    i = pl.program_id(1)
    has_prev = i > 0
    n_ap = CONV_HALO + tq
    k_off = GLU_W + Q_W

    @pl.when(i == 0)
    def _():
        ap_ref[tq:n_ap, :] = jnp.zeros((CONV_HALO, CONF_CH), F32)
        u_ref[tq - ATT_BLOCK:tq, k_off:AB_IN] = jnp.zeros((ATT_BLOCK, 2 * KV_W), F32)

    ap_ref[0:CONV_HALO, :] = ap_ref[tq:n_ap, :]
    ukv_prev_ref[...] = u_ref[tq - ATT_BLOCK:tq, k_off:AB_IN]

    hn = _rms(x_ref[...], gin_ref[...]).astype(BF16)
    for cc in range(AB_IN // AB_PROJ_COLS):
        cols = slice(cc * AB_PROJ_COLS, (cc + 1) * AB_PROJ_COLS)
        u_ref[:, cols] = _dot(hn, win_ref[:, cols])
    kvtail_ref[...] = u_ref[tq - ATT_BLOCK:tq, k_off:AB_IN]

    ap_ref[CONV_HALO:n_ap, :] = _glu(u_ref[:, 0:GLU_W])
    ap_ref[n_ap:n_ap + SUBLANES, :] = jnp.zeros((SUBLANES, CONF_CH), F32)
    tail_ref[...] = ap_ref[tq:tq + CONV_HALO, :]
    for r in range(1, SUBLANES):
        sh_ref[r - 1] = ap_ref[r:r + n_ap, :]

    first_tap = CONV_HALO - (CONF_KERNEL - 1)
    for rc in range(tq // CONV_ROWS):
        acc = jnp.broadcast_to(db_ref[...], (CONV_ROWS, CONF_CH))
        for k in range(CONF_KERNEL):
            tile, r = divmod(first_tap + k, SUBLANES)
            start = rc * CONV_ROWS + tile * SUBLANES
            if r == 0:
                rows = ap_ref[start:start + CONV_ROWS, :]
            else:
                rows = sh_ref[r - 1, start:start + CONV_ROWS, :]
            acc = acc + rows * dw_ref[k:k + 1, :]
        mu = jnp.mean(acc, axis=-1, keepdims=True)
        xc = acc - mu
        var = jnp.mean(xc * xc, axis=-1, keepdims=True)
        ln = xc * lax.rsqrt(var + LN_EPS) * lng_ref[...] + lnb_ref[...]
        mix_ref[rc * CONV_ROWS:(rc + 1) * CONV_ROWS, 0:CONF_CH] = _silu(ln).astype(BF16)

    n_blk = tq // ATT_BLOCK
    n_keys = 2 * ATT_BLOCK
    grp_rows = Q_PER_KV * ATT_BLOCK
    blk_rows = N_Q_HEADS * ATT_BLOCK
    k_off = GLU_W + Q_W
    v_off = k_off + KV_W

    def kv_rows(j, off):
        rows = slice(j * ATT_BLOCK, (j + 1) * ATT_BLOCK)
        if j == 0:
            prev = ukv_prev_ref[:, off - k_off:off - k_off + KV_W]
        else:
            prev = u_ref[(j - 1) * ATT_BLOCK:j * ATT_BLOCK, off:off + KV_W]
        return jnp.concatenate([prev, u_ref[rows, off:off + KV_W]], axis=0)

    col0 = lax.broadcasted_iota(jnp.int32, (grp_rows, n_keys), 1) == 0
    for j in range(n_blk):
        rows = slice(j * ATT_BLOCK, (j + 1) * ATT_BLOCK)
        q = (u_ref[rows, GLU_W:GLU_W + Q_W] * ATT_SCALE).astype(BF16)
        kk = kv_rows(j, k_off).astype(BF16)
        for g in range(N_KV_HEADS):
            heads = [g * Q_PER_KV + hh for hh in range(Q_PER_KV)]
            qg = jnp.concatenate([q[:, h * HEAD_DIM:(h + 1) * HEAD_DIM] for h in heads], axis=0)
            s = _dot_nt(qg, kk[:, g * HEAD_DIM:(g + 1) * HEAD_DIM])
            sink = jnp.concatenate([jnp.full((ATT_BLOCK, 1), sink_ref[h], F32) for h in heads], axis=0)
            base = j * blk_rows + g * grp_rows
            s_ref[base:base + grp_rows, :] = jnp.where(col0, sink, s)

    qi = lax.broadcasted_iota(jnp.int32, (ATT_BLOCK, n_keys), 0)
    kc = lax.broadcasted_iota(jnp.int32, (ATT_BLOCK, n_keys), 1)
    visible = ((kc > qi) & (kc <= qi + WINDOW)) | (kc == 0)
    visible_first = visible & ((kc >= ATT_BLOCK) | (kc == 0) | has_prev)
    bias = jnp.where(visible, 0.0, NEG_INF)
    bias_first = jnp.where(visible_first, 0.0, NEG_INF)
    for lo, hi, bb in ((0, blk_rows, bias_first), (blk_rows, n_blk * blk_rows, bias)):
        sm = s_ref[lo:hi, :].reshape((hi - lo) // ATT_BLOCK, ATT_BLOCK, n_keys) + bb
        m = jnp.max(sm, axis=-1, keepdims=True)
        p_ref[lo:hi, :] = jnp.exp(sm - m).astype(BF16).reshape(hi - lo, n_keys)

    vrow = lax.broadcasted_iota(jnp.int32, (n_keys, HEAD_DIM), 0)
    ones = jnp.ones((n_keys, HEAD_DIM), BF16)
    lo_half = lax.broadcasted_iota(jnp.int32, (ATT_BLOCK, 2 * HEAD_DIM), 1) < HEAD_DIM
    for j in range(n_blk):
        rows = slice(j * ATT_BLOCK, (j + 1) * ATT_BLOCK)
        vv = kv_rows(j, v_off)
        for g in range(N_KV_HEADS):
            vg = jnp.where(vrow > 0, vv[:, g * HEAD_DIM:(g + 1) * HEAD_DIM], 0.0).astype(BF16)
            v_ext = (jnp.concatenate([vg, ones], axis=1), jnp.concatenate([ones, vg], axis=1))
            for pr in range(Q_PER_KV // 2):
                normed = []
                for par in range(2):
                    base = j * blk_rows + g * grp_rows + (2 * pr + par) * ATT_BLOCK
                    oe = _dot(p_ref[base:base + ATT_BLOCK, :], v_ext[par])
                    normed.append(oe / pltpu.roll(oe, HEAD_DIM, 1))
                c0 = CONF_CH + (g * Q_PER_KV + 2 * pr) * HEAD_DIM
                mix_ref[rows, c0:c0 + 2 * HEAD_DIM] = jnp.where(lo_half, normed[0], normed[1]).astype(BF16)

    y = _dot(mix_ref[...], wout_ref[...])
    o_ref[...] = x_ref[...] + _rms(y, g_ref[...])


def _ab_prompt(x, gin, win, sinks, dw, db, lng, lnb, wout, g, *, bsz, t, tq):
    nt = t // tq
    m = bsz * t
    d = x.shape[1]

    def cur(b, i, s):
        return (b * nt + i, 0)

    const = lambda b, i, s: (0, 0)
    resident = pl.Buffered(1)
    grid_spec = pltpu.PrefetchScalarGridSpec(
        num_scalar_prefetch=1,
        grid=(bsz, nt),
        in_specs=[pl.BlockSpec((tq, d), cur),
                  pl.BlockSpec((1, d), const),
                  _layer_spec(win, (d, AB_IN), const, pipeline_mode=resident),
                  pl.BlockSpec(dw.shape, const),
                  pl.BlockSpec((1, CONF_CH), const),
                  pl.BlockSpec((1, CONF_CH), const),
                  pl.BlockSpec((1, CONF_CH), const),
                  _layer_spec(wout, (CONF_CH + Q_W, d), const, pipeline_mode=resident),
                  pl.BlockSpec((1, d), const)],
        out_specs=[pl.BlockSpec((tq, d), cur),
                   pl.BlockSpec((CONV_HALO, CONF_CH), cur),
                   pl.BlockSpec((ATT_BLOCK, 2 * KV_W), cur)],
        scratch_shapes=[pltpu.VMEM((tq, AB_IN), F32),
                        pltpu.VMEM((ATT_BLOCK, 2 * KV_W), F32),
                        pltpu.VMEM((CONV_HALO + tq + SUBLANES, CONF_CH), F32),
                        pltpu.VMEM((SUBLANES - 1, CONV_HALO + tq, CONF_CH), F32),
                        pltpu.VMEM((tq, CONF_CH + Q_W), BF16),
                        pltpu.VMEM((tq * N_Q_HEADS, 2 * ATT_BLOCK), F32),
                        pltpu.VMEM((tq * N_Q_HEADS, 2 * ATT_BLOCK), BF16)],
    )
    return pl.pallas_call(
        functools.partial(_ab_prompt_kernel, tq=tq),
        grid_spec=grid_spec,
        out_shape=[jax.ShapeDtypeStruct((m, d), F32),
                   jax.ShapeDtypeStruct((bsz * nt * CONV_HALO, CONF_CH), F32),
                   jax.ShapeDtypeStruct((bsz * nt * ATT_BLOCK, 2 * KV_W), F32)],
        compiler_params=_cparams(2),
        name="ab_prompt",
    )(sinks, x, gin, win[0], dw, db, lng, lnb, wout[0], g)


SSD_HALO = SUBLANES
SSD_CONV_COLS = 256
SSD_PROJ_COLS = 512
SSD_STEP_CHUNKS = 2


def _ssd_conv_masks(ch):
    sub = lax.broadcasted_iota(jnp.int32, (SUBLANES, ch), 0)
    return [sub < back for back in range(SSD_CONV)]


def _ssd_conv_silu(halo, cur, w, b, wrapped):
    rows, ch = cur.shape
    tiles = [halo] + [cur[r:r + SUBLANES, :] for r in range(0, rows, SUBLANES)]
    acc = b + cur * w[SSD_CONV - 1:SSD_CONV, :]
    for back in range(1, SSD_CONV):
        rolled = [pltpu.roll(v, back, 0) for v in tiles]
        shifted = jnp.concatenate([jnp.where(wrapped[back], rolled[i], rolled[i + 1])
                                   for i in range(len(tiles) - 1)], axis=0)
        acc = acc + shifted * w[SSD_CONV - 1 - back:SSD_CONV - back, :]
    return _silu(acc)


def _ssd_prompt_kernel(x0_ref, xn_ref, g_ref, wzx_ref, wdt_ref, cw_ref, cb_ref, dtb_ref, alog_ref, dskip_ref,
                       ng_ref, tri_ref, y_ref, state_ref, tail_ref,
                       ua_ref, dta_ref, ub_ref, dtb2_ref, hn_ref, halo_ref, xbc_ref, h_ref, xdec_ref, *, n_steps):
    c = pl.program_id(1)
    n = pl.program_id(0) * n_steps + c

    def in_proj_pieces(x_ref, u_ref, udt_ref):
        def norm():
            hn_ref[...] = _rms(x_ref[...], g_ref[...]).astype(BF16)

        def zx(cc):
            cols = slice(cc * SSD_PROJ_COLS, (cc + 1) * SSD_PROJ_COLS)
            u_ref[:, cols] = _dot(hn_ref[...], wzx_ref[:, cols])

        def dt():
            udt_ref[...] = _dot(hn_ref[...], wdt_ref[...])

        return [norm] + [functools.partial(zx, cc) for cc in range(SSD_ZX // SSD_PROJ_COLS)] + [dt]

    @pl.when(n == 0)
    def _():
        for piece in in_proj_pieces(x0_ref, ua_ref, dta_ref):
            piece()

    @pl.when(c == 0)
    def _():
        halo_ref[...] = jnp.zeros_like(halo_ref)
        h_ref[...] = jnp.zeros_like(h_ref)

    refs = (cw_ref, cb_ref, dtb_ref, alog_ref, dskip_ref, ng_ref, tri_ref, y_ref,
            halo_ref, xbc_ref, h_ref, xdec_ref)

    def step(u_ref, udt_ref, fillers):
        n_stages = SSD_STEP_CHUNKS * (SSD_CONV_CH // SSD_CONV_COLS + SSD_HEADS // 2)
        n_fill = len(fillers)
        stage = [0]

        def fill():
            stage[0] += 1
            while fillers and n_fill - len(fillers) < (stage[0] * n_fill + n_stages - 1) // n_stages:
                fillers.pop(0)()

        for k in range(SSD_STEP_CHUNKS):
            _ssd_chunk(u_ref, udt_ref, k * SSD_CHUNK, *refs, fill=fill)
        assert not fillers

    @pl.when(n % 2 == 0)
    def _():
        step(ua_ref, dta_ref, in_proj_pieces(xn_ref, ub_ref, dtb2_ref))

    @pl.when(n % 2 == 1)
    def _():
        step(ub_ref, dtb2_ref, in_proj_pieces(xn_ref, ua_ref, dta_ref))

    @pl.when(c == n_steps - 1)
    def _():
        state_ref[0] = h_ref[...]
        tail_ref[...] = halo_ref[...]


def _ssd_chunk(u_ref, udt_ref, r0, cw_ref, cb_ref, dtb_ref, alog_ref, dskip_ref, ng_ref, tri_ref, y_ref,
               halo_ref, xbc_ref, h_ref, xdec_ref, *, fill):
    L = SSD_CHUNK
    crows = slice(r0, r0 + L)
    wrapped = _ssd_conv_masks(SSD_CONV_COLS)
    for cc in range(SSD_CONV_CH // SSD_CONV_COLS):
        fill()
        cols = slice(cc * SSD_CONV_COLS, (cc + 1) * SSD_CONV_COLS)
        ucols = slice(SSD_INNER + cc * SSD_CONV_COLS, SSD_INNER + (cc + 1) * SSD_CONV_COLS)
        xbc_ref[:, cols] = _ssd_conv_silu(halo_ref[:, cols], u_ref[crows, ucols], cw_ref[:, cols], cb_ref[:, cols],
                                          wrapped)
    halo_ref[...] = u_ref[r0 + L - SSD_HALO:r0 + L, SSD_INNER:SSD_ZX]

    dt = _softplus(udt_ref[crows, :] + dtb_ref[...])
    a = -jnp.exp(alog_ref[...])
    cs = _dot_sel_left(tri_ref[...], dt * a)
    cs_t = cs.T

    row = lax.broadcasted_iota(jnp.int32, (L, LANES), 0)
    lane = lax.broadcasted_iota(jnp.int32, (L, LANES), 1)
    causal = row >= lane
    lo_half = lane < SSD_HEAD_DIM

    def bc(v, h):
        return jnp.broadcast_to(v[:, h:h + 1], (L, LANES))

    for g in range(SSD_GROUPS):
        b_g = xbc_ref[:, SSD_INNER + g * SSD_STATE:SSD_INNER + (g + 1) * SSD_STATE].astype(BF16)
        c_g = xbc_ref[:, SSD_INNER + SSD_BC + g * SSD_STATE:SSD_INNER + SSD_BC + (g + 1) * SSD_STATE].astype(BF16)
        hrows = slice(g * SSD_GW, (g + 1) * SSD_GW)
        cb = _dot_nt(c_g, b_g)
        yoff_raw = _dot_nt(c_g, h_ref[hrows, :].astype(BF16))
        end_decay = []
        gated = []
        for pr in range(SSD_HPG // 2):
            fill()
            h1 = g * SSD_HPG + 2 * pr
            h2 = h1 + 1
            cols = slice(h1 * SSD_HEAD_DIM, (h1 + 2) * SSD_HEAD_DIM)
            a1 = bc(cs, h1)
            a2 = bc(cs, h2)
            m1 = cb * jnp.exp(jnp.where(causal, a1 - cs_t[h1:h1 + 1, :], NEG_INF))
            m2 = cb * jnp.exp(jnp.where(causal, a2 - cs_t[h2:h2 + 1, :], NEG_INF))
            xs = xbc_ref[:, cols]
            xc = xs * jnp.where(lo_half, bc(dt, h1), bc(dt, h2))
            lhs = jnp.concatenate([m1, m2], axis=1).astype(BF16)
            rhs = jnp.concatenate([jnp.where(lo_half, xc, 0.0), jnp.where(lo_half, 0.0, xc)], axis=0).astype(BF16)
            y_diag = _dot(lhs, rhs)
            csp = jnp.where(lo_half, a1, a2)
            y_off = jnp.exp(csp) * yoff_raw[:, 2 * pr * SSD_HEAD_DIM:(2 * pr + 2) * SSD_HEAD_DIM]
            xdec_ref[:, cols] = (xc * jnp.exp(csp[L - 1:L, :] - csp)).astype(BF16)
            end_decay += [jnp.exp(a1[L - 1:L, :]), jnp.exp(a2[L - 1:L, :])]
            y = y_diag + y_off + dskip_ref[:, cols] * xs
            gated.append(y * _silu(u_ref[crows, cols]))
        ss = sum(jnp.sum(v * v, axis=-1, keepdims=True) for v in gated)
        scale = lax.rsqrt(ss * (1.0 / SSD_GW) + RMS_EPS)
        for pr in range(SSD_HPG // 2):
            cols = slice(g * SSD_GW + pr * LANES, g * SSD_GW + (pr + 1) * LANES)
            y_ref[crows, cols] = (gated[pr] * scale * ng_ref[:, cols]).astype(y_ref.dtype)
        st = _dot_tn(xdec_ref[:, hrows], b_g)
        for k in range(SSD_HPG):
            hs = slice(g * SSD_GW + k * SSD_HEAD_DIM, g * SSD_GW + (k + 1) * SSD_HEAD_DIM)
            h_ref[hs, :] = h_ref[hs, :] * end_decay[k] + st[k * SSD_HEAD_DIM:(k + 1) * SSD_HEAD_DIM, :]


def _ssd_prompt(x, g, wzx, wdt, cw, cb, dtb, alog, dskip, ng, tri, *, bsz, t):
    rows = SSD_STEP_CHUNKS * SSD_CHUNK
    nc = t // rows
    L = SSD_CHUNK
    d = x.shape[1]
    last = bsz * nc - 1
    cur = lambda b, c: (b * nc + c, 0)
    nxt = lambda b, c: (jnp.minimum(b * nc + c + 1, last), 0)
    const = lambda b, c: (0, 0)
    resident = pl.Buffered(1)
    return pl.pallas_call(
        functools.partial(_ssd_prompt_kernel, n_steps=nc),
        grid=(bsz, nc),
        in_specs=[pl.BlockSpec((rows, d), const),
                  pl.BlockSpec((rows, d), nxt),
                  pl.BlockSpec((1, d), const),
                  _layer_spec(wzx, (d, SSD_ZX), const, pipeline_mode=resident),
                  _layer_spec(wdt, (d, LANES), const, pipeline_mode=resident),
                  pl.BlockSpec((SSD_CONV, SSD_CONV_CH), const),
                  pl.BlockSpec((1, SSD_CONV_CH), const),
                  pl.BlockSpec((1, LANES), const),
                  pl.BlockSpec((1, LANES), const),
                  pl.BlockSpec((1, SSD_INNER), const),
                  pl.BlockSpec((1, SSD_INNER), const),
                  pl.BlockSpec((L, L), const)],
        out_specs=[pl.BlockSpec((rows, SSD_INNER), cur),
                   pl.BlockSpec((1, SSD_INNER, SSD_STATE), lambda b, c: (b, 0, 0)),
                   pl.BlockSpec((SSD_HALO, SSD_CONV_CH), lambda b, c: (b, 0))],
        out_shape=[jax.ShapeDtypeStruct((bsz * t, SSD_INNER), BF16),
                   jax.ShapeDtypeStruct((bsz, SSD_INNER, SSD_STATE), F32),
                   jax.ShapeDtypeStruct((bsz * SSD_HALO, SSD_CONV_CH), F32)],
        scratch_shapes=[pltpu.VMEM((rows, SSD_ZX), F32),
                        pltpu.VMEM((rows, LANES), F32),
                        pltpu.VMEM((rows, SSD_ZX), F32),
                        pltpu.VMEM((rows, LANES), F32),
                        pltpu.VMEM((rows, d), BF16),
                        pltpu.VMEM((SSD_HALO, SSD_CONV_CH), F32),
                        pltpu.VMEM((L, SSD_CONV_CH), F32),
                        pltpu.VMEM((SSD_INNER, SSD_STATE), F32),
                        pltpu.VMEM((L, SSD_INNER), BF16)],
        compiler_params=_cparams(2),
        name="ssd_prompt",
    )(x, x, g, wzx[0], wdt[0], cw, cb, dtb, alog, dskip, ng, tri)


def _conf_sample_kernel(u_ref, hist_ref, dw_ref, db_ref, lng_ref, lnb_ref, c_ref, a_ref):
    a = _glu(u_ref[:, 0:GLU_W])
    a_ref[...] = a
    nh = CONF_KERNEL - 1
    acc = db_ref[...] + a * dw_ref[nh:nh + 1, :]
    for k in range(nh):
        acc = acc + hist_ref[k] * dw_ref[k:k + 1, :]
    mu = jnp.mean(acc, axis=-1, keepdims=True)
    xc = acc - mu
    var = jnp.mean(xc * xc, axis=-1, keepdims=True)
    c_ref[...] = _silu(xc * lax.rsqrt(var + LN_EPS) * lng_ref[...] + lnb_ref[...])


def _conf_sample(u, hist, dw, db, lng, lnb, *, bb):
    n = u.shape[0]
    nh = CONF_KERNEL - 1
    const = lambda i: (0, 0)
    return pl.pallas_call(
        _conf_sample_kernel,
        grid=(n // bb,),
        in_specs=[pl.BlockSpec((bb, AB_IN), lambda i: (i, 0)),
                  _layer_spec(hist, (nh, bb, CONF_CH), lambda i: (0, i, 0)),
                  pl.BlockSpec(dw.shape, const),
                  pl.BlockSpec((1, CONF_CH), const),
                  pl.BlockSpec((1, CONF_CH), const),
                  pl.BlockSpec((1, CONF_CH), const)],
        out_specs=[pl.BlockSpec((bb, CONF_CH), lambda i: (i, 0)),
                   pl.BlockSpec((bb, CONF_CH), lambda i: (i, 0))],
        out_shape=[jax.ShapeDtypeStruct((n, CONF_CH), F32),
                   jax.ShapeDtypeStruct((n, CONF_CH), F32)],
        compiler_params=_cparams(1),
        name="conf_sample",
    )(u, hist[0], dw, db, lng, lnb)


def _attn_sample_kernel(qm_ref, kc_ref, vc_ref, kn_ref, vn_ref, sink_ref, o_ref, *, wc):
    q = qm_ref[...] * ATT_SCALE
    s = jnp.einsum("bhc,bcs->bhs", q.astype(BF16), kc_ref[...].astype(BF16), preferred_element_type=F32)
    kpos = lax.broadcasted_iota(jnp.int32, s.shape, 2)
    s = jnp.where((kpos <= wc) & (wc - kpos < WINDOW), s, NEG_INF)
    s_new = jnp.sum(q * kn_ref[...], axis=-1, keepdims=True)
    sink = sink_ref[...][None]
    m = jnp.maximum(jnp.maximum(jnp.max(s, axis=-1, keepdims=True), s_new), sink)
    p = jnp.exp(s - m)
    p_new = jnp.exp(s_new - m)
    den = jnp.sum(p, axis=-1, keepdims=True) + p_new + jnp.exp(sink - m)
    o = jnp.einsum("bhs,bcs->bhc", p.astype(BF16), vc_ref[...].astype(BF16), preferred_element_type=F32)
    o_ref[...] = (o + p_new * vn_ref[...]) / den


def _attn_sample(qm, kc, vc, kn, vn, sink, *, bb):
    _, n, _, wc = kc[0].shape
    blk3 = lambda i: (i, 0, 0)
    return pl.pallas_call(
        functools.partial(_attn_sample_kernel, wc=wc),
        grid=(n // bb,),
        in_specs=[pl.BlockSpec((bb, N_Q_HEADS, KV_W), blk3),
                  _layer_spec(kc, (bb, KV_W, wc), blk3),
                  _layer_spec(vc, (bb, KV_W, wc), blk3),
                  pl.BlockSpec((bb, 1, KV_W), blk3),
                  pl.BlockSpec((bb, 1, KV_W), blk3),
                  pl.BlockSpec((N_Q_HEADS, 1), lambda i: (0, 0))],
        out_specs=pl.BlockSpec((bb, N_Q_HEADS, KV_W), blk3),
        out_shape=jax.ShapeDtypeStruct((n, N_Q_HEADS, KV_W), F32),
        compiler_params=_cparams(1),
        name="attn_sample",
    )(qm, kc[0], vc[0], kn, vn, sink)


def _ssd_sample_pre_kernel(uzx_ref, udt_ref, hist_ref, cw_ref, cb_ref, dtb_ref, alog_ref, expand_ref,
                           xbc_ref, xdt_ref, da_ref):
    nh = SSD_CONV - 1
    xbc = uzx_ref[:, SSD_INNER:SSD_ZX]
    acc = cb_ref[...] + xbc * cw_ref[nh:nh + 1, :]
    for k in range(nh):
        acc = acc + hist_ref[k] * cw_ref[k:k + 1, :]
    conv = _silu(acc)
    xbc_ref[...] = conv
    dt = _softplus(udt_ref[...] + dtb_ref[...])
    da_ref[...] = jnp.exp(dt * (-jnp.exp(alog_ref[...])))
    xdt_ref[...] = conv[:, 0:SSD_INNER] * _dot_sel_right(dt, expand_ref[...])


def _ssd_sample_pre(uzx, udt, hist, cw, cb, dtb, alog, expand, *, bb):
    n = uzx.shape[0]
    const = lambda i: (0, 0)
    row = lambda i: (i, 0)
    return pl.pallas_call(
        _ssd_sample_pre_kernel,
        grid=(n // bb,),
        in_specs=[pl.BlockSpec((bb, SSD_ZX), row),
                  pl.BlockSpec((bb, LANES), row),
                  _layer_spec(hist, (SSD_CONV - 1, bb, SSD_CONV_CH), lambda i: (0, i, 0)),
                  pl.BlockSpec((SSD_CONV, SSD_CONV_CH), const),
                  pl.BlockSpec((1, SSD_CONV_CH), const),
                  pl.BlockSpec((1, LANES), const),
                  pl.BlockSpec((1, LANES), const),
                  pl.BlockSpec((LANES, SSD_INNER), const)],
        out_specs=[pl.BlockSpec((bb, SSD_CONV_CH), row),
                   pl.BlockSpec((bb, SSD_INNER), row),
                   pl.BlockSpec((bb, LANES), row)],
        out_shape=[jax.ShapeDtypeStruct((n, SSD_CONV_CH), F32),
                   jax.ShapeDtypeStruct((n, SSD_INNER), F32),
                   jax.ShapeDtypeStruct((n, LANES), F32)],
        compiler_params=_cparams(1),
        name="ssd_sample_pre",
    )(uzx, udt, hist[0], cw, cb, dtb, alog, expand)


def _ssd_sample_state_kernel(da_ref, st_ref, xdt_ref, xbc_ref, *rest, bb, has_prev):
    so_ref, yt_ref, xt_ref, xr_ref, xb_ref, yr_ref = rest[1:] if has_prev else rest
    i = pl.program_id(0)

    @pl.when(i == 0)
    def _():
        xt_ref[...] = xdt_ref[...].T
        yt_ref[...] = jnp.zeros_like(yt_ref)
        yr_ref[...] = jnp.zeros_like(yr_ref)

    xr_ref[...] = pltpu.roll(xt_ref[...], (LANES - i * bb) % LANES, 1)
    lane_g = lax.broadcasted_iota(jnp.int32, (SSD_GW, LANES), 1)
    for j in range(bb):
        b = i * bb + j
        xb_ref[...] = jnp.broadcast_to(xr_ref[:, j:j + 1], xb_ref.shape)
        for g in range(SSD_GROUPS):
            b_row = xbc_ref[j:j + 1, SSD_INNER + g * SSD_STATE:SSD_INNER + (g + 1) * SSD_STATE]
            c_row = xbc_ref[j:j + 1, SSD_INNER + SSD_BC + g * SSD_STATE:SSD_INNER + SSD_BC + (g + 1) * SSD_STATE]
            c_rows = jnp.broadcast_to(c_row, (LANES, SSD_STATE)).astype(BF16)
            parts = []
            for k in range(SSD_HPG):
                h = g * SSD_HPG + k
                hs = slice(h * SSD_HEAD_DIM, (h + 1) * SSD_HEAD_DIM)
                hn = st_ref[0, j, hs, :] * da_ref[b, h] + xb_ref[hs, :] * b_row
                so_ref[0, j, hs, :] = hn
                parts.append(hn.astype(BF16))
            y = _dot_nt(jnp.concatenate(parts, axis=0), c_rows)
            rows = slice(g * SSD_GW, (g + 1) * SSD_GW)
            yr_ref[rows, :] = jnp.where(lane_g == j, y, yr_ref[rows, :])
    lane = lax.broadcasted_iota(jnp.int32, yt_ref.shape, 1)
    mine = (lane >= i * bb) & (lane < (i + 1) * bb)
    yt_ref[...] = jnp.where(mine, pltpu.roll(yr_ref[...], i * bb, 1), yt_ref[...])


def _ssd_sample_state(da, state_all, xdt, xbc, prev_out, *, layer_idx, bb):
    n = state_all.shape[1]
    has_prev = prev_out is not None
    slab = lambda i, s: (layer_idx, i, 0, 0)
    in_specs = [pl.BlockSpec((1, bb, SSD_INNER, SSD_STATE), slab),
                pl.BlockSpec((n, SSD_INNER), lambda i, s: (0, 0)),
                pl.BlockSpec((bb, SSD_CONV_CH), lambda i, s: (i, 0))]
    operands = [da, state_all, xdt, xbc]
    if has_prev:
        in_specs.append(pl.BlockSpec(memory_space=pl.ANY))
        operands.append(prev_out)
    grid_spec = pltpu.PrefetchScalarGridSpec(
        num_scalar_prefetch=1,
        grid=(n // bb,),
        in_specs=in_specs,
        out_specs=[pl.BlockSpec((1, bb, SSD_INNER, SSD_STATE), slab),
                   pl.BlockSpec((SSD_INNER, n), lambda i, s: (0, 0))],
        scratch_shapes=[pltpu.VMEM((SSD_INNER, n), F32),
                        pltpu.VMEM((SSD_INNER, n), F32),
                        pltpu.VMEM((SSD_INNER, SSD_STATE), F32),
                        pltpu.VMEM((SSD_INNER, n), F32)],
    )
    return pl.pallas_call(
        functools.partial(_ssd_sample_state_kernel, bb=bb, has_prev=has_prev),
        grid_spec=grid_spec,
        out_shape=[jax.ShapeDtypeStruct(state_all.shape, F32),
                   jax.ShapeDtypeStruct((SSD_INNER, n), F32)],
        input_output_aliases={len(operands) - 1: 0} if has_prev else {},
        compiler_params=_cparams(1),
        name="ssd_sample_state",
    )(*operands)


def _ssd_sample_post_kernel(yt_ref, xbc_ref, uzx_ref, dskip_ref, ng_ref, o_ref):
    y = yt_ref[...].T + dskip_ref[...] * xbc_ref[:, 0:SSD_INNER]
    y = y * _silu(uzx_ref[:, 0:SSD_INNER])
    for g in range(SSD_GROUPS):
        cols = slice(g * SSD_GW, (g + 1) * SSD_GW)
        o_ref[:, cols] = _rms(y[:, cols], ng_ref[:, cols])


def _ssd_sample_post(yt, xbc, uzx, dskip, ng):
    n = xbc.shape[0]
    const = lambda i: (0, 0)
    return pl.pallas_call(
        _ssd_sample_post_kernel,
        grid=(1,),
        in_specs=[pl.BlockSpec((SSD_INNER, n), const),
                  pl.BlockSpec((n, SSD_CONV_CH), const),
                  pl.BlockSpec((n, SSD_ZX), const),
                  pl.BlockSpec((1, SSD_INNER), const),
                  pl.BlockSpec((1, SSD_INNER), const)],
        out_specs=pl.BlockSpec((n, SSD_INNER), const),
        out_shape=jax.ShapeDtypeStruct((n, SSD_INNER), F32),
        compiler_params=_cparams(1),
        name="ssd_sample_post",
    )(yt, xbc, uzx, dskip, ng)


def _pad_lanes(v):
    return jnp.pad(v.astype(F32), (0, LANES - v.shape[0]))[None, :]


def _head_expand(v):
    return jnp.repeat(v.astype(F32), SSD_HEAD_DIM)[None, :]


def _prep_weights(P):
    stacks = {k: P[k].astype(BF16) for k in ("mlp_w_up", "mlp_w_down", "ab_w_in", "ab_w_out", "ssd_w_in",
                                             "ssd_w_out")}
    ssd_w_dt = jnp.pad(stacks["ssd_w_in"][:, :, SSD_ZX:], ((0, 0), (0, 0), (0, LANES - SSD_HEADS)))
    W = []
    for layer in range(4):
        i = layer // 2
        g = P["norm_g"][layer]
        w = {"g": [g[k][None, :] for k in range(4)],
             "w_up": (stacks["mlp_w_up"], layer), "w_down": (stacks["mlp_w_down"], layer)}
        if layer % 2 == 0:
            w.update(w_in=(stacks["ab_w_in"], i), w_out=(stacks["ab_w_out"], i),
                     dw=P["conf_dw_w"][i], db=P["conf_dw_b"][i][None, :],
                     lng=P["conf_ln_g"][i][None, :], lnb=P["conf_ln_b"][i][None, :],
                     sinks=P["attn_sinks"][i])
        else:
            w.update(w_zx=(stacks["ssd_w_in"], i), w_dt=(ssd_w_dt, i), w_out=(stacks["ssd_w_out"], i),
                     cw=P["ssd_conv_w"][i], cb=P["ssd_conv_b"][i][None, :],
                     dtb=_pad_lanes(P["ssd_dt_bias"][i]), alog=_pad_lanes(P["ssd_a_log"][i]),
                     dskip=_head_expand(P["ssd_d"][i]), ng=P["ssd_norm_g"][i][None, :])
        W.append(w)
    return W


def _trunk_prompt(x3, W):
    bsz, t, d = x3.shape
    x = x3.reshape(bsz * t, d)
    tri = jnp.tril(jnp.ones((SSD_CHUNK, SSD_CHUNK), F32)).astype(BF16)
    win_k, win_v, conf_rows, ssm_states, ssd_rows = [], [], [], [], []
    tq = 512
    for layer in range(4):
        w = W[layer]
        g = w["g"]
        if layer % 2 == 0:
            x, tails, kv = _ab_prompt(x, g[0], w["w_in"], w["sinks"], w["dw"], w["db"], w["lng"], w["lnb"],
                                      w["w_out"], g[1], bsz=bsz, t=t, tq=tq)
            assert t >= WINDOW == ATT_BLOCK
            kv = kv.reshape(bsz, t // tq, ATT_BLOCK, 2 * KV_W)[:, -1]
            win_k.append(kv[:, :, :KV_W].reshape(bsz, WINDOW, N_KV_HEADS, HEAD_DIM))
            win_v.append(kv[:, :, KV_W:].reshape(bsz, WINDOW, N_KV_HEADS, HEAD_DIM))
            tails = tails.reshape(bsz, t // tq, CONV_HALO, CONF_CH)
            conf_rows.append(tails[:, -1, CONV_HALO - (CONF_KERNEL - 1):, :])
            x = _mlp(x, g[2], w["w_up"], w["w_down"], g[3], tm=1024)
        else:
            y, state, tail = _ssd_prompt(x, g[0], w["w_zx"], w["w_dt"], w["cw"], w["cb"], w["dtb"], w["alog"],
                                         w["dskip"], w["ng"], tri, bsz=bsz, t=t)
            x = _proj_mlp(y, w["w_out"], x, g[1], g[2], w["w_up"], w["w_down"], g[3], tm=512)
            ssm_states.append(state.reshape(bsz, SSD_HEADS, SSD_HEAD_DIM, SSD_STATE))
            ssd_rows.append(tail.reshape(bsz, SSD_HALO, SSD_CONV_CH)[:, SSD_HALO - (SSD_CONV - 1):, :])
    return (x.reshape(bsz, t, d), jnp.stack(win_k), jnp.stack(win_v), jnp.stack(conf_rows),
            jnp.stack(ssm_states), jnp.stack(ssd_rows))


def _trunk_sample(x3, W, cache_k, cache_v, conf_hist, ssm, ssd_hist):
    n, t, d = x3.shape
    x = x3.reshape(n, d)
    expand = jnp.repeat(jnp.eye(LANES, SSD_HEADS, dtype=F32), SSD_HEAD_DIM, axis=1).astype(BF16)
    win_k, win_v, conf_rows, ssd_rows = [], [], [], []
    ssm_out = None
    head_group = jnp.arange(N_Q_HEADS) // Q_PER_KV
    for layer in range(4):
        w = W[layer]
        g = w["g"]
        i = layer // 2
        if layer % 2 == 0:
            u = _norm_matmul(x, g[0], w["w_in"], n=AB_IN, tm=n, tn=AB_IN // 2)
            c_out, a = _conf_sample(u, (conf_hist, i), w["dw"], w["db"], w["lng"], w["lnb"], bb=32)
            k_off = GLU_W + Q_W
            q = u[:, GLU_W:k_off].reshape(n, N_Q_HEADS, HEAD_DIM)
            k_new = u[:, k_off:k_off + KV_W]
            v_new = u[:, k_off + KV_W:]
            lane_group = jnp.arange(KV_W) // HEAD_DIM
            qm = jnp.where(head_group[None, :, None] == lane_group[None, None, :],
                           jnp.tile(q, (1, 1, N_KV_HEADS)), 0.0)
            o = _attn_sample(qm, (cache_k, i), (cache_v, i), k_new[:, None, :], v_new[:, None, :],
                             w["sinks"][:, None], bb=32)
            o = o.reshape(n, N_Q_HEADS, N_KV_HEADS, HEAD_DIM)
            a_out = jnp.take_along_axis(o, head_group[None, :, None, None], axis=2).reshape(n, Q_W)
            x = _proj_norm_res(jnp.concatenate([c_out, a_out], axis=-1), w["w_out"], x, g[1], tm=n)
            win_k.append(k_new.reshape(n, 1, N_KV_HEADS, HEAD_DIM))
            win_v.append(v_new.reshape(n, 1, N_KV_HEADS, HEAD_DIM))
            conf_rows.append(a[:, None, :])
        else:
            uzx = _norm_matmul(x, g[0], w["w_zx"], n=SSD_ZX, tm=n, tn=1024)
            udt = _norm_matmul(x, g[0], w["w_dt"], n=LANES, tm=n, tn=LANES)
            xbc, xdt, da = _ssd_sample_pre(uzx, udt, (ssd_hist, i), w["cw"], w["cb"], w["dtb"], w["alog"], expand,
                                           bb=32)
            ssm_out, yt = _ssd_sample_state(da, ssm.reshape(ssm.shape[0], n, SSD_INNER, SSD_STATE), xdt, xbc,
                                            ssm_out, layer_idx=i, bb=8)
            y = _ssd_sample_post(yt, xbc, uzx, w["dskip"], w["ng"])
            x = _proj_norm_res(y, w["w_out"], x, g[1], tm=n)
            ssd_rows.append(uzx[:, None, SSD_INNER:])
        x = _mlp(x, g[2], w["w_up"], w["w_down"], g[3], tm=n)
    return (x.reshape(n, t, d), jnp.stack(win_k), jnp.stack(win_v), jnp.stack(conf_rows),
            ssm_out.reshape(ssm.shape), jnp.stack(ssd_rows))


def kernel(x_prompt, x_sample, cache_win_k, cache_win_v, state_conf_conv, state_ssm, state_ssd_conv,
           norm_g, ab_w_in, conf_dw_w, conf_dw_b, conf_ln_g, conf_ln_b, attn_sinks, ab_w_out,
           ssd_w_in, ssd_conv_w, ssd_conv_b, ssd_dt_bias, ssd_a_log, ssd_d, ssd_norm_g, ssd_w_out,
           mlp_w_up, mlp_w_down):
    P = {"norm_g": norm_g, "ab_w_in": ab_w_in, "conf_dw_w": conf_dw_w, "conf_dw_b": conf_dw_b,
         "conf_ln_g": conf_ln_g, "conf_ln_b": conf_ln_b, "attn_sinks": attn_sinks, "ab_w_out": ab_w_out,
         "ssd_w_in": ssd_w_in, "ssd_conv_w": ssd_conv_w, "ssd_conv_b": ssd_conv_b, "ssd_dt_bias": ssd_dt_bias,
         "ssd_a_log": ssd_a_log, "ssd_d": ssd_d, "ssd_norm_g": ssd_norm_g, "ssd_w_out": ssd_w_out,
         "mlp_w_up": mlp_w_up, "mlp_w_down": mlp_w_down}
    W = _prep_weights(P)
    y_p, wk_p, wv_p, cc_p, ssm_p, sc_p = _trunk_prompt(x_prompt, W)
    cache_k = jnp.swapaxes(cache_win_k.reshape(cache_win_k.shape[:3] + (KV_W,)), 2, 3)
    cache_v = jnp.swapaxes(cache_win_v.reshape(cache_win_v.shape[:3] + (KV_W,)), 2, 3)
    y_s, wk_s, wv_s, cc_s, ssm_s, sc_s = _trunk_sample(x_sample, W, cache_k, cache_v,
                                                       jnp.swapaxes(state_conf_conv, 1, 2), state_ssm,
                                                       jnp.swapaxes(state_ssd_conv, 1, 2))
    return (y_p, y_s, wk_p, wv_p, wk_s, wv_s, cc_p, cc_s, ssm_p, ssm_s, sc_p, sc_s)
```

```python
import functools

import jax
import jax.numpy as jnp
from jax import lax
from jax.experimental import pallas as pl
from jax.experimental.pallas import tpu as pltpu

F32 = jnp.float32
BF16 = jnp.bfloat16

D_MODEL = 1024
CONF_CH = 512
CONF_KERNEL = 31
HEAD_DIM = 64
N_Q_HEADS = 8
N_KV_HEADS = 2
Q_PER_KV = N_Q_HEADS // N_KV_HEADS
WINDOW = 128
ATT_BLOCK = 128
ATT_SCALE = HEAD_DIM ** -0.5
Q_W = N_Q_HEADS * HEAD_DIM
KV_W = N_KV_HEADS * HEAD_DIM
GLU_W = 2 * CONF_CH
AB_IN = GLU_W + Q_W + 2 * KV_W
SSD_INNER = 2048
SSD_HEAD_DIM = 64
SSD_HEADS = 32
SSD_GROUPS = 4
SSD_HPG = SSD_HEADS // SSD_GROUPS
SSD_STATE = 128
SSD_CONV = 4
SSD_BC = SSD_GROUPS * SSD_STATE
SSD_CONV_CH = SSD_INNER + 2 * SSD_BC
SSD_ZX = SSD_INNER + SSD_CONV_CH
SSD_GW = SSD_INNER // SSD_GROUPS
SSD_CHUNK = 128
D_FF = 4096
RMS_EPS = 1e-6
LN_EPS = 1e-5

LANES = 128
SUBLANES = 8
VMEM_LIMIT = 56 * 1024 * 1024

NEG_INF = float("-inf")


def _cparams(n_axes):
    return pltpu.CompilerParams(dimension_semantics=("arbitrary",) * n_axes,
                                vmem_limit_bytes=VMEM_LIMIT)


def _layer_spec(stacked, block, index_map, **kw):
    layer = stacked[1]
    return pl.BlockSpec((None,) + tuple(block), lambda *g: (layer,) + tuple(index_map(*g)), **kw)


def _rms(x, g):
    return x * lax.rsqrt(jnp.mean(x * x, axis=-1, keepdims=True) + RMS_EPS) * g


def _sigmoid(x):
    return 1.0 / (1.0 + jnp.exp(-x))


def _silu(x):
    return x * _sigmoid(x)


def _log1p(e):
    u = 1.0 + e
    tiny = u == 1.0
    return jnp.where(tiny, e, jnp.log(u) * (e / jnp.where(tiny, 1.0, u - 1.0)))


def _softplus(x):
    return jnp.maximum(x, 0.0) + _log1p(jnp.exp(-jnp.abs(x)))


def _dot(a, b):
    return jnp.dot(a, b, preferred_element_type=F32)


def _dot_nt(a, b):
    return lax.dot_general(a, b, (((1,), (1,)), ((), ())), preferred_element_type=F32)


def _dot_tn(a, b):
    return lax.dot_general(a, b, (((0,), (0,)), ((), ())), preferred_element_type=F32)


def _split3(x):
    hi = x.astype(BF16)
    r1 = x - hi.astype(F32)
    mid = r1.astype(BF16)
    lo = (r1 - mid.astype(F32)).astype(BF16)
    return hi, mid, lo


def _dot_sel_left(sel, x):
    hi, mid, lo = _split3(x)
    return _dot(sel, hi) + _dot(sel, mid) + _dot(sel, lo)


def _dot_sel_right(x, sel):
    hi, mid, lo = _split3(x)
    return _dot(hi, sel) + _dot(mid, sel) + _dot(lo, sel)


def _norm_matmul_kernel(x_ref, g_ref, w_ref, o_ref, hn_ref):
    @pl.when(pl.program_id(1) == 0)
    def _():
        hn_ref[...] = _rms(x_ref[...], g_ref[...]).astype(BF16)

    o_ref[...] = _dot(hn_ref[...], w_ref[...])


def _norm_matmul(x, g, w, *, n, tm, tn):
    m, d = x.shape
    return pl.pallas_call(
        _norm_matmul_kernel,
        grid=(m // tm, n // tn),
        in_specs=[pl.BlockSpec((tm, d), lambda i, j: (i, 0)),
                  pl.BlockSpec((1, d), lambda i, j: (0, 0)),
                  _layer_spec(w, (d, tn), lambda i, j: (0, j))],
        out_specs=pl.BlockSpec((tm, tn), lambda i, j: (i, j)),
        out_shape=jax.ShapeDtypeStruct((m, n), F32),
        scratch_shapes=[pltpu.VMEM((tm, d), BF16)],
        compiler_params=_cparams(2),
        name="norm_matmul",
    )(x, g, w[0])


def _proj_norm_res_kernel(a_ref, w_ref, x_ref, g_ref, o_ref):
    y = _dot(a_ref[...].astype(BF16), w_ref[...])
    o_ref[...] = x_ref[...] + _rms(y, g_ref[...])


def _proj_norm_res(a, w, x, g, *, tm):
    m, k = a.shape
    d = x.shape[1]
    return pl.pallas_call(
        _proj_norm_res_kernel,
        grid=(m // tm,),
        in_specs=[pl.BlockSpec((tm, k), lambda i: (i, 0)),
                  _layer_spec(w, (k, d), lambda i: (0, 0)),
                  pl.BlockSpec((tm, d), lambda i: (i, 0)),
                  pl.BlockSpec((1, d), lambda i: (0, 0))],
        out_specs=pl.BlockSpec((tm, d), lambda i: (i, 0)),
        out_shape=jax.ShapeDtypeStruct((m, d), F32),
        compiler_params=_cparams(1),
        name="proj_norm_res",
    )(a, w[0], x, g)


def _mlp_block(x, gpre_ref, wup_ref, wdn_ref, gpost_ref, tf):
    h = _rms(x, gpre_ref[...]).astype(BF16)
    f = None
    for c in range(D_FF // tf):
        a = jnp.maximum(_dot(h, wup_ref[:, c * tf:(c + 1) * tf]), 0.0)
        part = _dot((a * a).astype(BF16), wdn_ref[c * tf:(c + 1) * tf, :])
        f = part if f is None else f + part
    return x + _rms(f, gpost_ref[...])


def _mlp_kernel(x_ref, gpre_ref, wup_ref, wdn_ref, gpost_ref, o_ref, *, tf):
    o_ref[...] = _mlp_block(x_ref[...], gpre_ref, wup_ref, wdn_ref, gpost_ref, tf)


def _proj_mlp_kernel(a_ref, wo_ref, x_ref, gmix_ref, gpre_ref, wup_ref, wdn_ref, gpost_ref, o_ref, *, tf):
    x_mid = x_ref[...] + _rms(_dot(a_ref[...].astype(BF16), wo_ref[...]), gmix_ref[...])
    o_ref[...] = _mlp_block(x_mid, gpre_ref, wup_ref, wdn_ref, gpost_ref, tf)


def _mlp(x, gpre, wup, wdn, gpost, *, tm, tf=1024):
    m, d = x.shape
    return pl.pallas_call(
        functools.partial(_mlp_kernel, tf=tf),
        grid=(m // tm,),
        in_specs=[pl.BlockSpec((tm, d), lambda i: (i, 0)),
                  pl.BlockSpec((1, d), lambda i: (0, 0)),
                  _layer_spec(wup, (d, D_FF), lambda i: (0, 0), pipeline_mode=pl.Buffered(1)),
                  _layer_spec(wdn, (D_FF, d), lambda i: (0, 0), pipeline_mode=pl.Buffered(1)),
                  pl.BlockSpec((1, d), lambda i: (0, 0))],
        out_specs=pl.BlockSpec((tm, d), lambda i: (i, 0)),
        out_shape=jax.ShapeDtypeStruct((m, d), F32),
        compiler_params=_cparams(1),
        name="mlp",
    )(x, gpre, wup[0], wdn[0], gpost)


def _proj_mlp(a, wo, x, gmix, gpre, wup, wdn, gpost, *, tm, tf=1024):
    m, d = x.shape
    k = a.shape[1]
    row = lambda i: (i, 0)
    const = lambda i: (0, 0)
    resident = pl.Buffered(1)
    return pl.pallas_call(
        functools.partial(_proj_mlp_kernel, tf=tf),
        grid=(m // tm,),
        in_specs=[pl.BlockSpec((tm, k), row),
                  _layer_spec(wo, (k, d), const, pipeline_mode=resident),
                  pl.BlockSpec((tm, d), row),
                  pl.BlockSpec((1, d), const),
                  pl.BlockSpec((1, d), const),
                  _layer_spec(wup, (d, D_FF), const, pipeline_mode=resident),
                  _layer_spec(wdn, (D_FF, d), const, pipeline_mode=resident),
                  pl.BlockSpec((1, d), const)],
        out_specs=pl.BlockSpec((tm, d), row),
        out_shape=jax.ShapeDtypeStruct((m, d), F32),
        compiler_params=_cparams(1),
        name="proj_mlp",
    )(a, wo[0], x, gmix, gpre, wup[0], wdn[0], gpost)


CONV_HALO = 32
CONV_ROWS = 512
AB_PROJ_COLS = 256


def _glu(u):
    return u[:, :CONF_CH] * _sigmoid(u[:, CONF_CH:GLU_W])


def _ab_prompt_kernel(sink_ref, x_ref, gin_ref, win_ref, dw_ref, db_ref, lng_ref, lnb_ref,
                      wout_ref, g_ref, o_ref, tail_ref, kvtail_ref,
                      u_ref, ukv_prev_ref, ap_ref, sh_ref, mix_ref, s_ref, p_ref, *, tq):
    i = pl.program_id(1)
    has_prev = i > 0
    n_ap = CONV_HALO + tq
    k_off = GLU_W + Q_W

    @pl.when(i == 0)
    def _():
        ap_ref[tq:n_ap, :] = jnp.zeros((CONV_HALO, CONF_CH), F32)
        u_ref[tq - ATT_BLOCK:tq, k_off:AB_IN] = jnp.zeros((ATT_BLOCK, 2 * KV_W), F32)

    ap_ref[0:CONV_HALO, :] = ap_ref[tq:n_ap, :]
    ukv_prev_ref[...] = u_ref[tq - ATT_BLOCK:tq, k_off:AB_IN]

    hn = _rms(x_ref[...], gin_ref[...]).astype(BF16)
    for cc in range(AB_IN // AB_PROJ_COLS):
        cols = slice(cc * AB_PROJ_COLS, (cc + 1) * AB_PROJ_COLS)
        u_ref[:, cols] = _dot(hn, win_ref[:, cols])
    kvtail_ref[...] = u_ref[tq - ATT_BLOCK:tq, k_off:AB_IN]

    ap_ref[CONV_HALO:n_ap, :] = _glu(u_ref[:, 0:GLU_W])
    ap_ref[n_ap:n_ap + SUBLANES, :] = jnp.zeros((SUBLANES, CONF_CH), F32)
    tail_ref[...] = ap_ref[tq:tq + CONV_HALO, :]
    for r in range(1, SUBLANES):
        sh_ref[r - 1] = ap_ref[r:r + n_ap, :]

    first_tap = CONV_HALO - (CONF_KERNEL - 1)
    for rc in range(tq // CONV_ROWS):
        acc = jnp.broadcast_to(db_ref[...], (CONV_ROWS, CONF_CH))
        for k in range(CONF_KERNEL):
            tile, r = divmod(first_tap + k, SUBLANES)
            start = rc * CONV_ROWS + tile * SUBLANES
            if r == 0:
                rows = ap_ref[start:start + CONV_ROWS, :]
            else:
                rows = sh_ref[r - 1, start:start + CONV_ROWS, :]
            acc = acc + rows * dw_ref[k:k + 1, :]
        mu = jnp.mean(acc, axis=-1, keepdims=True)
        xc = acc - mu
        var = jnp.mean(xc * xc, axis=-1, keepdims=True)
        ln = xc * lax.rsqrt(var + LN_EPS) * lng_ref[...] + lnb_ref[...]
        mix_ref[rc * CONV_ROWS:(rc + 1) * CONV_ROWS, 0:CONF_CH] = _silu(ln).astype(BF16)

    n_blk = tq // ATT_BLOCK
    n_keys = 2 * ATT_BLOCK
    grp_rows = Q_PER_KV * ATT_BLOCK
    blk_rows = N_Q_HEADS * ATT_BLOCK
    k_off = GLU_W + Q_W
    v_off = k_off + KV_W

    def kv_rows(j, off):
        rows = slice(j * ATT_BLOCK, (j + 1) * ATT_BLOCK)
        if j == 0:
            prev = ukv_prev_ref[:, off - k_off:off - k_off + KV_W]
        else:
            prev = u_ref[(j - 1) * ATT_BLOCK:j * ATT_BLOCK, off:off + KV_W]
        return jnp.concatenate([prev, u_ref[rows, off:off + KV_W]], axis=0)

    col0 = lax.broadcasted_iota(jnp.int32, (grp_rows, n_keys), 1) == 0
    for j in range(n_blk):
        rows = slice(j * ATT_BLOCK, (j + 1) * ATT_BLOCK)
        q = (u_ref[rows, GLU_W:GLU_W + Q_W] * ATT_SCALE).astype(BF16)
        kk = kv_rows(j, k_off).astype(BF16)
        for g in range(N_KV_HEADS):
            heads = [g * Q_PER_KV + hh for hh in range(Q_PER_KV)]
            qg = jnp.concatenate([q[:, h * HEAD_DIM:(h + 1) * HEAD_DIM] for h in heads], axis=0)
            s = _dot_nt(qg, kk[:, g * HEAD_DIM:(g + 1) * HEAD_DIM])
            sink = jnp.concatenate([jnp.full((ATT_BLOCK, 1), sink_ref[h], F32) for h in heads], axis=0)
            base = j * blk_rows + g * grp_rows
            s_ref[base:base + grp_rows, :] = jnp.where(col0, sink, s)

    qi = lax.broadcasted_iota(jnp.int32, (ATT_BLOCK, n_keys), 0)
    kc = lax.broadcasted_iota(jnp.int32, (ATT_BLOCK, n_keys), 1)
    visible = ((kc > qi) & (kc <= qi + WINDOW)) | (kc == 0)
    visible_first = visible & ((kc >= ATT_BLOCK) | (kc == 0) | has_prev)
    bias = jnp.where(visible, 0.0, NEG_INF)
    bias_first = jnp.where(visible_first, 0.0, NEG_INF)
    for lo, hi, bb in ((0, blk_rows, bias_first), (blk_rows, n_blk * blk_rows, bias)):
        sm = s_ref[lo:hi, :].reshape((hi - lo) // ATT_BLOCK, ATT_BLOCK, n_keys) + bb
        m = jnp.max(sm, axis=-1, keepdims=True)
        p_ref[lo:hi, :] = jnp.exp(sm - m).astype(BF16).reshape(hi - lo, n_keys)

    vrow = lax.broadcasted_iota(jnp.int32, (n_keys, HEAD_DIM), 0)
    ones = jnp.ones((n_keys, HEAD_DIM), BF16)
    lo_half = lax.broadcasted_iota(jnp.int32, (ATT_BLOCK, 2 * HEAD_DIM), 1) < HEAD_DIM
    for j in range(n_blk):
        rows = slice(j * ATT_BLOCK, (j + 1) * ATT_BLOCK)
        vv = kv_rows(j, v_off)
        for g in range(N_KV_HEADS):
            vg = jnp.where(vrow > 0, vv[:, g * HEAD_DIM:(g + 1) * HEAD_DIM], 0.0).astype(BF16)
            v_ext = (jnp.concatenate([vg, ones], axis=1), jnp.concatenate([ones, vg], axis=1))
            for pr in range(Q_PER_KV // 2):
                normed = []
                for par in range(2):
                    base = j * blk_rows + g * grp_rows + (2 * pr + par) * ATT_BLOCK
                    oe = _dot(p_ref[base:base + ATT_BLOCK, :], v_ext[par])
                    normed.append(oe / pltpu.roll(oe, HEAD_DIM, 1))
                c0 = CONF_CH + (g * Q_PER_KV + 2 * pr) * HEAD_DIM
                mix_ref[rows, c0:c0 + 2 * HEAD_DIM] = jnp.where(lo_half, normed[0], normed[1]).astype(BF16)

    y = _dot(mix_ref[...], wout_ref[...])
    o_ref[...] = x_ref[...] + _rms(y, g_ref[...])


def _ab_prompt(x, gin, win, sinks, dw, db, lng, lnb, wout, g, *, bsz, t, tq):
    nt = t // tq
    m = bsz * t
    d = x.shape[1]

    def cur(b, i, s):
        return (b * nt + i, 0)

    const = lambda b, i, s: (0, 0)
    resident = pl.Buffered(1)
    grid_spec = pltpu.PrefetchScalarGridSpec(
        num_scalar_prefetch=1,
        grid=(bsz, nt),
        in_specs=[pl.BlockSpec((tq, d), cur),
                  pl.BlockSpec((1, d), const),
                  _layer_spec(win, (d, AB_IN), const, pipeline_mode=resident),
                  pl.BlockSpec(dw.shape, const),
                  pl.BlockSpec((1, CONF_CH), const),
                  pl.BlockSpec((1, CONF_CH), const),
                  pl.BlockSpec((1, CONF_CH), const),
                  _layer_spec(wout, (CONF_CH + Q_W, d), const, pipeline_mode=resident),
                  pl.BlockSpec((1, d), const)],
        out_specs=[pl.BlockSpec((tq, d), cur),
                   pl.BlockSpec((CONV_HALO, CONF_CH), cur),
                   pl.BlockSpec((ATT_BLOCK, 2 * KV_W), cur)],
        scratch_shapes=[pltpu.VMEM((tq, AB_IN), F32),
                        pltpu.VMEM((ATT_BLOCK, 2 * KV_W), F32),
                        pltpu.VMEM((CONV_HALO + tq + SUBLANES, CONF_CH), F32),
                        pltpu.VMEM((SUBLANES - 1, CONV_HALO + tq, CONF_CH), F32),
                        pltpu.VMEM((tq, CONF_CH + Q_W), BF16),
                        pltpu.VMEM((tq * N_Q_HEADS, 2 * ATT_BLOCK), F32),
                        pltpu.VMEM((tq * N_Q_HEADS, 2 * ATT_BLOCK), BF16)],
    )
    return pl.pallas_call(
        functools.partial(_ab_prompt_kernel, tq=tq),
        grid_spec=grid_spec,
        out_shape=[jax.ShapeDtypeStruct((m, d), F32),
                   jax.ShapeDtypeStruct((bsz * nt * CONV_HALO, CONF_CH), F32),
                   jax.ShapeDtypeStruct((bsz * nt * ATT_BLOCK, 2 * KV_W), F32)],
        compiler_params=_cparams(2),
        name="ab_prompt",
    )(sinks, x, gin, win[0], dw, db, lng, lnb, wout[0], g)


SSD_HALO = SUBLANES
SSD_CONV_COLS = 256
SSD_PROJ_COLS = 1024
SSD_STEP_CHUNKS = 2


def _ssd_conv_masks(ch):
    sub = lax.broadcasted_iota(jnp.int32, (SUBLANES, ch), 0)
    return [sub < back for back in range(SSD_CONV)]


def _ssd_conv_silu(halo, cur, w, b, wrapped):
    rows, ch = cur.shape
    tiles = [halo] + [cur[r:r + SUBLANES, :] for r in range(0, rows, SUBLANES)]
    acc = b + cur * w[SSD_CONV - 1:SSD_CONV, :]
    for back in range(1, SSD_CONV):
        rolled = [pltpu.roll(v, back, 0) for v in tiles]
        shifted = jnp.concatenate([jnp.where(wrapped[back], rolled[i], rolled[i + 1])
                                   for i in range(len(tiles) - 1)], axis=0)
        acc = acc + shifted * w[SSD_CONV - 1 - back:SSD_CONV - back, :]
    return _silu(acc)


def _ssd_prompt_kernel(x0_ref, xn_ref, g_ref, wzx_ref, wdt_ref, cw_ref, cb_ref, dtb_ref, alog_ref, dskip_ref,
                       ng_ref, tri_ref, y_ref, state_ref, tail_ref,
                       ua_ref, dta_ref, ub_ref, dtb2_ref, hn_ref, halo_ref, xbc_ref, h_ref, xdec_ref, *, n_steps):
    c = pl.program_id(1)
    n = pl.program_id(0) * n_steps + c

    def in_proj_pieces(x_ref, u_ref, udt_ref):
        def norm():
            hn_ref[...] = _rms(x_ref[...], g_ref[...]).astype(BF16)

        def zx(cc):
            cols = slice(cc * SSD_PROJ_COLS, (cc + 1) * SSD_PROJ_COLS)
            u_ref[:, cols] = _dot(hn_ref[...], wzx_ref[:, cols])

        def dt():
            udt_ref[...] = _dot(hn_ref[...], wdt_ref[...])

        return [norm] + [functools.partial(zx, cc) for cc in range(SSD_ZX // SSD_PROJ_COLS)] + [dt]

    @pl.when(n == 0)
    def _():
        for piece in in_proj_pieces(x0_ref, ua_ref, dta_ref):
            piece()

    @pl.when(c == 0)
    def _():
        halo_ref[...] = jnp.zeros_like(halo_ref)
        h_ref[...] = jnp.zeros_like(h_ref)

    refs = (cw_ref, cb_ref, dtb_ref, alog_ref, dskip_ref, ng_ref, tri_ref, y_ref,
            halo_ref, xbc_ref, h_ref, xdec_ref)

    def step(u_ref, udt_ref, fillers):
        n_stages = SSD_STEP_CHUNKS * (SSD_CONV_CH // SSD_CONV_COLS + SSD_HEADS // 2)
        n_fill = len(fillers)
        stage = [0]

        def fill():
            stage[0] += 1
            while fillers and n_fill - len(fillers) < (stage[0] * n_fill + n_stages - 1) // n_stages:
                fillers.pop(0)()

        for k in range(SSD_STEP_CHUNKS):
            _ssd_chunk(u_ref, udt_ref, k * SSD_CHUNK, *refs, fill=fill)
        assert not fillers

    @pl.when(n % 2 == 0)
    def _():
        step(ua_ref, dta_ref, in_proj_pieces(xn_ref, ub_ref, dtb2_ref))

    @pl.when(n % 2 == 1)
    def _():
        step(ub_ref, dtb2_ref, in_proj_pieces(xn_ref, ua_ref, dta_ref))

    @pl.when(c == n_steps - 1)
    def _():
        state_ref[0] = h_ref[...]
        tail_ref[...] = halo_ref[...]


def _ssd_chunk(u_ref, udt_ref, r0, cw_ref, cb_ref, dtb_ref, alog_ref, dskip_ref, ng_ref, tri_ref, y_ref,
               halo_ref, xbc_ref, h_ref, xdec_ref, *, fill):
    L = SSD_CHUNK
    crows = slice(r0, r0 + L)
    wrapped = _ssd_conv_masks(SSD_CONV_COLS)
    for cc in range(SSD_CONV_CH // SSD_CONV_COLS):
        fill()
        cols = slice(cc * SSD_CONV_COLS, (cc + 1) * SSD_CONV_COLS)
        ucols = slice(SSD_INNER + cc * SSD_CONV_COLS, SSD_INNER + (cc + 1) * SSD_CONV_COLS)
        xbc_ref[:, cols] = _ssd_conv_silu(halo_ref[:, cols], u_ref[crows, ucols], cw_ref[:, cols], cb_ref[:, cols],
                                          wrapped)
    halo_ref[...] = u_ref[r0 + L - SSD_HALO:r0 + L, SSD_INNER:SSD_ZX]

    dt = _softplus(udt_ref[crows, :] + dtb_ref[...])
    a = -jnp.exp(alog_ref[...])
    cs = _dot_sel_left(tri_ref[...], dt * a)
    cs_t = cs.T

    row = lax.broadcasted_iota(jnp.int32, (L, LANES), 0)
    lane = lax.broadcasted_iota(jnp.int32, (L, LANES), 1)
    causal = row >= lane
    lo_half = lane < SSD_HEAD_DIM

    def bc(v, h):
        return jnp.broadcast_to(v[:, h:h + 1], (L, LANES))

    for g in range(SSD_GROUPS):
        b_g = xbc_ref[:, SSD_INNER + g * SSD_STATE:SSD_INNER + (g + 1) * SSD_STATE].astype(BF16)
        c_g = xbc_ref[:, SSD_INNER + SSD_BC + g * SSD_STATE:SSD_INNER + SSD_BC + (g + 1) * SSD_STATE].astype(BF16)
        hrows = slice(g * SSD_GW, (g + 1) * SSD_GW)
        cb = _dot_nt(c_g, b_g)
        yoff_raw = _dot_nt(c_g, h_ref[hrows, :].astype(BF16))
        end_decay = []
        gated = []
        for pr in range(SSD_HPG // 2):
            fill()
            h1 = g * SSD_HPG + 2 * pr
            h2 = h1 + 1
            cols = slice(h1 * SSD_HEAD_DIM, (h1 + 2) * SSD_HEAD_DIM)
            a1 = bc(cs, h1)
            a2 = bc(cs, h2)
            m1 = cb * jnp.exp(jnp.where(causal, a1 - cs_t[h1:h1 + 1, :], NEG_INF))
            m2 = cb * jnp.exp(jnp.where(causal, a2 - cs_t[h2:h2 + 1, :], NEG_INF))
            xs = xbc_ref[:, cols]
            xc = xs * jnp.where(lo_half, bc(dt, h1), bc(dt, h2))
            lhs = jnp.concatenate([m1, m2], axis=1).astype(BF16)
            rhs = jnp.concatenate([jnp.where(lo_half, xc, 0.0), jnp.where(lo_half, 0.0, xc)], axis=0).astype(BF16)
            y_diag = _dot(lhs, rhs)
            csp = jnp.where(lo_half, a1, a2)
            y_off = jnp.exp(csp) * yoff_raw[:, 2 * pr * SSD_HEAD_DIM:(2 * pr + 2) * SSD_HEAD_DIM]
            xdec_ref[:, cols] = (xc * jnp.exp(csp[L - 1:L, :] - csp)).astype(BF16)
            end_decay += [jnp.exp(a1[L - 1:L, :]), jnp.exp(a2[L - 1:L, :])]
            y = y_diag + y_off + dskip_ref[:, cols] * xs
            gated.append(y * _silu(u_ref[crows, cols]))
        ss = sum(jnp.sum(v * v, axis=-1, keepdims=True) for v in gated)
        scale = lax.rsqrt(ss * (1.0 / SSD_GW) + RMS_EPS)
        for pr in range(SSD_HPG // 2):
            cols = slice(g * SSD_GW + pr * LANES, g * SSD_GW + (pr + 1) * LANES)
            y_ref[crows, cols] = (gated[pr] * scale * ng_ref[:, cols]).astype(y_ref.dtype)
        st = _dot_tn(xdec_ref[:, hrows], b_g)
        for k in range(SSD_HPG):
            hs = slice(g * SSD_GW + k * SSD_HEAD_DIM, g * SSD_GW + (k + 1) * SSD_HEAD_DIM)
            h_ref[hs, :] = h_ref[hs, :] * end_decay[k] + st[k * SSD_HEAD_DIM:(k + 1) * SSD_HEAD_DIM, :]


def _ssd_prompt(x, g, wzx, wdt, cw, cb, dtb, alog, dskip, ng, tri, *, bsz, t):
    rows = SSD_STEP_CHUNKS * SSD_CHUNK
    nc = t // rows
    L = SSD_CHUNK
    d = x.shape[1]
    last = bsz * nc - 1
    cur = lambda b, c: (b * nc + c, 0)
    nxt = lambda b, c: (jnp.minimum(b * nc + c + 1, last), 0)
    const = lambda b, c: (0, 0)
    resident = pl.Buffered(1)
    return pl.pallas_call(
        functools.partial(_ssd_prompt_kernel, n_steps=nc),
        grid=(bsz, nc),
        in_specs=[pl.BlockSpec((rows, d), const),
                  pl.BlockSpec((rows, d), nxt),
                  pl.BlockSpec((1, d), const),
                  _layer_spec(wzx, (d, SSD_ZX), const, pipeline_mode=resident),
                  _layer_spec(wdt, (d, LANES), const, pipeline_mode=resident),
                  pl.BlockSpec((SSD_CONV, SSD_CONV_CH), const),
                  pl.BlockSpec((1, SSD_CONV_CH), const),
                  pl.BlockSpec((1, LANES), const),
                  pl.BlockSpec((1, LANES), const),
                  pl.BlockSpec((1, SSD_INNER), const),
                  pl.BlockSpec((1, SSD_INNER), const),
                  pl.BlockSpec((L, L), const)],
        out_specs=[pl.BlockSpec((rows, SSD_INNER), cur),
                   pl.BlockSpec((1, SSD_INNER, SSD_STATE), lambda b, c: (b, 0, 0)),
                   pl.BlockSpec((SSD_HALO, SSD_CONV_CH), lambda b, c: (b, 0))],
        out_shape=[jax.ShapeDtypeStruct((bsz * t, SSD_INNER), BF16),
                   jax.ShapeDtypeStruct((bsz, SSD_INNER, SSD_STATE), F32),
                   jax.ShapeDtypeStruct((bsz * SSD_HALO, SSD_CONV_CH), F32)],
        scratch_shapes=[pltpu.VMEM((rows, SSD_ZX), F32),
                        pltpu.VMEM((rows, LANES), F32),
                        pltpu.VMEM((rows, SSD_ZX), F32),
                        pltpu.VMEM((rows, LANES), F32),
                        pltpu.VMEM((rows, d), BF16),
                        pltpu.VMEM((SSD_HALO, SSD_CONV_CH), F32),
                        pltpu.VMEM((L, SSD_CONV_CH), F32),
                        pltpu.VMEM((SSD_INNER, SSD_STATE), F32),
                        pltpu.VMEM((L, SSD_INNER), BF16)],
        compiler_params=_cparams(2),
        name="ssd_prompt",
    )(x, x, g, wzx[0], wdt[0], cw, cb, dtb, alog, dskip, ng, tri)


def _conf_sample_kernel(u_ref, hist_ref, dw_ref, db_ref, lng_ref, lnb_ref, c_ref, a_ref):
    a = _glu(u_ref[:, 0:GLU_W])
    a_ref[...] = a
    nh = CONF_KERNEL - 1
    acc = db_ref[...] + a * dw_ref[nh:nh + 1, :]
    for k in range(nh):
        acc = acc + hist_ref[k] * dw_ref[k:k + 1, :]
    mu = jnp.mean(acc, axis=-1, keepdims=True)
    xc = acc - mu
    var = jnp.mean(xc * xc, axis=-1, keepdims=True)
    c_ref[...] = _silu(xc * lax.rsqrt(var + LN_EPS) * lng_ref[...] + lnb_ref[...])


def _conf_sample(u, hist, dw, db, lng, lnb, *, bb):
    n = u.shape[0]
    nh = CONF_KERNEL - 1
    const = lambda i: (0, 0)
    return pl.pallas_call(
        _conf_sample_kernel,
        grid=(n // bb,),
        in_specs=[pl.BlockSpec((bb, AB_IN), lambda i: (i, 0)),
                  _layer_spec(hist, (nh, bb, CONF_CH), lambda i: (0, i, 0)),
                  pl.BlockSpec(dw.shape, const),
                  pl.BlockSpec((1, CONF_CH), const),
                  pl.BlockSpec((1, CONF_CH), const),
                  pl.BlockSpec((1, CONF_CH), const)],
        out_specs=[pl.BlockSpec((bb, CONF_CH), lambda i: (i, 0)),
                   pl.BlockSpec((bb, CONF_CH), lambda i: (i, 0))],
        out_shape=[jax.ShapeDtypeStruct((n, CONF_CH), F32),
                   jax.ShapeDtypeStruct((n, CONF_CH), F32)],
        compiler_params=_cparams(1),
        name="conf_sample",
    )(u, hist[0], dw, db, lng, lnb)


def _attn_sample_kernel(qm_ref, kc_ref, vc_ref, kn_ref, vn_ref, sink_ref, o_ref, *, wc):
    q = qm_ref[...] * ATT_SCALE
    s = jnp.einsum("bhc,bcs->bhs", q.astype(BF16), kc_ref[...].astype(BF16), preferred_element_type=F32)
    kpos = lax.broadcasted_iota(jnp.int32, s.shape, 2)
    s = jnp.where((kpos <= wc) & (wc - kpos < WINDOW), s, NEG_INF)
    s_new = jnp.sum(q * kn_ref[...], axis=-1, keepdims=True)
    sink = sink_ref[...][None]
    m = jnp.maximum(jnp.maximum(jnp.max(s, axis=-1, keepdims=True), s_new), sink)
    p = jnp.exp(s - m)
    p_new = jnp.exp(s_new - m)
    den = jnp.sum(p, axis=-1, keepdims=True) + p_new + jnp.exp(sink - m)
    o = jnp.einsum("bhs,bcs->bhc", p.astype(BF16), vc_ref[...].astype(BF16), preferred_element_type=F32)
    o_ref[...] = (o + p_new * vn_ref[...]) / den


def _attn_sample(qm, kc, vc, kn, vn, sink, *, bb):
    _, n, _, wc = kc[0].shape
    blk3 = lambda i: (i, 0, 0)
    return pl.pallas_call(
        functools.partial(_attn_sample_kernel, wc=wc),
        grid=(n // bb,),
        in_specs=[pl.BlockSpec((bb, N_Q_HEADS, KV_W), blk3),
                  _layer_spec(kc, (bb, KV_W, wc), blk3),
                  _layer_spec(vc, (bb, KV_W, wc), blk3),
                  pl.BlockSpec((bb, 1, KV_W), blk3),
                  pl.BlockSpec((bb, 1, KV_W), blk3),
                  pl.BlockSpec((N_Q_HEADS, 1), lambda i: (0, 0))],
        out_specs=pl.BlockSpec((bb, N_Q_HEADS, KV_W), blk3),
        out_shape=jax.ShapeDtypeStruct((n, N_Q_HEADS, KV_W), F32),
        compiler_params=_cparams(1),
        name="attn_sample",
    )(qm, kc[0], vc[0], kn, vn, sink)


def _ssd_sample_pre_kernel(uzx_ref, udt_ref, hist_ref, cw_ref, cb_ref, dtb_ref, alog_ref, expand_ref,
                           xbc_ref, xdt_ref, da_ref):
    nh = SSD_CONV - 1
    xbc = uzx_ref[:, SSD_INNER:SSD_ZX]
    acc = cb_ref[...] + xbc * cw_ref[nh:nh + 1, :]
    for k in range(nh):
        acc = acc + hist_ref[k] * cw_ref[k:k + 1, :]
    conv = _silu(acc)
    xbc_ref[...] = conv
    dt = _softplus(udt_ref[...] + dtb_ref[...])
    da_ref[...] = jnp.exp(dt * (-jnp.exp(alog_ref[...])))
    xdt_ref[...] = conv[:, 0:SSD_INNER] * _dot_sel_right(dt, expand_ref[...])


def _ssd_sample_pre(uzx, udt, hist, cw, cb, dtb, alog, expand, *, bb):
    n = uzx.shape[0]
    const = lambda i: (0, 0)
    row = lambda i: (i, 0)
    return pl.pallas_call(
        _ssd_sample_pre_kernel,
        grid=(n // bb,),
        in_specs=[pl.BlockSpec((bb, SSD_ZX), row),
                  pl.BlockSpec((bb, LANES), row),
                  _layer_spec(hist, (SSD_CONV - 1, bb, SSD_CONV_CH), lambda i: (0, i, 0)),
                  pl.BlockSpec((SSD_CONV, SSD_CONV_CH), const),
                  pl.BlockSpec((1, SSD_CONV_CH), const),
                  pl.BlockSpec((1, LANES), const),
                  pl.BlockSpec((1, LANES), const),
                  pl.BlockSpec((LANES, SSD_INNER), const)],
        out_specs=[pl.BlockSpec((bb, SSD_CONV_CH), row),
                   pl.BlockSpec((bb, SSD_INNER), row),
                   pl.BlockSpec((bb, LANES), row)],
        out_shape=[jax.ShapeDtypeStruct((n, SSD_CONV_CH), F32),
                   jax.ShapeDtypeStruct((n, SSD_INNER), F32),
                   jax.ShapeDtypeStruct((n, LANES), F32)],
        compiler_params=_cparams(1),
        name="ssd_sample_pre",
    )(uzx, udt, hist[0], cw, cb, dtb, alog, expand)


def _ssd_sample_state_kernel(da_ref, st_ref, xdt_ref, xbc_ref, *rest, bb, has_prev):
    so_ref, yt_ref, xt_ref, xr_ref, xb_ref, yr_ref = rest[1:] if has_prev else rest
    i = pl.program_id(0)

    @pl.when(i == 0)
    def _():
        xt_ref[...] = xdt_ref[...].T
        yt_ref[...] = jnp.zeros_like(yt_ref)
        yr_ref[...] = jnp.zeros_like(yr_ref)

    xr_ref[...] = pltpu.roll(xt_ref[...], (LANES - i * bb) % LANES, 1)
    lane_g = lax.broadcasted_iota(jnp.int32, (SSD_GW, LANES), 1)
    for j in range(bb):
        b = i * bb + j
        xb_ref[...] = jnp.broadcast_to(xr_ref[:, j:j + 1], xb_ref.shape)
        for g in range(SSD_GROUPS):
            b_row = xbc_ref[j:j + 1, SSD_INNER + g * SSD_STATE:SSD_INNER + (g + 1) * SSD_STATE]
            c_row = xbc_ref[j:j + 1, SSD_INNER + SSD_BC + g * SSD_STATE:SSD_INNER + SSD_BC + (g + 1) * SSD_STATE]
            c_rows = jnp.broadcast_to(c_row, (LANES, SSD_STATE)).astype(BF16)
            parts = []
            for k in range(SSD_HPG):
                h = g * SSD_HPG + k
                hs = slice(h * SSD_HEAD_DIM, (h + 1) * SSD_HEAD_DIM)
                hn = st_ref[0, j, hs, :] * da_ref[b, h] + xb_ref[hs, :] * b_row
                so_ref[0, j, hs, :] = hn
                parts.append(hn.astype(BF16))
            y = _dot_nt(jnp.concatenate(parts, axis=0), c_rows)
            rows = slice(g * SSD_GW, (g + 1) * SSD_GW)
            yr_ref[rows, :] = jnp.where(lane_g == j, y, yr_ref[rows, :])
    lane = lax.broadcasted_iota(jnp.int32, yt_ref.shape, 1)
    mine = (lane >= i * bb) & (lane < (i + 1) * bb)
    yt_ref[...] = jnp.where(mine, pltpu.roll(yr_ref[...], i * bb, 1), yt_ref[...])


def _ssd_sample_state(da, state_all, xdt, xbc, prev_out, *, layer_idx, bb):
    n = state_all.shape[1]
    has_prev = prev_out is not None
    slab = lambda i, s: (layer_idx, i, 0, 0)
    in_specs = [pl.BlockSpec((1, bb, SSD_INNER, SSD_STATE), slab),
                pl.BlockSpec((n, SSD_INNER), lambda i, s: (0, 0)),
                pl.BlockSpec((bb, SSD_CONV_CH), lambda i, s: (i, 0))]
    operands = [da, state_all, xdt, xbc]
    if has_prev:
        in_specs.append(pl.BlockSpec(memory_space=pl.ANY))
        operands.append(prev_out)
    grid_spec = pltpu.PrefetchScalarGridSpec(
        num_scalar_prefetch=1,
        grid=(n // bb,),
        in_specs=in_specs,
        out_specs=[pl.BlockSpec((1, bb, SSD_INNER, SSD_STATE), slab),
                   pl.BlockSpec((SSD_INNER, n), lambda i, s: (0, 0))],
        scratch_shapes=[pltpu.VMEM((SSD_INNER, n), F32),
                        pltpu.VMEM((SSD_INNER, n), F32),
                        pltpu.VMEM((SSD_INNER, SSD_STATE), F32),
                        pltpu.VMEM((SSD_INNER, n), F32)],
    )
    return pl.pallas_call(
        functools.partial(_ssd_sample_state_kernel, bb=bb, has_prev=has_prev),
        grid_spec=grid_spec,
        out_shape=[jax.ShapeDtypeStruct(state_all.shape, F32),
                   jax.ShapeDtypeStruct((SSD_INNER, n), F32)],
        input_output_aliases={len(operands) - 1: 0} if has_prev else {},
        compiler_params=_cparams(1),
        name="ssd_sample_state",
    )(*operands)


def _ssd_sample_post_kernel(yt_ref, xbc_ref, uzx_ref, dskip_ref, ng_ref, o_ref):
    y = yt_ref[...].T + dskip_ref[...] * xbc_ref[:, 0:SSD_INNER]
    y = y * _silu(uzx_ref[:, 0:SSD_INNER])
    for g in range(SSD_GROUPS):
        cols = slice(g * SSD_GW, (g + 1) * SSD_GW)
        o_ref[:, cols] = _rms(y[:, cols], ng_ref[:, cols])


def _ssd_sample_post(yt, xbc, uzx, dskip, ng):
    n = xbc.shape[0]
    const = lambda i: (0, 0)
    return pl.pallas_call(
        _ssd_sample_post_kernel,
        grid=(1,),
        in_specs=[pl.BlockSpec((SSD_INNER, n), const),
                  pl.BlockSpec((n, SSD_CONV_CH), const),
                  pl.BlockSpec((n, SSD_ZX), const),
                  pl.BlockSpec((1, SSD_INNER), const),
                  pl.BlockSpec((1, SSD_INNER), const)],
        out_specs=pl.BlockSpec((n, SSD_INNER), const),
        out_shape=jax.ShapeDtypeStruct((n, SSD_INNER), F32),
        compiler_params=_cparams(1),
        name="ssd_sample_post",
    )(yt, xbc, uzx, dskip, ng)


def _pad_lanes(v):
    return jnp.pad(v.astype(F32), (0, LANES - v.shape[0]))[None, :]


def _head_expand(v):
    return jnp.repeat(v.astype(F32), SSD_HEAD_DIM)[None, :]


def _prep_weights(P):
    stacks = {k: P[k].astype(BF16) for k in ("mlp_w_up", "mlp_w_down", "ab_w_in", "ab_w_out", "ssd_w_in",
                                             "ssd_w_out")}
    ssd_w_dt = jnp.pad(stacks["ssd_w_in"][:, :, SSD_ZX:], ((0, 0), (0, 0), (0, LANES - SSD_HEADS)))
    W = []
    for layer in range(4):
        i = layer // 2
        g = P["norm_g"][layer]
        w = {"g": [g[k][None, :] for k in range(4)],
             "w_up": (stacks["mlp_w_up"], layer), "w_down": (stacks["mlp_w_down"], layer)}
        if layer % 2 == 0:
            w.update(w_in=(stacks["ab_w_in"], i), w_out=(stacks["ab_w_out"], i),
                     dw=P["conf_dw_w"][i], db=P["conf_dw_b"][i][None, :],
                     lng=P["conf_ln_g"][i][None, :], lnb=P["conf_ln_b"][i][None, :],
                     sinks=P["attn_sinks"][i])
        else:
            w.update(w_zx=(stacks["ssd_w_in"], i), w_dt=(ssd_w_dt, i), w_out=(stacks["ssd_w_out"], i),
                     cw=P["ssd_conv_w"][i], cb=P["ssd_conv_b"][i][None, :],
                     dtb=_pad_lanes(P["ssd_dt_bias"][i]), alog=_pad_lanes(P["ssd_a_log"][i]),
                     dskip=_head_expand(P["ssd_d"][i]), ng=P["ssd_norm_g"][i][None, :])
        W.append(w)
    return W


def _trunk_prompt(x3, W):
    bsz, t, d = x3.shape
    x = x3.reshape(bsz * t, d)
    tri = jnp.tril(jnp.ones((SSD_CHUNK, SSD_CHUNK), F32)).astype(BF16)
    win_k, win_v, conf_rows, ssm_states, ssd_rows = [], [], [], [], []
    tq = 512
    for layer in range(4):
        w = W[layer]
        g = w["g"]
        if layer % 2 == 0:
            x, tails, kv = _ab_prompt(x, g[0], w["w_in"], w["sinks"], w["dw"], w["db"], w["lng"], w["lnb"],
                                      w["w_out"], g[1], bsz=bsz, t=t, tq=tq)
            assert t >= WINDOW == ATT_BLOCK
            kv = kv.reshape(bsz, t // tq, ATT_BLOCK, 2 * KV_W)[:, -1]
            win_k.append(kv[:, :, :KV_W].reshape(bsz, WINDOW, N_KV_HEADS, HEAD_DIM))
            win_v.append(kv[:, :, KV_W:].reshape(bsz, WINDOW, N_KV_HEADS, HEAD_DIM))
            tails = tails.reshape(bsz, t // tq, CONV_HALO, CONF_CH)
            conf_rows.append(tails[:, -1, CONV_HALO - (CONF_KERNEL - 1):, :])
            x = _mlp(x, g[2], w["w_up"], w["w_down"], g[3], tm=1024)
        else:
            y, state, tail = _ssd_prompt(x, g[0], w["w_zx"], w["w_dt"], w["cw"], w["cb"], w["dtb"], w["alog"],
                                         w["dskip"], w["ng"], tri, bsz=bsz, t=t)
            x = _proj_mlp(y, w["w_out"], x, g[1], g[2], w["w_up"], w["w_down"], g[3], tm=512)
            ssm_states.append(state.reshape(bsz, SSD_HEADS, SSD_HEAD_DIM, SSD_STATE))
            ssd_rows.append(tail.reshape(bsz, SSD_HALO, SSD_CONV_CH)[:, SSD_HALO - (SSD_CONV - 1):, :])
    return (x.reshape(bsz, t, d), jnp.stack(win_k), jnp.stack(win_v), jnp.stack(conf_rows),
            jnp.stack(ssm_states), jnp.stack(ssd_rows))


def _trunk_sample(x3, W, cache_k, cache_v, conf_hist, ssm, ssd_hist):
    n, t, d = x3.shape
    x = x3.reshape(n, d)
    expand = jnp.repeat(jnp.eye(LANES, SSD_HEADS, dtype=F32), SSD_HEAD_DIM, axis=1).astype(BF16)
    win_k, win_v, conf_rows, ssd_rows = [], [], [], []
    ssm_out = None
    head_group = jnp.arange(N_Q_HEADS) // Q_PER_KV
    for layer in range(4):
        w = W[layer]
        g = w["g"]
        i = layer // 2
        if layer % 2 == 0:
            u = _norm_matmul(x, g[0], w["w_in"], n=AB_IN, tm=n, tn=AB_IN // 2)
            c_out, a = _conf_sample(u, (conf_hist, i), w["dw"], w["db"], w["lng"], w["lnb"], bb=32)
            k_off = GLU_W + Q_W
            q = u[:, GLU_W:k_off].reshape(n, N_Q_HEADS, HEAD_DIM)
            k_new = u[:, k_off:k_off + KV_W]
            v_new = u[:, k_off + KV_W:]
            lane_group = jnp.arange(KV_W) // HEAD_DIM
            qm = jnp.where(head_group[None, :, None] == lane_group[None, None, :],
                           jnp.tile(q, (1, 1, N_KV_HEADS)), 0.0)
            o = _attn_sample(qm, (cache_k, i), (cache_v, i), k_new[:, None, :], v_new[:, None, :],
                             w["sinks"][:, None], bb=32)
            o = o.reshape(n, N_Q_HEADS, N_KV_HEADS, HEAD_DIM)
            a_out = jnp.take_along_axis(o, head_group[None, :, None, None], axis=2).reshape(n, Q_W)
            x = _proj_norm_res(jnp.concatenate([c_out, a_out], axis=-1), w["w_out"], x, g[1], tm=n)
            win_k.append(k_new.reshape(n, 1, N_KV_HEADS, HEAD_DIM))
            win_v.append(v_new.reshape(n, 1, N_KV_HEADS, HEAD_DIM))
            conf_rows.append(a[:, None, :])
        else:
            uzx = _norm_matmul(x, g[0], w["w_zx"], n=SSD_ZX, tm=n, tn=1024)
            udt = _norm_matmul(x, g[0], w["w_dt"], n=LANES, tm=n, tn=LANES)
            xbc, xdt, da = _ssd_sample_pre(uzx, udt, (ssd_hist, i), w["cw"], w["cb"], w["dtb"], w["alog"], expand,
                                           bb=32)
            ssm_out, yt = _ssd_sample_state(da, ssm.reshape(ssm.shape[0], n, SSD_INNER, SSD_STATE), xdt, xbc,
                                            ssm_out, layer_idx=i, bb=8)
            y = _ssd_sample_post(yt, xbc, uzx, w["dskip"], w["ng"])
            x = _proj_norm_res(y, w["w_out"], x, g[1], tm=n)
            ssd_rows.append(uzx[:, None, SSD_INNER:])
        x = _mlp(x, g[2], w["w_up"], w["w_down"], g[3], tm=n)
    return (x.reshape(n, t, d), jnp.stack(win_k), jnp.stack(win_v), jnp.stack(conf_rows),
            ssm_out.reshape(ssm.shape), jnp.stack(ssd_rows))


def kernel(x_prompt, x_sample, cache_win_k, cache_win_v, state_conf_conv, state_ssm, state_ssd_conv,
           norm_g, ab_w_in, conf_dw_w, conf_dw_b, conf_ln_g, conf_ln_b, attn_sinks, ab_w_out,
           ssd_w_in, ssd_conv_w, ssd_conv_b, ssd_dt_bias, ssd_a_log, ssd_d, ssd_norm_g, ssd_w_out,
           mlp_w_up, mlp_w_down):
    P = {"norm_g": norm_g, "ab_w_in": ab_w_in, "conf_dw_w": conf_dw_w, "conf_dw_b": conf_dw_b,
         "conf_ln_g": conf_ln_g, "conf_ln_b": conf_ln_b, "attn_sinks": attn_sinks, "ab_w_out": ab_w_out,
         "ssd_w_in": ssd_w_in, "ssd_conv_w": ssd_conv_w, "ssd_conv_b": ssd_conv_b, "ssd_dt_bias": ssd_dt_bias,
         "ssd_a_log": ssd_a_log, "ssd_d": ssd_d, "ssd_norm_g": ssd_norm_g, "ssd_w_out": ssd_w_out,
         "mlp_w_up": mlp_w_up, "mlp_w_down": mlp_w_down}
    W = _prep_weights(P)
    y_p, wk_p, wv_p, cc_p, ssm_p, sc_p = _trunk_prompt(x_prompt, W)
    cache_k = jnp.swapaxes(cache_win_k.reshape(cache_win_k.shape[:3] + (KV_W,)), 2, 3)
    cache_v = jnp.swapaxes(cache_win_v.reshape(cache_win_v.shape[:3] + (KV_W,)), 2, 3)
    y_s, wk_s, wv_s, cc_s, ssm_s, sc_s = _trunk_sample(x_sample, W, cache_k, cache_v,
                                                       jnp.swapaxes(state_conf_conv, 1, 2), state_ssm,
                                                       jnp.swapaxes(state_ssd_conv, 1, 2))
    return (y_p, y_s, wk_p, wv_p, wk_s, wv_s, cc_p, cc_s, ssm_p, ssm_s, sc_p, sc_s)
```

```python
import functools

import jax
import jax.numpy as jnp
from jax import lax
from jax.experimental import pallas as pl
from jax.experimental.pallas import tpu as pltpu

F32 = jnp.float32
BF16 = jnp.bfloat16

D_MODEL = 1024
CONF_CH = 512
CONF_KERNEL = 31
HEAD_DIM = 64
N_Q_HEADS = 8
N_KV_HEADS = 2
Q_PER_KV = N_Q_HEADS // N_KV_HEADS
WINDOW = 128
ATT_BLOCK = 128
ATT_SCALE = HEAD_DIM ** -0.5
Q_W = N_Q_HEADS * HEAD_DIM
KV_W = N_KV_HEADS * HEAD_DIM
GLU_W = 2 * CONF_CH
AB_IN = GLU_W + Q_W + 2 * KV_W
SSD_INNER = 2048
SSD_HEAD_DIM = 64
SSD_HEADS = 32
SSD_GROUPS = 4
SSD_HPG = SSD_HEADS // SSD_GROUPS
SSD_STATE = 128
SSD_CONV = 4
SSD_BC = SSD_GROUPS * SSD_STATE
SSD_CONV_CH = SSD_INNER + 2 * SSD_BC
SSD_ZX = SSD_INNER + SSD_CONV_CH
SSD_GW = SSD_INNER // SSD_GROUPS
SSD_CHUNK = 128
D_FF = 4096
RMS_EPS = 1e-6
LN_EPS = 1e-5

LANES = 128
SUBLANES = 8
VMEM_LIMIT = 56 * 1024 * 1024

NEG_INF = float("-inf")


def _cparams(n_axes):
    return pltpu.CompilerParams(dimension_semantics=("arbitrary",) * n_axes,
                                vmem_limit_bytes=VMEM_LIMIT)


def _layer_spec(stacked, block, index_map, **kw):
    layer = stacked[1]
    return pl.BlockSpec((None,) + tuple(block), lambda *g: (layer,) + tuple(index_map(*g)), **kw)


def _rms(x, g):
    return x * lax.rsqrt(jnp.mean(x * x, axis=-1, keepdims=True) + RMS_EPS) * g


def _sigmoid(x):
    return 1.0 / (1.0 + jnp.exp(-x))


def _silu(x):
    return x * _sigmoid(x)


def _log1p(e):
    u = 1.0 + e
    tiny = u == 1.0
    return jnp.where(tiny, e, jnp.log(u) * (e / jnp.where(tiny, 1.0, u - 1.0)))


def _softplus(x):
    return jnp.maximum(x, 0.0) + _log1p(jnp.exp(-jnp.abs(x)))


def _dot(a, b):
    return jnp.dot(a, b, preferred_element_type=F32)


def _dot_nt(a, b):
    return lax.dot_general(a, b, (((1,), (1,)), ((), ())), preferred_element_type=F32)


def _dot_tn(a, b):
    return lax.dot_general(a, b, (((0,), (0,)), ((), ())), preferred_element_type=F32)


def _split3(x):
    hi = x.astype(BF16)
    r1 = x - hi.astype(F32)
    mid = r1.astype(BF16)
    lo = (r1 - mid.astype(F32)).astype(BF16)
    return hi, mid, lo


def _dot_sel_left(sel, x):
    hi, mid, lo = _split3(x)
    return _dot(sel, hi) + _dot(sel, mid) + _dot(sel, lo)


def _dot_sel_right(x, sel):
    hi, mid, lo = _split3(x)
    return _dot(hi, sel) + _dot(mid, sel) + _dot(lo, sel)


def _norm_matmul_kernel(x_ref, g_ref, w_ref, o_ref, hn_ref):
    @pl.when(pl.program_id(1) == 0)
    def _():
        hn_ref[...] = _rms(x_ref[...], g_ref[...]).astype(BF16)

    o_ref[...] = _dot(hn_ref[...], w_ref[...])


def _norm_matmul(x, g, w, *, n, tm, tn):
    m, d = x.shape
    return pl.pallas_call(
        _norm_matmul_kernel,
        grid=(m // tm, n // tn),
        in_specs=[pl.BlockSpec((tm, d), lambda i, j: (i, 0)),
                  pl.BlockSpec((1, d), lambda i, j: (0, 0)),
                  _layer_spec(w, (d, tn), lambda i, j: (0, j))],
        out_specs=pl.BlockSpec((tm, tn), lambda i, j: (i, j)),
        out_shape=jax.ShapeDtypeStruct((m, n), F32),
        scratch_shapes=[pltpu.VMEM((tm, d), BF16)],
        compiler_params=_cparams(2),
        name="norm_matmul",
    )(x, g, w[0])


def _proj_norm_res_kernel(a_ref, w_ref, x_ref, g_ref, o_ref):
    y = _dot(a_ref[...].astype(BF16), w_ref[...])
    o_ref[...] = x_ref[...] + _rms(y, g_ref[...])


def _proj_norm_res(a, w, x, g, *, tm):
    m, k = a.shape
    d = x.shape[1]
    return pl.pallas_call(
        _proj_norm_res_kernel,
        grid=(m // tm,),
        in_specs=[pl.BlockSpec((tm, k), lambda i: (i, 0)),
                  _layer_spec(w, (k, d), lambda i: (0, 0)),
                  pl.BlockSpec((tm, d), lambda i: (i, 0)),
                  pl.BlockSpec((1, d), lambda i: (0, 0))],
        out_specs=pl.BlockSpec((tm, d), lambda i: (i, 0)),
        out_shape=jax.ShapeDtypeStruct((m, d), F32),
        compiler_params=_cparams(1),
        name="proj_norm_res",
    )(a, w[0], x, g)


def _mlp_block(x, gpre_ref, wup_ref, wdn_ref, gpost_ref, tf):
    h = _rms(x, gpre_ref[...]).astype(BF16)
    f = None
    for c in range(D_FF // tf):
        a = jnp.maximum(_dot(h, wup_ref[:, c * tf:(c + 1) * tf]), 0.0)
        part = _dot((a * a).astype(BF16), wdn_ref[c * tf:(c + 1) * tf, :])
        f = part if f is None else f + part
    return x + _rms(f, gpost_ref[...])


def _mlp_kernel(x_ref, gpre_ref, wup_ref, wdn_ref, gpost_ref, o_ref, *, tf):
    o_ref[...] = _mlp_block(x_ref[...], gpre_ref, wup_ref, wdn_ref, gpost_ref, tf)


def _proj_mlp_kernel(a_ref, wo_ref, x_ref, gmix_ref, gpre_ref, wup_ref, wdn_ref, gpost_ref, o_ref, *, tf):
    x_mid = x_ref[...] + _rms(_dot(a_ref[...].astype(BF16), wo_ref[...]), gmix_ref[...])
    o_ref[...] = _mlp_block(x_mid, gpre_ref, wup_ref, wdn_ref, gpost_ref, tf)


def _mlp(x, gpre, wup, wdn, gpost, *, tm, tf=1024):
    m, d = x.shape
    return pl.pallas_call(
        functools.partial(_mlp_kernel, tf=tf),
        grid=(m // tm,),
        in_specs=[pl.BlockSpec((tm, d), lambda i: (i, 0)),
                  pl.BlockSpec((1, d), lambda i: (0, 0)),
                  _layer_spec(wup, (d, D_FF), lambda i: (0, 0), pipeline_mode=pl.Buffered(1)),
                  _layer_spec(wdn, (D_FF, d), lambda i: (0, 0), pipeline_mode=pl.Buffered(1)),
                  pl.BlockSpec((1, d), lambda i: (0, 0))],
        out_specs=pl.BlockSpec((tm, d), lambda i: (i, 0)),
        out_shape=jax.ShapeDtypeStruct((m, d), F32),
        compiler_params=_cparams(1),
        name="mlp",
    )(x, gpre, wup[0], wdn[0], gpost)


def _proj_mlp(a, wo, x, gmix, gpre, wup, wdn, gpost, *, tm, tf=1024):
    m, d = x.shape
    k = a.shape[1]
    row = lambda i: (i, 0)
    const = lambda i: (0, 0)
    resident = pl.Buffered(1)
    return pl.pallas_call(
        functools.partial(_proj_mlp_kernel, tf=tf),
        grid=(m // tm,),
        in_specs=[pl.BlockSpec((tm, k), row),
                  _layer_spec(wo, (k, d), const, pipeline_mode=resident),
                  pl.BlockSpec((tm, d), row),
                  pl.BlockSpec((1, d), const),
                  pl.BlockSpec((1, d), const),
                  _layer_spec(wup, (d, D_FF), const, pipeline_mode=resident),
                  _layer_spec(wdn, (D_FF, d), const, pipeline_mode=resident),
                  pl.BlockSpec((1, d), const)],
        out_specs=pl.BlockSpec((tm, d), row),
        out_shape=jax.ShapeDtypeStruct((m, d), F32),
        compiler_params=_cparams(1),
        name="proj_mlp",
    )(a, wo[0], x, gmix, gpre, wup[0], wdn[0], gpost)


CONV_HALO = 32
CONV_ROWS = 512
AB_PROJ_COLS = 256


def _glu(u):
    return u[:, :CONF_CH] * _sigmoid(u[:, CONF_CH:GLU_W])


def _ab_prompt_kernel(sink_ref, x_ref, gin_ref, win_ref, dw_ref, db_ref, lng_ref, lnb_ref,
                      wout_ref, g_ref, o_ref, tail_ref, kvtail_ref,
                      u_ref, ukv_prev_ref, ap_ref, sh_ref, mix_ref, s_ref, p_ref, *, tq):
    i = pl.program_id(1)
    has_prev = i > 0
    n_ap = CONV_HALO + tq
    k_off = GLU_W + Q_W

    @pl.when(i == 0)
    def _():
        ap_ref[tq:n_ap, :] = jnp.zeros((CONV_HALO, CONF_CH), F32)
        u_ref[tq - ATT_BLOCK:tq, k_off:AB_IN] = jnp.zeros((ATT_BLOCK, 2 * KV_W), F32)

    ap_ref[0:CONV_HALO, :] = ap_ref[tq:n_ap, :]
    ukv_prev_ref[...] = u_ref[tq - ATT_BLOCK:tq, k_off:AB_IN]

    hn = _rms(x_ref[...], gin_ref[...]).astype(BF16)
    for cc in range(AB_IN // AB_PROJ_COLS):
        cols = slice(cc * AB_PROJ_COLS, (cc + 1) * AB_PROJ_COLS)
        u_ref[:, cols] = _dot(hn, win_ref[:, cols])
    kvtail_ref[...] = u_ref[tq - ATT_BLOCK:tq, k_off:AB_IN]

    ap_ref[CONV_HALO:n_ap, :] = _glu(u_ref[:, 0:GLU_W])
    ap_ref[n_ap:n_ap + SUBLANES, :] = jnp.zeros((SUBLANES, CONF_CH), F32)
    tail_ref[...] = ap_ref[tq:tq + CONV_HALO, :]
    for r in range(1, SUBLANES):
        sh_ref[r - 1] = ap_ref[r:r + n_ap, :]

    first_tap = CONV_HALO - (CONF_KERNEL - 1)
    for rc in range(tq // CONV_ROWS):
        acc = jnp.broadcast_to(db_ref[...], (CONV_ROWS, CONF_CH))
        for k in range(CONF_KERNEL):
            tile, r = divmod(first_tap + k, SUBLANES)
            start = rc * CONV_ROWS + tile * SUBLANES
            if r == 0:
                rows = ap_ref[start:start + CONV_ROWS, :]
            else:
                rows = sh_ref[r - 1, start:start + CONV_ROWS, :]
            acc = acc + rows * dw_ref[k:k + 1, :]
        mu = jnp.mean(acc, axis=-1, keepdims=True)
        xc = acc - mu
        var = jnp.mean(xc * xc, axis=-1, keepdims=True)
        ln = xc * lax.rsqrt(var + LN_EPS) * lng_ref[...] + lnb_ref[...]
        mix_ref[rc * CONV_ROWS:(rc + 1) * CONV_ROWS, 0:CONF_CH] = _silu(ln).astype(BF16)

    n_blk = tq // ATT_BLOCK
    n_keys = 2 * ATT_BLOCK
    grp_rows = Q_PER_KV * ATT_BLOCK
    blk_rows = N_Q_HEADS * ATT_BLOCK
    k_off = GLU_W + Q_W
    v_off = k_off + KV_W

    def kv_rows(j, off):
        rows = slice(j * ATT_BLOCK, (j + 1) * ATT_BLOCK)
        if j == 0:
            prev = ukv_prev_ref[:, off - k_off:off - k_off + KV_W]
        else:
            prev = u_ref[(j - 1) * ATT_BLOCK:j * ATT_BLOCK, off:off + KV_W]
        return jnp.concatenate([prev, u_ref[rows, off:off + KV_W]], axis=0)

    col0 = lax.broadcasted_iota(jnp.int32, (grp_rows, n_keys), 1) == 0
    for j in range(n_blk):
        rows = slice(j * ATT_BLOCK, (j + 1) * ATT_BLOCK)
        q = (u_ref[rows, GLU_W:GLU_W + Q_W] * ATT_SCALE).astype(BF16)
        kk = kv_rows(j, k_off).astype(BF16)
        for g in range(N_KV_HEADS):
            heads = [g * Q_PER_KV + hh for hh in range(Q_PER_KV)]
            qg = jnp.concatenate([q[:, h * HEAD_DIM:(h + 1) * HEAD_DIM] for h in heads], axis=0)
            s = _dot_nt(qg, kk[:, g * HEAD_DIM:(g + 1) * HEAD_DIM])
            sink = jnp.concatenate([jnp.full((ATT_BLOCK, 1), sink_ref[h], F32) for h in heads], axis=0)
            base = j * blk_rows + g * grp_rows
            s_ref[base:base + grp_rows, :] = jnp.where(col0, sink, s)

    qi = lax.broadcasted_iota(jnp.int32, (ATT_BLOCK, n_keys), 0)
    kc = lax.broadcasted_iota(jnp.int32, (ATT_BLOCK, n_keys), 1)
    visible = ((kc > qi) & (kc <= qi + WINDOW)) | (kc == 0)
    visible_first = visible & ((kc >= ATT_BLOCK) | (kc == 0) | has_prev)
    bias = jnp.where(visible, 0.0, NEG_INF)
    bias_first = jnp.where(visible_first, 0.0, NEG_INF)
    for lo, hi, bb in ((0, blk_rows, bias_first), (blk_rows, n_blk * blk_rows, bias)):
        sm = s_ref[lo:hi, :].reshape((hi - lo) // ATT_BLOCK, ATT_BLOCK, n_keys) + bb
        m = jnp.max(sm, axis=-1, keepdims=True)
        p_ref[lo:hi, :] = jnp.exp(sm - m).astype(BF16).reshape(hi - lo, n_keys)

    vrow = lax.broadcasted_iota(jnp.int32, (n_keys, HEAD_DIM), 0)
    ones = jnp.ones((n_keys, HEAD_DIM), BF16)
    lo_half = lax.broadcasted_iota(jnp.int32, (ATT_BLOCK, 2 * HEAD_DIM), 1) < HEAD_DIM
    for j in range(n_blk):
        rows = slice(j * ATT_BLOCK, (j + 1) * ATT_BLOCK)
        vv = kv_rows(j, v_off)
        for g in range(N_KV_HEADS):
            vg = jnp.where(vrow > 0, vv[:, g * HEAD_DIM:(g + 1) * HEAD_DIM], 0.0).astype(BF16)
            v_ext = (jnp.concatenate([vg, ones], axis=1), jnp.concatenate([ones, vg], axis=1))
            for pr in range(Q_PER_KV // 2):
                normed = []
                for par in range(2):
                    base = j * blk_rows + g * grp_rows + (2 * pr + par) * ATT_BLOCK
                    oe = _dot(p_ref[base:base + ATT_BLOCK, :], v_ext[par])
                    normed.append(oe / pltpu.roll(oe, HEAD_DIM, 1))
                c0 = CONF_CH + (g * Q_PER_KV + 2 * pr) * HEAD_DIM
                mix_ref[rows, c0:c0 + 2 * HEAD_DIM] = jnp.where(lo_half, normed[0], normed[1]).astype(BF16)

    y = _dot(mix_ref[...], wout_ref[...])
    o_ref[...] = x_ref[...] + _rms(y, g_ref[...])


def _ab_prompt(x, gin, win, sinks, dw, db, lng, lnb, wout, g, *, bsz, t, tq):
    nt = t // tq
    m = bsz * t
    d = x.shape[1]

    def cur(b, i, s):
        return (b * nt + i, 0)

    const = lambda b, i, s: (0, 0)
    resident = pl.Buffered(1)
    grid_spec = pltpu.PrefetchScalarGridSpec(
        num_scalar_prefetch=1,
        grid=(bsz, nt),
        in_specs=[pl.BlockSpec((tq, d), cur),
                  pl.BlockSpec((1, d), const),
                  _layer_spec(win, (d, AB_IN), const, pipeline_mode=resident),
                  pl.BlockSpec(dw.shape, const),
                  pl.BlockSpec((1, CONF_CH), const),
                  pl.BlockSpec((1, CONF_CH), const),
                  pl.BlockSpec((1, CONF_CH), const),
                  _layer_spec(wout, (CONF_CH + Q_W, d), const, pipeline_mode=resident),
                  pl.BlockSpec((1, d), const)],
        out_specs=[pl.BlockSpec((tq, d), cur),
                   pl.BlockSpec((CONV_HALO, CONF_CH), cur),
                   pl.BlockSpec((ATT_BLOCK, 2 * KV_W), cur)],
        scratch_shapes=[pltpu.VMEM((tq, AB_IN), F32),
                        pltpu.VMEM((ATT_BLOCK, 2 * KV_W), F32),
                        pltpu.VMEM((CONV_HALO + tq + SUBLANES, CONF_CH), F32),
                        pltpu.VMEM((SUBLANES - 1, CONV_HALO + tq, CONF_CH), F32),
                        pltpu.VMEM((tq, CONF_CH + Q_W), BF16),
                        pltpu.VMEM((tq * N_Q_HEADS, 2 * ATT_BLOCK), F32),
                        pltpu.VMEM((tq * N_Q_HEADS, 2 * ATT_BLOCK), BF16)],
    )
    return pl.pallas_call(
        functools.partial(_ab_prompt_kernel, tq=tq),
        grid_spec=grid_spec,
        out_shape=[jax.ShapeDtypeStruct((m, d), F32),
                   jax.ShapeDtypeStruct((bsz * nt * CONV_HALO, CONF_CH), F32),
                   jax.ShapeDtypeStruct((bsz * nt * ATT_BLOCK, 2 * KV_W), F32)],
        compiler_params=_cparams(2),
        name="ab_prompt",
    )(sinks, x, gin, win[0], dw, db, lng, lnb, wout[0], g)


SSD_HALO = SUBLANES
SSD_CONV_COLS = 256
SSD_PROJ_COLS = 1024
SSD_STEP_CHUNKS = 2


def _ssd_conv_masks(ch):
    sub = lax.broadcasted_iota(jnp.int32, (SUBLANES, ch), 0)
    return [sub < back for back in range(SSD_CONV)]


def _ssd_conv_silu(halo, cur, w, b, wrapped):
    rows, ch = cur.shape
    tiles = [halo] + [cur[r:r + SUBLANES, :] for r in range(0, rows, SUBLANES)]
    acc = b + cur * w[SSD_CONV - 1:SSD_CONV, :]
    for back in range(1, SSD_CONV):
        rolled = [pltpu.roll(v, back, 0) for v in tiles]
        shifted = jnp.concatenate([jnp.where(wrapped[back], rolled[i], rolled[i + 1])
                                   for i in range(len(tiles) - 1)], axis=0)
        acc = acc + shifted * w[SSD_CONV - 1 - back:SSD_CONV - back, :]
    return _silu(acc)


def _ssd_prompt_kernel(x0_ref, xn_ref, g_ref, wzx_ref, wdt_ref, cw_ref, cb_ref, dtb_ref, alog_ref, dskip_ref,
                       ng_ref, tri_ref, y_ref, state_ref, tail_ref,
                       ua_ref, dta_ref, ub_ref, dtb2_ref, hn_ref, halo_ref, xbc_ref, h_ref, xdec_ref, *, n_steps):
    c = pl.program_id(1)
    n = pl.program_id(0) * n_steps + c

    def in_proj_pieces(x_ref, u_ref, udt_ref):
        def norm():
            hn_ref[...] = _rms(x_ref[...], g_ref[...]).astype(BF16)

        def zx(cc):
            cols = slice(cc * SSD_PROJ_COLS, (cc + 1) * SSD_PROJ_COLS)
            u_ref[:, cols] = _dot(hn_ref[...], wzx_ref[:, cols])

        def dt():
            udt_ref[...] = _dot(hn_ref[...], wdt_ref[...])

        return [norm] + [functools.partial(zx, cc) for cc in range(SSD_ZX // SSD_PROJ_COLS)] + [dt]

    @pl.when(n == 0)
    def _():
        for piece in in_proj_pieces(x0_ref, ua_ref, dta_ref):
            piece()

    @pl.when(c == 0)
    def _():
        halo_ref[...] = jnp.zeros_like(halo_ref)
        h_ref[...] = jnp.zeros_like(h_ref)

    refs = (cw_ref, cb_ref, dtb_ref, alog_ref, dskip_ref, ng_ref, tri_ref, y_ref,
            halo_ref, xbc_ref, h_ref, xdec_ref)

    def step(u_ref, udt_ref, fillers):
        n_stages = SSD_STEP_CHUNKS * (SSD_CONV_CH // SSD_CONV_COLS + SSD_HEADS // 2)
        n_fill = len(fillers)
        stage = [0]

        def fill():
            stage[0] += 1
            while fillers and n_fill - len(fillers) < (stage[0] * n_fill + n_stages - 1) // n_stages:
                fillers.pop(0)()

        for k in range(SSD_STEP_CHUNKS):
            _ssd_chunk(u_ref, udt_ref, k * SSD_CHUNK, *refs, fill=fill)
        assert not fillers

    @pl.when(n % 2 == 0)
    def _():
        step(ua_ref, dta_ref, in_proj_pieces(xn_ref, ub_ref, dtb2_ref))

    @pl.when(n % 2 == 1)
    def _():
        step(ub_ref, dtb2_ref, in_proj_pieces(xn_ref, ua_ref, dta_ref))

    @pl.when(c == n_steps - 1)
    def _():
        state_ref[0] = h_ref[...]
        tail_ref[...] = halo_ref[...]


def _ssd_chunk(u_ref, udt_ref, r0, cw_ref, cb_ref, dtb_ref, alog_ref, dskip_ref, ng_ref, tri_ref, y_ref,
               halo_ref, xbc_ref, h_ref, xdec_ref, *, fill):
    L = SSD_CHUNK
    crows = slice(r0, r0 + L)
    wrapped = _ssd_conv_masks(SSD_CONV_COLS)
    for cc in range(SSD_CONV_CH // SSD_CONV_COLS):
        fill()
        cols = slice(cc * SSD_CONV_COLS, (cc + 1) * SSD_CONV_COLS)
        ucols = slice(SSD_INNER + cc * SSD_CONV_COLS, SSD_INNER + (cc + 1) * SSD_CONV_COLS)
        xbc_ref[:, cols] = _ssd_conv_silu(halo_ref[:, cols], u_ref[crows, ucols], cw_ref[:, cols], cb_ref[:, cols],
                                          wrapped)
    halo_ref[...] = u_ref[r0 + L - SSD_HALO:r0 + L, SSD_INNER:SSD_ZX]

    dt = _softplus(udt_ref[crows, :] + dtb_ref[...])
    a = -jnp.exp(alog_ref[...])
    cs = _dot_sel_left(tri_ref[...], dt * a)
    cs_t = cs.T

    row = lax.broadcasted_iota(jnp.int32, (L, LANES), 0)
    lane = lax.broadcasted_iota(jnp.int32, (L, LANES), 1)
    causal = row >= lane
    lo_half = lane < SSD_HEAD_DIM

    def bc(v, h):
        return jnp.broadcast_to(v[:, h:h + 1], (L, LANES))

    for g in range(SSD_GROUPS):
        b_g = xbc_ref[:, SSD_INNER + g * SSD_STATE:SSD_INNER + (g + 1) * SSD_STATE].astype(BF16)
        c_g = xbc_ref[:, SSD_INNER + SSD_BC + g * SSD_STATE:SSD_INNER + SSD_BC + (g + 1) * SSD_STATE].astype(BF16)
        hrows = slice(g * SSD_GW, (g + 1) * SSD_GW)
        cb = _dot_nt(c_g, b_g)
        yoff_raw = _dot_nt(c_g, h_ref[hrows, :].astype(BF16))
        end_decay = []
        gated = []
        for pr in range(SSD_HPG // 2):
            fill()
            h1 = g * SSD_HPG + 2 * pr
            h2 = h1 + 1
            cols = slice(h1 * SSD_HEAD_DIM, (h1 + 2) * SSD_HEAD_DIM)
            a1 = bc(cs, h1)
            a2 = bc(cs, h2)
            m1 = cb * jnp.exp(jnp.where(causal, a1 - cs_t[h1:h1 + 1, :], NEG_INF))
            m2 = cb * jnp.exp(jnp.where(causal, a2 - cs_t[h2:h2 + 1, :], NEG_INF))
            xs = xbc_ref[:, cols]
            xc = xs * jnp.where(lo_half, bc(dt, h1), bc(dt, h2))
            lhs = jnp.concatenate([m1, m2], axis=1).astype(BF16)
            rhs = jnp.concatenate([jnp.where(lo_half, xc, 0.0), jnp.where(lo_half, 0.0, xc)], axis=0).astype(BF16)
            y_diag = _dot(lhs, rhs)
            csp = jnp.where(lo_half, a1, a2)
            y_off = jnp.exp(csp) * yoff_raw[:, 2 * pr * SSD_HEAD_DIM:(2 * pr + 2) * SSD_HEAD_DIM]
            xdec_ref[:, cols] = (xc * jnp.exp(csp[L - 1:L, :] - csp)).astype(BF16)
            end_decay += [jnp.exp(a1[L - 1:L, :]), jnp.exp(a2[L - 1:L, :])]
            y = y_diag + y_off + dskip_ref[:, cols] * xs
            gated.append(y * _silu(u_ref[crows, cols]))
        ss = sum(jnp.sum(v * v, axis=-1, keepdims=True) for v in gated)
        scale = lax.rsqrt(ss * (1.0 / SSD_GW) + RMS_EPS)
        for pr in range(SSD_HPG // 2):
            cols = slice(g * SSD_GW + pr * LANES, g * SSD_GW + (pr + 1) * LANES)
            y_ref[crows, cols] = (gated[pr] * scale * ng_ref[:, cols]).astype(y_ref.dtype)
        st = _dot_tn(xdec_ref[:, hrows], b_g)
        for k in range(SSD_HPG):
            hs = slice(g * SSD_GW + k * SSD_HEAD_DIM, g * SSD_GW + (k + 1) * SSD_HEAD_DIM)
            h_ref[hs, :] = h_ref[hs, :] * end_decay[k] + st[k * SSD_HEAD_DIM:(k + 1) * SSD_HEAD_DIM, :]


def _ssd_prompt(x, g, wzx, wdt, cw, cb, dtb, alog, dskip, ng, tri, *, bsz, t):
    rows = SSD_STEP_CHUNKS * SSD_CHUNK
    nc = t // rows
    L = SSD_CHUNK
    d = x.shape[1]
    last = bsz * nc - 1
    cur = lambda b, c: (b * nc + c, 0)
    nxt = lambda b, c: (jnp.minimum(b * nc + c + 1, last), 0)
    const = lambda b, c: (0, 0)
    resident = pl.Buffered(1)
    return pl.pallas_call(
        functools.partial(_ssd_prompt_kernel, n_steps=nc),
        grid=(bsz, nc),
        in_specs=[pl.BlockSpec((rows, d), const),
                  pl.BlockSpec((rows, d), nxt),
                  pl.BlockSpec((1, d), const),
                  _layer_spec(wzx, (d, SSD_ZX), const, pipeline_mode=resident),
                  _layer_spec(wdt, (d, LANES), const, pipeline_mode=resident),
                  pl.BlockSpec((SSD_CONV, SSD_CONV_CH), const),
                  pl.BlockSpec((1, SSD_CONV_CH), const),
                  pl.BlockSpec((1, LANES), const),
                  pl.BlockSpec((1, LANES), const),
                  pl.BlockSpec((1, SSD_INNER), const),
                  pl.BlockSpec((1, SSD_INNER), const),
                  pl.BlockSpec((L, L), const)],
        out_specs=[pl.BlockSpec((rows, SSD_INNER), cur),
                   pl.BlockSpec((1, SSD_INNER, SSD_STATE), lambda b, c: (b, 0, 0)),
                   pl.BlockSpec((SSD_HALO, SSD_CONV_CH), lambda b, c: (b, 0))],
        out_shape=[jax.ShapeDtypeStruct((bsz * t, SSD_INNER), BF16),
                   jax.ShapeDtypeStruct((bsz, SSD_INNER, SSD_STATE), F32),
                   jax.ShapeDtypeStruct((bsz * SSD_HALO, SSD_CONV_CH), F32)],
        scratch_shapes=[pltpu.VMEM((rows, SSD_ZX), F32),
                        pltpu.VMEM((rows, LANES), F32),
                        pltpu.VMEM((rows, SSD_ZX), F32),
                        pltpu.VMEM((rows, LANES), F32),
                        pltpu.VMEM((rows, d), BF16),
                        pltpu.VMEM((SSD_HALO, SSD_CONV_CH), F32),
                        pltpu.VMEM((L, SSD_CONV_CH), F32),
                        pltpu.VMEM((SSD_INNER, SSD_STATE), F32),
                        pltpu.VMEM((L, SSD_INNER), BF16)],
        compiler_params=_cparams(2),
        name="ssd_prompt",
    )(x, x, g, wzx[0], wdt[0], cw, cb, dtb, alog, dskip, ng, tri)


def _conf_sample_kernel(u_ref, hist_ref, dw_ref, db_ref, lng_ref, lnb_ref, c_ref, a_ref):
    a = _glu(u_ref[:, 0:GLU_W])
    a_ref[...] = a
    nh = CONF_KERNEL - 1
    acc = db_ref[...] + a * dw_ref[nh:nh + 1, :]
    for k in range(nh):
        acc = acc + hist_ref[k] * dw_ref[k:k + 1, :]
    mu = jnp.mean(acc, axis=-1, keepdims=True)
    xc = acc - mu
    var = jnp.mean(xc * xc, axis=-1, keepdims=True)
    c_ref[...] = _silu(xc * lax.rsqrt(var + LN_EPS) * lng_ref[...] + lnb_ref[...])


def _conf_sample(u, hist, dw, db, lng, lnb, *, bb):
    n = u.shape[0]
    nh = CONF_KERNEL - 1
    const = lambda i: (0, 0)
    return pl.pallas_call(
        _conf_sample_kernel,
        grid=(n // bb,),
        in_specs=[pl.BlockSpec((bb, AB_IN), lambda i: (i, 0)),
                  _layer_spec(hist, (nh, bb, CONF_CH), lambda i: (0, i, 0)),
                  pl.BlockSpec(dw.shape, const),
                  pl.BlockSpec((1, CONF_CH), const),
                  pl.BlockSpec((1, CONF_CH), const),
                  pl.BlockSpec((1, CONF_CH), const)],
        out_specs=[pl.BlockSpec((bb, CONF_CH), lambda i: (i, 0)),
                   pl.BlockSpec((bb, CONF_CH), lambda i: (i, 0))],
        out_shape=[jax.ShapeDtypeStruct((n, CONF_CH), F32),
                   jax.ShapeDtypeStruct((n, CONF_CH), F32)],
        compiler_params=_cparams(1),
        name="conf_sample",
    )(u, hist[0], dw, db, lng, lnb)


def _attn_sample_kernel(qm_ref, kc_ref, vc_ref, kn_ref, vn_ref, sink_ref, o_ref, *, wc):
    q = qm_ref[...] * ATT_SCALE
    s = jnp.einsum("bhc,bcs->bhs", q.astype(BF16), kc_ref[...].astype(BF16), preferred_element_type=F32)
    kpos = lax.broadcasted_iota(jnp.int32, s.shape, 2)
    s = jnp.where((kpos <= wc) & (wc - kpos < WINDOW), s, NEG_INF)
    s_new = jnp.sum(q * kn_ref[...], axis=-1, keepdims=True)
    sink = sink_ref[...][None]
    m = jnp.maximum(jnp.maximum(jnp.max(s, axis=-1, keepdims=True), s_new), sink)
    p = jnp.exp(s - m)
    p_new = jnp.exp(s_new - m)
    den = jnp.sum(p, axis=-1, keepdims=True) + p_new + jnp.exp(sink - m)
    o = jnp.einsum("bhs,bcs->bhc", p.astype(BF16), vc_ref[...].astype(BF16), preferred_element_type=F32)
    o_ref[...] = (o + p_new * vn_ref[...]) / den


def _attn_sample(qm, kc, vc, kn, vn, sink, *, bb):
    _, n, _, wc = kc[0].shape
    blk3 = lambda i: (i, 0, 0)
    return pl.pallas_call(
        functools.partial(_attn_sample_kernel, wc=wc),
        grid=(n // bb,),
        in_specs=[pl.BlockSpec((bb, N_Q_HEADS, KV_W), blk3),
                  _layer_spec(kc, (bb, KV_W, wc), blk3),
                  _layer_spec(vc, (bb, KV_W, wc), blk3),
                  pl.BlockSpec((bb, 1, KV_W), blk3),
                  pl.BlockSpec((bb, 1, KV_W), blk3),
                  pl.BlockSpec((N_Q_HEADS, 1), lambda i: (0, 0))],
        out_specs=pl.BlockSpec((bb, N_Q_HEADS, KV_W), blk3),
        out_shape=jax.ShapeDtypeStruct((n, N_Q_HEADS, KV_W), F32),
        compiler_params=_cparams(1),
        name="attn_sample",
    )(qm, kc[0], vc[0], kn, vn, sink)


def _ssd_sample_pre_kernel(uzx_ref, udt_ref, hist_ref, cw_ref, cb_ref, dtb_ref, alog_ref, expand_ref,
                           xbc_ref, xdt_ref, da_ref):
    nh = SSD_CONV - 1
    xbc = uzx_ref[:, SSD_INNER:SSD_ZX]
    acc = cb_ref[...] + xbc * cw_ref[nh:nh + 1, :]
    for k in range(nh):
        acc = acc + hist_ref[k] * cw_ref[k:k + 1, :]
    conv = _silu(acc)
    xbc_ref[...] = conv
    dt = _softplus(udt_ref[...] + dtb_ref[...])
    da_ref[...] = jnp.exp(dt * (-jnp.exp(alog_ref[...])))
    xdt_ref[...] = conv[:, 0:SSD_INNER] * _dot_sel_right(dt, expand_ref[...])


def _ssd_sample_pre(uzx, udt, hist, cw, cb, dtb, alog, expand, *, bb):
    n = uzx.shape[0]
    const = lambda i: (0, 0)
    row = lambda i: (i, 0)
    return pl.pallas_call(
        _ssd_sample_pre_kernel,
        grid=(n // bb,),
        in_specs=[pl.BlockSpec((bb, SSD_ZX), row),
                  pl.BlockSpec((bb, LANES), row),
                  _layer_spec(hist, (SSD_CONV - 1, bb, SSD_CONV_CH), lambda i: (0, i, 0)),
                  pl.BlockSpec((SSD_CONV, SSD_CONV_CH), const),
                  pl.BlockSpec((1, SSD_CONV_CH), const),
                  pl.BlockSpec((1, LANES), const),
                  pl.BlockSpec((1, LANES), const),
                  pl.BlockSpec((LANES, SSD_INNER), const)],
        out_specs=[pl.BlockSpec((bb, SSD_CONV_CH), row),
                   pl.BlockSpec((bb, SSD_INNER), row),
                   pl.BlockSpec((bb, LANES), row)],
        out_shape=[jax.ShapeDtypeStruct((n, SSD_CONV_CH), F32),
                   jax.ShapeDtypeStruct((n, SSD_INNER), F32),
                   jax.ShapeDtypeStruct((n, LANES), F32)],
        compiler_params=_cparams(1),
        name="ssd_sample_pre",
    )(uzx, udt, hist[0], cw, cb, dtb, alog, expand)


def _ssd_sample_state_kernel(da_ref, st_ref, xdt_ref, xbc_ref, *rest, bb, has_prev):
    so_ref, yt_ref, xt_ref, xr_ref, xb_ref, yr_ref = rest[1:] if has_prev else rest
    i = pl.program_id(0)

    @pl.when(i == 0)
    def _():
        xt_ref[...] = xdt_ref[...].T
        yt_ref[...] = jnp.zeros_like(yt_ref)
        yr_ref[...] = jnp.zeros_like(yr_ref)

    xr_ref[...] = pltpu.roll(xt_ref[...], (LANES - i * bb) % LANES, 1)
    lane_g = lax.broadcasted_iota(jnp.int32, (SSD_GW, LANES), 1)
    for j in range(bb):
        b = i * bb + j
        xb_ref[...] = jnp.broadcast_to(xr_ref[:, j:j + 1], xb_ref.shape)
        for g in range(SSD_GROUPS):
            b_row = xbc_ref[j:j + 1, SSD_INNER + g * SSD_STATE:SSD_INNER + (g + 1) * SSD_STATE]
            c_row = xbc_ref[j:j + 1, SSD_INNER + SSD_BC + g * SSD_STATE:SSD_INNER + SSD_BC + (g + 1) * SSD_STATE]
            c_rows = jnp.broadcast_to(c_row, (LANES, SSD_STATE)).astype(BF16)
            parts = []
            for k in range(SSD_HPG):
                h = g * SSD_HPG + k
                hs = slice(h * SSD_HEAD_DIM, (h + 1) * SSD_HEAD_DIM)
                hn = st_ref[0, j, hs, :] * da_ref[b, h] + xb_ref[hs, :] * b_row
                so_ref[0, j, hs, :] = hn
                parts.append(hn.astype(BF16))
            y = _dot_nt(jnp.concatenate(parts, axis=0), c_rows)
            rows = slice(g * SSD_GW, (g + 1) * SSD_GW)
            yr_ref[rows, :] = jnp.where(lane_g == j, y, yr_ref[rows, :])
    lane = lax.broadcasted_iota(jnp.int32, yt_ref.shape, 1)
    mine = (lane >= i * bb) & (lane < (i + 1) * bb)
    yt_ref[...] = jnp.where(mine, pltpu.roll(yr_ref[...], i * bb, 1), yt_ref[...])


def _ssd_sample_state(da, state_all, xdt, xbc, prev_out, *, layer_idx, bb):
    n = state_all.shape[1]
    has_prev = prev_out is not None
    slab = lambda i, s: (layer_idx, i, 0, 0)
    in_specs = [pl.BlockSpec((1, bb, SSD_INNER, SSD_STATE), slab),
                pl.BlockSpec((n, SSD_INNER), lambda i, s: (0, 0)),
                pl.BlockSpec((bb, SSD_CONV_CH), lambda i, s: (i, 0))]
    operands = [da, state_all, xdt, xbc]
    if has_prev:
        in_specs.append(pl.BlockSpec(memory_space=pl.ANY))
        operands.append(prev_out)
    grid_spec = pltpu.PrefetchScalarGridSpec(
        num_scalar_prefetch=1,
        grid=(n // bb,),
        in_specs=in_specs,
        out_specs=[pl.BlockSpec((1, bb, SSD_INNER, SSD_STATE), slab),
                   pl.BlockSpec((SSD_INNER, n), lambda i, s: (0, 0))],
        scratch_shapes=[pltpu.VMEM((SSD_INNER, n), F32),
                        pltpu.VMEM((SSD_INNER, n), F32),
                        pltpu.VMEM((SSD_INNER, SSD_STATE), F32),
                        pltpu.VMEM((SSD_INNER, n), F32)],
    )
    return pl.pallas_call(
        functools.partial(_ssd_sample_state_kernel, bb=bb, has_prev=has_prev),
        grid_spec=grid_spec,
        out_shape=[jax.ShapeDtypeStruct(state_all.shape, F32),
                   jax.ShapeDtypeStruct((SSD_INNER, n), F32)],
        input_output_aliases={len(operands) - 1: 0} if has_prev else {},
        compiler_params=_cparams(1),
        name="ssd_sample_state",
    )(*operands)


def _ssd_sample_post_kernel(yt_ref, xbc_ref, uzx_ref, dskip_ref, ng_ref, o_ref):
    y = yt_ref[...].T + dskip_ref[...] * xbc_ref[:, 0:SSD_INNER]
    y = y * _silu(uzx_ref[:, 0:SSD_INNER])
    for g in range(SSD_GROUPS):
        cols = slice(g * SSD_GW, (g + 1) * SSD_GW)
        o_ref[:, cols] = _rms(y[:, cols], ng_ref[:, cols])


def _ssd_sample_post(yt, xbc, uzx, dskip, ng):
    n = xbc.shape[0]
    const = lambda i: (0, 0)
    return pl.pallas_call(
        _ssd_sample_post_kernel,
        grid=(1,),
        in_specs=[pl.BlockSpec((SSD_INNER, n), const),
                  pl.BlockSpec((n, SSD_CONV_CH), const),
                  pl.BlockSpec((n, SSD_ZX), const),
                  pl.BlockSpec((1, SSD_INNER), const),
                  pl.BlockSpec((1, SSD_INNER), const)],
        out_specs=pl.BlockSpec((n, SSD_INNER), const),
        out_shape=jax.ShapeDtypeStruct((n, SSD_INNER), F32),
        compiler_params=_cparams(1),
        name="ssd_sample_post",
    )(yt, xbc, uzx, dskip, ng)


def _pad_lanes(v):
    return jnp.pad(v.astype(F32), (0, LANES - v.shape[0]))[None, :]


def _head_expand(v):
    return jnp.repeat(v.astype(F32), SSD_HEAD_DIM)[None, :]


def _prep_weights(P):
    stacks = {k: P[k].astype(BF16) for k in ("mlp_w_up", "mlp_w_down", "ab_w_in", "ab_w_out", "ssd_w_in",
                                             "ssd_w_out")}
    ssd_w_dt = jnp.pad(stacks["ssd_w_in"][:, :, SSD_ZX:], ((0, 0), (0, 0), (0, LANES - SSD_HEADS)))
    W = []
    for layer in range(4):
        i = layer // 2
        g = P["norm_g"][layer]
        w = {"g": [g[k][None, :] for k in range(4)],
             "w_up": (stacks["mlp_w_up"], layer), "w_down": (stacks["mlp_w_down"], layer)}
        if layer % 2 == 0:
            w.update(w_in=(stacks["ab_w_in"], i), w_out=(stacks["ab_w_out"], i),
                     dw=P["conf_dw_w"][i], db=P["conf_dw_b"][i][None, :],
                     lng=P["conf_ln_g"][i][None, :], lnb=P["conf_ln_b"][i][None, :],
                     sinks=P["attn_sinks"][i])
        else:
            w.update(w_zx=(stacks["ssd_w_in"], i), w_dt=(ssd_w_dt, i), w_out=(stacks["ssd_w_out"], i),
                     cw=P["ssd_conv_w"][i], cb=P["ssd_conv_b"][i][None, :],
                     dtb=_pad_lanes(P["ssd_dt_bias"][i]), alog=_pad_lanes(P["ssd_a_log"][i]),
                     dskip=_head_expand(P["ssd_d"][i]), ng=P["ssd_norm_g"][i][None, :])
        W.append(w)
    return W


def _trunk_prompt(x3, W):
    bsz, t, d = x3.shape
    x = x3.reshape(bsz * t, d)
    tri = jnp.tril(jnp.ones((SSD_CHUNK, SSD_CHUNK), F32)).astype(BF16)
    win_k, win_v, conf_rows, ssm_states, ssd_rows = [], [], [], [], []
    tq = 512
    for layer in range(4):
        w = W[layer]
        g = w["g"]
        if layer % 2 == 0:
            x, tails, kv = _ab_prompt(x, g[0], w["w_in"], w["sinks"], w["dw"], w["db"], w["lng"], w["lnb"],
                                      w["w_out"], g[1], bsz=bsz, t=t, tq=tq)
            assert t >= WINDOW == ATT_BLOCK
            kv = kv.reshape(bsz, t // tq, ATT_BLOCK, 2 * KV_W)[:, -1]
            win_k.append(kv[:, :, :KV_W].reshape(bsz, WINDOW, N_KV_HEADS, HEAD_DIM))
            win_v.append(kv[:, :, KV_W:].reshape(bsz, WINDOW, N_KV_HEADS, HEAD_DIM))
            tails = tails.reshape(bsz, t // tq, CONV_HALO, CONF_CH)
            conf_rows.append(tails[:, -1, CONV_HALO - (CONF_KERNEL - 1):, :])
            x = _mlp(x, g[2], w["w_up"], w["w_down"], g[3], tm=1024)
        else:
            y, state, tail = _ssd_prompt(x, g[0], w["w_zx"], w["w_dt"], w["cw"], w["cb"], w["dtb"], w["alog"],
                                         w["dskip"], w["ng"], tri, bsz=bsz, t=t)
            x = _proj_mlp(y, w["w_out"], x, g[1], g[2], w["w_up"], w["w_down"], g[3], tm=512)
            ssm_states.append(state.reshape(bsz, SSD_HEADS, SSD_HEAD_DIM, SSD_STATE))
            ssd_rows.append(tail.reshape(bsz, SSD_HALO, SSD_CONV_CH)[:, SSD_HALO - (SSD_CONV - 1):, :])
    return (x.reshape(bsz, t, d), jnp.stack(win_k), jnp.stack(win_v), jnp.stack(conf_rows),
            jnp.stack(ssm_states), jnp.stack(ssd_rows))


def _trunk_sample(x3, W, cache_k, cache_v, conf_hist, ssm, ssd_hist):
    n, t, d = x3.shape
    x = x3.reshape(n, d)
    expand = jnp.repeat(jnp.eye(LANES, SSD_HEADS, dtype=F32), SSD_HEAD_DIM, axis=1).astype(BF16)
    win_k, win_v, conf_rows, ssd_rows = [], [], [], []
    ssm_out = None
    head_group = jnp.arange(N_Q_HEADS) // Q_PER_KV
    for layer in range(4):
        w = W[layer]
        g = w["g"]
        i = layer // 2
        if layer % 2 == 0:
            u = _norm_matmul(x, g[0], w["w_in"], n=AB_IN, tm=n, tn=AB_IN // 2)
            c_out, a = _conf_sample(u, (conf_hist, i), w["dw"], w["db"], w["lng"], w["lnb"], bb=64)
            k_off = GLU_W + Q_W
            q = u[:, GLU_W:k_off].reshape(n, N_Q_HEADS, HEAD_DIM)
            k_new = u[:, k_off:k_off + KV_W]
            v_new = u[:, k_off + KV_W:]
            lane_group = jnp.arange(KV_W) // HEAD_DIM
            qm = jnp.where(head_group[None, :, None] == lane_group[None, None, :],
                           jnp.tile(q, (1, 1, N_KV_HEADS)), 0.0)
            o = _attn_sample(qm, (cache_k, i), (cache_v, i), k_new[:, None, :], v_new[:, None, :],
                             w["sinks"][:, None], bb=64)
            o = o.reshape(n, N_Q_HEADS, N_KV_HEADS, HEAD_DIM)
            a_out = jnp.take_along_axis(o, head_group[None, :, None, None], axis=2).reshape(n, Q_W)
            x = _proj_norm_res(jnp.concatenate([c_out, a_out], axis=-1), w["w_out"], x, g[1], tm=n)
            win_k.append(k_new.reshape(n, 1, N_KV_HEADS, HEAD_DIM))
            win_v.append(v_new.reshape(n, 1, N_KV_HEADS, HEAD_DIM))
            conf_rows.append(a[:, None, :])
        else:
            uzx = _norm_matmul(x, g[0], w["w_zx"], n=SSD_ZX, tm=n, tn=1024)
            udt = _norm_matmul(x, g[0], w["w_dt"], n=LANES, tm=n, tn=LANES)
            xbc, xdt, da = _ssd_sample_pre(uzx, udt, (ssd_hist, i), w["cw"], w["cb"], w["dtb"], w["alog"], expand,
                                           bb=64)
            ssm_out, yt = _ssd_sample_state(da, ssm.reshape(ssm.shape[0], n, SSD_INNER, SSD_STATE), xdt, xbc,
                                            ssm_out, layer_idx=i, bb=8)
            y = _ssd_sample_post(yt, xbc, uzx, w["dskip"], w["ng"])
            x = _proj_norm_res(y, w["w_out"], x, g[1], tm=n)
            ssd_rows.append(uzx[:, None, SSD_INNER:])
        x = _mlp(x, g[2], w["w_up"], w["w_down"], g[3], tm=n)
    return (x.reshape(n, t, d), jnp.stack(win_k), jnp.stack(win_v), jnp.stack(conf_rows),
            ssm_out.reshape(ssm.shape), jnp.stack(ssd_rows))


def kernel(x_prompt, x_sample, cache_win_k, cache_win_v, state_conf_conv, state_ssm, state_ssd_conv,
           norm_g, ab_w_in, conf_dw_w, conf_dw_b, conf_ln_g, conf_ln_b, attn_sinks, ab_w_out,
           ssd_w_in, ssd_conv_w, ssd_conv_b, ssd_dt_bias, ssd_a_log, ssd_d, ssd_norm_g, ssd_w_out,
           mlp_w_up, mlp_w_down):
    P = {"norm_g": norm_g, "ab_w_in": ab_w_in, "conf_dw_w": conf_dw_w, "conf_dw_b": conf_dw_b,
         "conf_ln_g": conf_ln_g, "conf_ln_b": conf_ln_b, "attn_sinks": attn_sinks, "ab_w_out": ab_w_out,
         "ssd_w_in": ssd_w_in, "ssd_conv_w": ssd_conv_w, "ssd_conv_b": ssd_conv_b, "ssd_dt_bias": ssd_dt_bias,
         "ssd_a_log": ssd_a_log, "ssd_d": ssd_d, "ssd_norm_g": ssd_norm_g, "ssd_w_out": ssd_w_out,
         "mlp_w_up": mlp_w_up, "mlp_w_down": mlp_w_down}
    W = _prep_weights(P)
    y_p, wk_p, wv_p, cc_p, ssm_p, sc_p = _trunk_prompt(x_prompt, W)
    cache_k = jnp.swapaxes(cache_win_k.reshape(cache_win_k.shape[:3] + (KV_W,)), 2, 3)
    cache_v = jnp.swapaxes(cache_win_v.reshape(cache_win_v.shape[:3] + (KV_W,)), 2, 3)
    y_s, wk_s, wv_s, cc_s, ssm_s, sc_s = _trunk_sample(x_sample, W, cache_k, cache_v,
                                                       jnp.swapaxes(state_conf_conv, 1, 2), state_ssm,
                                                       jnp.swapaxes(state_ssd_conv, 1, 2))
    return (y_p, y_s, wk_p, wv_p, wk_s, wv_s, cc_p, cc_s, ssm_p, ssm_s, sc_p, sc_s)
```
